```python
import math
import jax, jax.numpy as jnp
from jax import lax
import numpy as np

D_MODEL = 2048
BATCH = 4
SEQ = 8192
DEPTH = 1
DEC_BATCH = 16
DEC_SEQ = 16
PAST_LEN = 4096

CHUNK = 64
D_MIX = D_MODEL
MLA_HEADS = D_MIX // 256
QK_NOPE = 128
QK_ROPE = 64
QK_DIM = QK_NOPE + QK_ROPE
V_DIM = 128
Q_LORA = 512
KV_LORA = 512
ROPE_THETA = 10000.0
ATTN_SCALE = QK_DIM ** -0.5
Q_BLOCK = 128
D_ATTN = MLA_HEADS * V_DIM
D_SSM = D_MIX - D_ATTN
SSM_HEAD_DIM = 64
SSM_HEADS = D_SSM // SSM_HEAD_DIM
SSM_GROUPS = 2
HEADS_PER_GROUP = SSM_HEADS // SSM_GROUPS
D_STATE = 128
CONV_K = 4
CONV_CH = D_SSM + 2 * SSM_GROUPS * D_STATE
SSD_CHUNK = CHUNK
SPLITS = (Q_LORA, Q_LORA + KV_LORA, Q_LORA + KV_LORA + QK_ROPE,
          Q_LORA + KV_LORA + QK_ROPE + D_SSM, Q_LORA + KV_LORA + QK_ROPE + D_SSM + CONV_CH)
D_IN = Q_LORA + KV_LORA + QK_ROPE + D_SSM + CONV_CH + SSM_HEADS
N_EXPERT_GROUPS = 8
EXPERTS_PER_GROUP = 8
N_EXPERTS = N_EXPERT_GROUPS * EXPERTS_PER_GROUP
TOP_K = 2
D_EXPERT = 512
ROW_BLOCK = 128
ALPHA = (2 * DEPTH) ** 0.25
BETA = (8 * DEPTH) ** -0.25
RMS_EPS = 1e-6
LN_EPS = 1e-5

kernel_name = 'hymba_mla_ssd_hmoe_stream_step'


def _rmsnorm(x, g):
    x32 = x.astype(jnp.float32)
    y = x32 * lax.rsqrt(jnp.mean(x32 * x32, -1, keepdims=True) + RMS_EPS)
    return (y * g.astype(jnp.float32)).astype(x.dtype)


def _layernorm(x, g, b):
    x32 = x.astype(jnp.float32)
    mu = jnp.mean(x32, -1, keepdims=True)
    xc = x32 - mu
    var = jnp.mean(xc * xc, -1, keepdims=True)
    return (xc * lax.rsqrt(var + LN_EPS) * g.astype(jnp.float32) + b.astype(jnp.float32)).astype(x.dtype)


def _rope(x, pos):
    half = QK_ROPE // 2
    inv_freq = ROPE_THETA ** (-jnp.arange(half, dtype=jnp.float32) / half)
    ang = pos.astype(jnp.float32)[:, None] * inv_freq[None, :]
    cos = jnp.cos(ang)[None, :, None, :]
    sin = jnp.sin(ang)[None, :, None, :]
    x32 = x.astype(jnp.float32)
    x1, x2 = x32[..., :half], x32[..., half:]
    return jnp.concatenate([x1 * cos - x2 * sin, x1 * sin + x2 * cos], -1).astype(x.dtype)


def _mla_prompt(q_nope, q_rope, lat, k_rope, w_uk, w_uv, pos):
    b, L, H, _ = q_nope.shape
    k_nope = jnp.einsum('btl,lhn->bthn', lat, w_uk)
    v = jnp.einsum('btl,lhv->bthv', lat, w_uv)
    k = jnp.concatenate([k_nope, jnp.broadcast_to(k_rope[:, :, None, :], (b, L, H, QK_ROPE))], -1)
    q = jnp.concatenate([q_nope, q_rope], -1)
    nb = L // Q_BLOCK
    q_blocks = q.reshape(b, nb, Q_BLOCK, H, QK_DIM).transpose(1, 0, 2, 3, 4)
    pos_blocks = pos.reshape(nb, Q_BLOCK)
    k_chunk = pos // CHUNK

    def one_block(args):
        qb, qp = args
        s = jnp.einsum('bqhd,bkhd->bhqk', qb, k).astype(jnp.float32) * ATTN_SCALE
        allowed = k_chunk[None, :] <= (qp // CHUNK)[:, None]
        s = jnp.where(allowed[None, None], s, -jnp.inf)
        p = jax.nn.softmax(s, -1).astype(v.dtype)
        return jnp.einsum('bhqk,bkhv->bqhv', p, v)

    o = lax.map(one_block, (q_blocks, pos_blocks))
    return o.transpose(1, 0, 2, 3, 4).reshape(b, L, H, V_DIM)


def _mla_cached(q_nope, q_rope, lat, k_rope, lat_past, kr_past, w_uk, w_uv, pos):
    past = lat_past.shape[1]
    L = lat.shape[1]
    lat_all = jnp.concatenate([lat_past.astype(lat.dtype), lat], 1)
    kr_all = jnp.concatenate([kr_past.astype(k_rope.dtype), k_rope], 1)
    k_pos = jnp.arange(past + L, dtype=jnp.int32)
    q_lat = jnp.einsum('bshn,lhn->bshl', q_nope, w_uk)
    s = (jnp.einsum('bshl,btl->bhst', q_lat, lat_all).astype(jnp.float32)
         + jnp.einsum('bshr,btr->bhst', q_rope, kr_all).astype(jnp.float32)) * ATTN_SCALE
    allowed = (k_pos // CHUNK)[None, :] <= (pos // CHUNK)[:, None]
    s = jnp.where(allowed[None, None], s, -jnp.inf)
    p = jax.nn.softmax(s, -1).astype(lat_all.dtype)
    o_lat = jnp.einsum('bhst,btl->bshl', p, lat_all)
    return jnp.einsum('bshl,lhv->bshv', o_lat, w_uv)


def _ssd_scan(x, dt, A, Bm, Cm, h0):
    b, L = x.shape[0], x.shape[1]
    Q = min(SSD_CHUNK, L)
    c = L // Q
    G, E, P, N = SSM_GROUPS, HEADS_PER_GROUP, SSM_HEAD_DIM, D_STATE
    xg = x.reshape(b, c, Q, G, E, P)
    dtg = dt.reshape(b, c, Q, G, E)
    Bg = Bm.reshape(b, c, Q, G, N)
    Cg = Cm.reshape(b, c, Q, G, N)
    a_cs = jnp.cumsum(dtg * A.reshape(G, E), axis=2)
    diff = a_cs[:, :, :, None] - a_cs[:, :, None, :]
    causal = jnp.tril(jnp.ones((Q, Q), dtype=bool))[:, :, None, None]
    decay = jnp.exp(jnp.where(causal, diff, -jnp.inf))
    cb = jnp.einsum('bcign,bcjgn->bcijg', Cg, Bg)
    w_diag = cb[..., None] * decay * dtg[:, :, None]
    y_diag = jnp.einsum('bcijge,bcjgep->bcigep', w_diag, xg)
    decay_end = jnp.exp(a_cs[:, :, -1:] - a_cs)
    xw = xg * (decay_end * dtg)[..., None]
    states = jnp.einsum('bcjgn,bcjgep->bcgepn', Bg, xw)
    chunk_decay = jnp.exp(a_cs[:, :, -1])

    def step(h, inp):
        s_c, d_c = inp
        return h * d_c[..., None, None] + s_c, h

    h_last, h_prev = lax.scan(step, h0.reshape(b, G, E, P, N),
                              (states.swapaxes(0, 1), chunk_decay.swapaxes(0, 1)))
    h_prev = h_prev.swapaxes(0, 1)
    y_off = jnp.einsum('bcign,bcgepn->bcigep', Cg, h_prev) * jnp.exp(a_cs)[..., None]
    y = (y_diag + y_off).reshape(b, L, SSM_HEADS, P)
    return y, h_last.reshape(b, SSM_HEADS, P, N)


def _ssd_mixer(z, xbc, dt_raw, conv_prev, ssm_prev, conv_w, conv_b, dt_bias, a_log, d_skip, ssm_norm):
    b, L, _ = xbc.shape
    ext = jnp.concatenate([conv_prev.astype(xbc.dtype), xbc], 1)
    conv = lax.conv_general_dilated(ext, conv_w[:, None, :].astype(ext.dtype), (1,), 'VALID',
                                    dimension_numbers=('NWC', 'WIO', 'NWC'),
                                    feature_group_count=CONV_CH)
    u = jax.nn.silu((conv + conv_b).astype(jnp.float32))
    xs, Bm, Cm = jnp.split(u, (D_SSM, D_SSM + SSM_GROUPS * D_STATE), axis=-1)
    xs = xs.reshape(b, L, SSM_HEADS, SSM_HEAD_DIM)
    Bm = Bm.reshape(b, L, SSM_GROUPS, D_STATE)
    Cm = Cm.reshape(b, L, SSM_GROUPS, D_STATE)
    dt = jax.nn.softplus(dt_raw.astype(jnp.float32) + dt_bias.astype(jnp.float32))
    A = -jnp.exp(a_log.astype(jnp.float32))
    y, h_new = _ssd_scan(xs, dt, A, Bm, Cm, ssm_prev.astype(jnp.float32))
    y = y + d_skip.astype(jnp.float32)[:, None] * xs
    y = y.reshape(b, L, D_SSM) * jax.nn.silu(z.astype(jnp.float32))
    yg = y.reshape(b, L, SSM_GROUPS, D_SSM // SSM_GROUPS)
    yg = yg * lax.rsqrt(jnp.mean(yg * yg, -1, keepdims=True) + RMS_EPS)
    y = yg.reshape(b, L, D_SSM) * ssm_norm.astype(jnp.float32)
    return y.astype(z.dtype), ext[:, L:], h_new


def _token_mixer(x, pos, lat_past, kr_past, conv_prev, ssm_prev, w_in, q_norm, w_uq, kv_norm, w_uk,
                 w_uv, conv_w, conv_b, dt_bias, a_log, d_skip, ssm_norm, w_o):
    b, L, _ = x.shape
    proj = x @ w_in
    c_q, c_kv, k_r, z, xbc, dt_raw = jnp.split(proj, SPLITS, axis=-1)
    q = (_rmsnorm(c_q, q_norm) @ w_uq).reshape(b, L, MLA_HEADS, QK_DIM)
    q_nope = q[..., :QK_NOPE]
    q_rope = _rope(q[..., QK_NOPE:], pos)
    lat = _rmsnorm(c_kv, kv_norm)
    k_rope = _rope(k_r[:, :, None, :], pos)[:, :, 0, :]
    if lat_past is None:
        attn = _mla_prompt(q_nope, q_rope, lat, k_rope, w_uk, w_uv, pos)
        conv_prev = jnp.zeros((b, CONV_K - 1, CONV_CH), x.dtype)
        ssm_prev = jnp.zeros((b, SSM_HEADS, SSM_HEAD_DIM, D_STATE), jnp.float32)
    else:
        attn = _mla_cached(q_nope, q_rope, lat, k_rope, lat_past, kr_past, w_uk, w_uv, pos)
    y_ssm, conv_new, ssm_new = _ssd_mixer(z, xbc, dt_raw, conv_prev, ssm_prev, conv_w, conv_b,
                                          dt_bias, a_log, d_skip, ssm_norm)
    mixed = jnp.concatenate([attn.reshape(b, L, D_ATTN), y_ssm], -1) @ w_o
    return mixed, lat, k_rope, conv_new, ssm_new.astype(x.dtype)


def _hmoe(h, w_rg, b_rg, w_re, b_re, w_gate, w_up, w_down):
    b, L, D = h.shape
    T = b * L
    xt = h.reshape(T, D)
    tok_ids = jnp.arange(T, dtype=jnp.int32)
    g_logit = (xt @ w_rg).astype(jnp.float32) + b_rg.astype(jnp.float32)
    g_prob = jax.nn.softmax(g_logit, -1)
    grp = jnp.argmax(g_logit, -1).astype(jnp.int32)
    g_w = g_prob[tok_ids, grp]
    e_logit = ((xt @ w_re).astype(jnp.float32) + b_re.astype(jnp.float32)).reshape(
        T, N_EXPERT_GROUPS, EXPERTS_PER_GROUP)[tok_ids, grp]
    top_p, top_i = lax.top_k(jax.nn.softmax(e_logit, -1), TOP_K)
    gate = g_w[:, None] * top_p / jnp.sum(top_p, -1, keepdims=True)
    expert = grp[:, None] * EXPERTS_PER_GROUP + top_i.astype(jnp.int32)
    A = T * TOP_K
    e_flat = expert.reshape(A)
    tok_flat = jnp.repeat(tok_ids, TOP_K)
    w_flat = gate.reshape(A)
    order = jnp.argsort(e_flat)
    e_s, tok_s, w_s = e_flat[order], tok_flat[order], w_flat[order]
    counts = jnp.bincount(e_flat, length=N_EXPERTS).astype(jnp.int32)
    start = jnp.cumsum(counts) - counts
    pcounts = (counts + ROW_BLOCK - 1) // ROW_BLOCK * ROW_BLOCK
    pend = jnp.cumsum(pcounts)
    pstart = pend - pcounts
    row_pos = pstart[e_s] + jnp.arange(A, dtype=jnp.int32) - start[e_s]
    n_blk = (A + N_EXPERTS * (ROW_BLOCK - 1) + ROW_BLOCK - 1) // ROW_BLOCK
    n_rows = n_blk * ROW_BLOCK
    row_tok = jnp.zeros((n_rows,), jnp.int32).at[row_pos].set(tok_s)
    row_w = jnp.zeros((n_rows,), jnp.float32).at[row_pos].set(w_s)
    blk_e = jnp.clip(jnp.searchsorted(pend, jnp.arange(n_blk, dtype=jnp.int32) * ROW_BLOCK,
                                      side='right'), 0, N_EXPERTS - 1).astype(jnp.int32)

    def body(out, blk):
        tok, w, e = blk
        xb = xt[tok]
        hid = jax.nn.silu(xb @ w_gate[e]) * (xb @ w_up[e])
        yb = (hid @ w_down[e]) * w[:, None].astype(xb.dtype)
        return out.at[tok].add(yb.astype(out.dtype)), None

    out, _ = lax.scan(body, jnp.zeros_like(xt),
                      (row_tok.reshape(n_blk, ROW_BLOCK), row_w.reshape(n_blk, ROW_BLOCK), blk_e))
    return out.reshape(b, L, D)


def _layer(x, pos, lat_past, kr_past, conv_prev, ssm_prev, w_in, q_norm, w_uq, kv_norm, w_uk, w_uv,
           conv_w, conv_b, dt_bias, a_log, d_skip, ssm_norm, w_o, ln1_g, ln1_b, w_rg, b_rg, w_re,
           b_re, w_gate, w_up, w_down, ln2_g, ln2_b):
    mixed, lat, kr, conv_new, ssm_new = _token_mixer(
        x, pos, lat_past, kr_past, conv_prev, ssm_prev, w_in, q_norm, w_uq, kv_norm, w_uk, w_uv,
        conv_w, conv_b, dt_bias, a_log, d_skip, ssm_norm, w_o)
    h = _layernorm(ALPHA * x + mixed, ln1_g, ln1_b)
    f = _hmoe(h, w_rg, b_rg, w_re, b_re, w_gate, w_up, w_down)
    out = _layernorm(ALPHA * h + f, ln2_g, ln2_b)
    return out, lat, kr, conv_new, ssm_new


def setup_inputs(seed: int = 0) -> dict:
    key = jax.random.key(seed)
    ks = jax.random.split(key, 40)
    f32 = jnp.float32

    def nrm(k, shape, scale):
        return jax.random.normal(k, shape, f32) * scale

    def gain(k, shape):
        return 1.0 + 0.02 * jax.random.normal(k, shape, f32)

    u = jax.random.uniform(ks[14], (DEPTH, SSM_HEADS), f32)
    dt0 = jnp.exp(u * (math.log(0.1) - math.log(0.001)) + math.log(0.001))
    dt_bias = dt0 + jnp.log(-jnp.expm1(-dt0))
    a_log = jnp.log(jax.random.uniform(ks[15], (DEPTH, SSM_HEADS), f32, 1.0, 16.0))
    return {
        'x_prompt': nrm(ks[0], (BATCH, SEQ, D_MODEL), 1.0),
        'x_sample': nrm(ks[1], (DEC_BATCH, DEC_SEQ, D_MODEL), 1.0),
        'cache_kv_latent': nrm(ks[2], (DEPTH, DEC_BATCH, PAST_LEN, KV_LORA), 1.0),
        'cache_k_rope': nrm(ks[3], (DEPTH, DEC_BATCH, PAST_LEN, QK_ROPE), 1.0),
        'state_conv': nrm(ks[4], (DEPTH, DEC_BATCH, CONV_K - 1, CONV_CH), 1.0),
        'state_ssm': nrm(ks[5], (DEPTH, DEC_BATCH, SSM_HEADS, SSM_HEAD_DIM, D_STATE), 0.5),
        'ln0_g': gain(ks[6], (D_MODEL,)),
        'ln0_b': nrm(ks[7], (D_MODEL,), 0.02),
        'w_in': nrm(ks[8], (DEPTH, D_MODEL, D_IN), D_MODEL ** -0.5),
        'q_norm': gain(ks[9], (DEPTH, Q_LORA)),
        'w_uq': nrm(ks[10], (DEPTH, Q_LORA, MLA_HEADS * QK_DIM), Q_LORA ** -0.5),
        'kv_norm': gain(ks[11], (DEPTH, KV_LORA)),
        'w_uk': nrm(ks[12], (DEPTH, KV_LORA, MLA_HEADS, QK_NOPE), KV_LORA ** -0.5),
        'w_uv': nrm(ks[13], (DEPTH, KV_LORA, MLA_HEADS, V_DIM), BETA * KV_LORA ** -0.5),
        'conv_w': nrm(ks[16], (DEPTH, CONV_K, CONV_CH), CONV_K ** -0.5),
        'conv_b': nrm(ks[17], (DEPTH, CONV_CH), 0.02),
        'dt_bias': dt_bias,
        'a_log': a_log,
        'd_skip': gain(ks[18], (DEPTH, SSM_HEADS)),
        'ssm_norm': gain(ks[19], (DEPTH, D_SSM)),
        'w_o': nrm(ks[20], (DEPTH, D_MIX, D_MODEL), BETA * D_MIX ** -0.5),
        'ln1_g': gain(ks[21], (DEPTH, D_MODEL)),
        'ln1_b': nrm(ks[22], (DEPTH, D_MODEL), 0.02),
        'w_rg': nrm(ks[23], (DEPTH, D_MODEL, N_EXPERT_GROUPS), D_MODEL ** -0.5),
        'b_rg': nrm(ks[24], (DEPTH, N_EXPERT_GROUPS), 0.01),
        'w_re': nrm(ks[25], (DEPTH, D_MODEL, N_EXPERTS), D_MODEL ** -0.5),
        'b_re': nrm(ks[26], (DEPTH, N_EXPERTS), 0.01),
        'w_gate': nrm(ks[27], (DEPTH, N_EXPERTS, D_MODEL, D_EXPERT), D_MODEL ** -0.5),
        'w_up': nrm(ks[28], (DEPTH, N_EXPERTS, D_MODEL, D_EXPERT), D_MODEL ** -0.5),
        'w_down': nrm(ks[29], (DEPTH, N_EXPERTS, D_EXPERT, D_MODEL), BETA * D_EXPERT ** -0.5),
        'ln2_g': gain(ks[30], (DEPTH, D_MODEL)),
        'ln2_b': nrm(ks[31], (DEPTH, D_MODEL), 0.02),
    }


def reference(x_prompt, x_sample, cache_kv_latent, cache_k_rope, state_conv, state_ssm, ln0_g, ln0_b,
              w_in, q_norm, w_uq, kv_norm, w_uk, w_uv, conv_w, conv_b, dt_bias, a_log, d_skip,
              ssm_norm, w_o, ln1_g, ln1_b, w_rg, b_rg, w_re, b_re, w_gate, w_up, w_down, ln2_g, ln2_b):
    seq_p = x_prompt.shape[1]
    seq_s = x_sample.shape[1]
    past = cache_kv_latent.shape[2]
    pos_p = jnp.arange(seq_p, dtype=jnp.int32)
    pos_s = past + jnp.arange(seq_s, dtype=jnp.int32)
    hp = _layernorm(x_prompt, ln0_g, ln0_b)
    hs = _layernorm(x_sample, ln0_g, ln0_b)
    lat_p, lat_s, kr_p, kr_s, cv_p, cv_s, ss_p, ss_s = [], [], [], [], [], [], [], []
    stacked = (w_in, q_norm, w_uq, kv_norm, w_uk, w_uv, conv_w, conv_b, dt_bias, a_log, d_skip,
               ssm_norm, w_o, ln1_g, ln1_b, w_rg, b_rg, w_re, b_re, w_gate, w_up, w_down, ln2_g, ln2_b)
    for l in range(DEPTH):
        lw = [p[l] for p in stacked]
        hp, a, bq, c, d = _layer(hp, pos_p, None, None, None, None, *lw)
        lat_p.append(a); kr_p.append(bq); cv_p.append(c); ss_p.append(d)
        hs, a, bq, c, d = _layer(hs, pos_s, cache_kv_latent[l], cache_k_rope[l], state_conv[l],
                                 state_ssm[l], *lw)
        lat_s.append(a); kr_s.append(bq); cv_s.append(c); ss_s.append(d)
    y_prompt = hp
    y_sample = hs
    kv_latent_prompt = jnp.stack(lat_p)
    kv_latent_sample = jnp.stack(lat_s)
    k_rope_prompt = jnp.stack(kr_p)
    k_rope_sample = jnp.stack(kr_s)
    conv_prompt = jnp.stack(cv_p)
    conv_sample = jnp.stack(cv_s)
    ssm_prompt = jnp.stack(ss_p)
    ssm_sample = jnp.stack(ss_s)
    return (y_prompt, y_sample, kv_latent_prompt, kv_latent_sample, k_rope_prompt, k_rope_sample,
            conv_prompt, conv_sample, ssm_prompt, ssm_sample)
```

```python
import functools
import math

import jax
import jax.numpy as jnp
import numpy as np
from jax import lax
from jax.experimental import pallas as pl
from jax.experimental.pallas import tpu as pltpu

F32 = jnp.float32
BF16 = jnp.bfloat16

D_MODEL = 2048
CHUNK = 64
MLA_HEADS = 8
QK_NOPE = 128
QK_ROPE = 64
QK_DIM = QK_NOPE + QK_ROPE
V_DIM = 128
Q_LORA = 512
KV_LORA = 512
ROPE_THETA = 10000.0
ATTN_SCALE = QK_DIM ** -0.5
D_ATTN = MLA_HEADS * V_DIM
D_SSM = 1024
SSM_HEAD_DIM = 64
SSM_HEADS = D_SSM // SSM_HEAD_DIM
SSM_GROUPS = 2
HEADS_PER_GROUP = SSM_HEADS // SSM_GROUPS
D_STATE = 128
CONV_K = 4
CONV_CH = D_SSM + 2 * SSM_GROUPS * D_STATE
N_EXPERT_GROUPS = 8
EXPERTS_PER_GROUP = 8
N_EXPERTS = N_EXPERT_GROUPS * EXPERTS_PER_GROUP
TOP_K = 2
D_EXPERT = 512
DEPTH = 1
ALPHA = (2 * DEPTH) ** 0.25
RMS_EPS = 1e-6
LN_EPS = 1e-5

LANES = 128
MXU_DIM = 256
VMEM_LIMIT = 56 * 1024 * 1024

QK_PAD = 2 * LANES
C_Q, C_KV, C_Z, C_XBC, C_KR, C_DT = 0, 512, 1024, 2048, 3584, 3712
D_IN_PAD = 3840


def _const_spec(shape):
    nd = len(shape)
    return pl.BlockSpec(shape, lambda *_: (0,) * nd, pipeline_mode=pl.Buffered(1))


def _split3(a):
    hi = a.astype(BF16)
    r1 = a - hi.astype(F32)
    mid = r1.astype(BF16)
    lo = (r1 - mid.astype(F32)).astype(BF16)
    return hi, mid, lo


def _dot(a, b):
    return jnp.dot(a, b, preferred_element_type=F32)


def _dot_nt(a, b):
    return lax.dot_general(a, b, (((1,), (1,)), ((), ())), preferred_element_type=F32)


def _dot_tn(a, b):
    return lax.dot_general(a, b, (((0,), (0,)), ((), ())), preferred_element_type=F32)


def _exact_dot(a_f32, sel_bf16):
    hi, mid, lo = _split3(a_f32)
    return _dot(hi, sel_bf16) + _dot(mid, sel_bf16) + _dot(lo, sel_bf16)


def _exact_dot_l(sel_bf16, a_f32):
    hi, mid, lo = _split3(a_f32)
    return _dot(sel_bf16, hi) + _dot(sel_bf16, mid) + _dot(sel_bf16, lo)


def _layernorm(x, g, b):
    mu = jnp.mean(x, axis=-1, keepdims=True)
    xc = x - mu
    var = jnp.mean(xc * xc, axis=-1, keepdims=True)
    return xc * lax.rsqrt(var + LN_EPS) * g + b


def _rmsnorm(x, g):
    return x * lax.rsqrt(jnp.mean(x * x, axis=-1, keepdims=True) + RMS_EPS) * g


def _silu(x):
    return x * (1.0 / (1.0 + jnp.exp(-x)))


def _rope128(x, cos_t, sin_t):
    lane = lax.broadcasted_iota(jnp.int32, x.shape, 1)
    partner = jnp.where(lane < QK_ROPE // 2, pltpu.roll(x, LANES - QK_ROPE // 2, 1),
                        pltpu.roll(x, QK_ROPE // 2, 1))
    return x * cos_t + partner * sin_t


def _inproj_kernel(x_ref, g0_ref, b0_ref, w_in_ref, qn_ref, kvn_ref, w_uq_ref, w_ukv_ref,
                   cos_ref, sin_ref, dtb_ref,
                   q_ref, kv_ref, lat_ref, kr_ref, krp_ref, z_ref, xbc_ref, dt_ref, *, with_kv):
    xn = _layernorm(x_ref[...], g0_ref[...], b0_ref[...]).astype(BF16)
    cos_t = cos_ref[...]
    sin_t = sin_ref[...]

    c_q = _dot(xn, w_in_ref[:, C_Q:C_Q + Q_LORA])
    qb = _rmsnorm(c_q, qn_ref[...]).astype(BF16)
    for h in range(MLA_HEADS):
        qh = _dot(qb, w_uq_ref[:, h * QK_PAD:(h + 1) * QK_PAD])
        q_ref[:, h * QK_PAD:h * QK_PAD + LANES] = qh[:, :LANES].astype(BF16)
        q_ref[:, h * QK_PAD + LANES:(h + 1) * QK_PAD] = _rope128(qh[:, LANES:], cos_t, sin_t).astype(BF16)

    c_kv = _dot(xn, w_in_ref[:, C_KV:C_KV + KV_LORA])
    lat = _rmsnorm(c_kv, kvn_ref[...])
    lat_ref[...] = lat
    if with_kv:
        kv_ref[...] = _dot(lat.astype(BF16), w_ukv_ref[...]).astype(BF16)
    else:
        kv_ref[...] = jnp.zeros(kv_ref.shape, BF16)

    k_r = _rope128(_dot(xn, w_in_ref[:, C_KR:C_KR + LANES]), cos_t, sin_t)
    kr_ref[...] = k_r[:, :QK_ROPE]
    krp_ref[...] = k_r.astype(BF16)

    z_ref[...] = _dot(xn, w_in_ref[:, C_Z:C_Z + D_SSM]).astype(BF16)
    xbc_ref[...] = _dot(xn, w_in_ref[:, C_XBC:C_XBC + CONV_CH])

    dt_raw = _dot(xn, w_in_ref[:, C_DT:C_DT + LANES]) + dtb_ref[...]
    dt = jnp.maximum(dt_raw, 0.0) + jnp.log1p(jnp.exp(-jnp.abs(dt_raw)))
    lane = lax.broadcasted_iota(jnp.int32, dt.shape, 1)
    dt_ref[...] = jnp.where(lane < SSM_HEADS, dt, 0.0)


def _inproj(x, pw, cos_t, sin_t, *, tm, pos_blocks, with_kv):
    T = x.shape[0]
    n = T // tm
    row = lambda i: (i, 0)
    pos = lambda i: (i % pos_blocks, 0)
    kv_cols = 2 * D_ATTN if with_kv else LANES
    out_shape = [
        jax.ShapeDtypeStruct((T, MLA_HEADS * QK_PAD), BF16),
        jax.ShapeDtypeStruct((T, kv_cols), BF16),
        jax.ShapeDtypeStruct((T, KV_LORA), F32),
        jax.ShapeDtypeStruct((T, QK_ROPE), F32),
        jax.ShapeDtypeStruct((T, LANES), BF16),
        jax.ShapeDtypeStruct((T, D_SSM), BF16),
        jax.ShapeDtypeStruct((T, CONV_CH), F32),
        jax.ShapeDtypeStruct((T, LANES), F32),
    ]
    out_specs = [pl.BlockSpec((tm, s.shape[1]), row) for s in out_shape]
    in_specs = [
        pl.BlockSpec((tm, D_MODEL), row),
        _const_spec((1, D_MODEL)), _const_spec((1, D_MODEL)),
        _const_spec((D_MODEL, D_IN_PAD)),
        _const_spec((1, Q_LORA)), _const_spec((1, KV_LORA)),
        _const_spec((Q_LORA, MLA_HEADS * QK_PAD)),
        _const_spec((KV_LORA, 2 * D_ATTN)),
        pl.BlockSpec((tm, LANES), pos), pl.BlockSpec((tm, LANES), pos),
        _const_spec((1, LANES)),
    ]
    return pl.pallas_call(
        functools.partial(_inproj_kernel, with_kv=with_kv),
        grid=(n,), in_specs=in_specs, out_specs=out_specs, out_shape=out_shape,
        compiler_params=pltpu.CompilerParams(dimension_semantics=("arbitrary",),
                                             vmem_limit_bytes=VMEM_LIMIT),
        name="inproj",
    )(x, pw["ln0_g"], pw["ln0_b"], pw["w_in"], pw["q_norm"], pw["kv_norm"], pw["w_uq"],
      pw["w_ukv"], cos_t, sin_t, pw["dt_bias"])


def _attn_prompt_kernel(qi_ref, ki_ref, q_ref, kv_ref, krp_ref, o_ref, m_ref, l_ref, acc_ref, *, tq):
    p_id = pl.program_id(1)
    qi = qi_ref[p_id]
    ki = ki_ref[p_id]

    @pl.when(ki == 0)
    def _():
        m_ref[...] = jnp.full(m_ref.shape, -jnp.inf, F32)
        l_ref[...] = jnp.zeros(l_ref.shape, F32)
        acc_ref[...] = jnp.zeros(acc_ref.shape, F32)

    def step(masked):
        krp = krp_ref[...]
        if masked:
            r = lax.broadcasted_iota(jnp.int32, (tq, tq), 0) // CHUNK
            c = lax.broadcasted_iota(jnp.int32, (tq, tq), 1) // CHUNK
            allowed = c <= r
        for h in range(MLA_HEADS):
            qh = q_ref[:, h * QK_PAD:(h + 1) * QK_PAD]
            kh = jnp.concatenate([kv_ref[:, h * QK_NOPE:(h + 1) * QK_NOPE], krp], axis=1)
            s = _dot_nt(qh, kh)
            if masked:
                s = jnp.where(allowed, s, -jnp.inf)
            m_old = m_ref[h]
            m_new = jnp.maximum(m_old, jnp.max(s, axis=1, keepdims=True))
            alpha = jnp.exp(m_old - m_new)
            p = jnp.exp(s - m_new)
            l_ref[h] = alpha * l_ref[h] + jnp.sum(p, axis=1, keepdims=True)
            vh = kv_ref[:, D_ATTN + h * V_DIM:D_ATTN + (h + 1) * V_DIM]
            acc_ref[h] = alpha * acc_ref[h] + _dot(p.astype(BF16), vh)
            m_ref[h] = m_new

    @pl.when(ki < qi)
    def _():
        step(False)

    @pl.when(ki == qi)
    def _():
        step(True)
        for h in range(MLA_HEADS):
            o_ref[:, h * V_DIM:(h + 1) * V_DIM] = (acc_ref[h] / l_ref[h]).astype(BF16)


def _attn_prompt(q, kv, krp, *, tq):
    B, L, _ = q.shape
    nq = L // tq
    pairs = [(i, j) for i in range(nq) for j in range(i + 1)]
    qi_tab = jnp.asarray(np.array([p[0] for p in pairs], np.int32))
    ki_tab = jnp.asarray(np.array([p[1] for p in pairs], np.int32))
    grid_spec = pltpu.PrefetchScalarGridSpec(
        num_scalar_prefetch=2,
        grid=(B, len(pairs)),
        in_specs=[
            pl.BlockSpec((None, tq, MLA_HEADS * QK_PAD), lambda b, p, qi, ki: (b, qi[p], 0)),
            pl.BlockSpec((None, tq, 2 * D_ATTN), lambda b, p, qi, ki: (b, ki[p], 0)),
            pl.BlockSpec((None, tq, LANES), lambda b, p, qi, ki: (b, ki[p], 0)),
        ],
        out_specs=pl.BlockSpec((None, tq, D_ATTN), lambda b, p, qi, ki: (b, qi[p], 0)),
        scratch_shapes=[pltpu.VMEM((MLA_HEADS, tq, 1), F32), pltpu.VMEM((MLA_HEADS, tq, 1), F32),
                        pltpu.VMEM((MLA_HEADS, tq, V_DIM), F32)],
    )
    return pl.pallas_call(
        functools.partial(_attn_prompt_kernel, tq=tq),
        grid_spec=grid_spec,
        out_shape=jax.ShapeDtypeStruct((B, L, D_ATTN), BF16),
        compiler_params=pltpu.CompilerParams(dimension_semantics=("arbitrary", "arbitrary"),
                                             vmem_limit_bytes=VMEM_LIMIT),
        name="attn_prompt",
    )(qi_tab, ki_tab, q, kv, krp)


def _attn_sample_kernel(q_ref, wukt_ref, wuv_ref, latp_ref, krpast_ref, latn_ref, krn_ref, o_ref,
                        qlat_ref, qrp_ref, m_ref, l_ref, acc_ref, *, ls, past, tk, n_new_pad):
    k = pl.program_id(1)
    nk = pl.num_programs(1)
    rows = MLA_HEADS * ls

    @pl.when(k == 0)
    def _():
        for h in range(MLA_HEADS):
            qn = q_ref[:, h * QK_PAD:h * QK_PAD + LANES]
            qlat_ref[h * ls:(h + 1) * ls, :] = _dot(qn, wukt_ref[h]).astype(BF16)
            qrp_ref[h * ls:(h + 1) * ls, :] = q_ref[:, h * QK_PAD + LANES:(h + 1) * QK_PAD]
        m_ref[...] = jnp.full(m_ref.shape, -jnp.inf, F32)
        l_ref[...] = jnp.zeros(l_ref.shape, F32)
        acc_ref[...] = jnp.zeros(acc_ref.shape, F32)

    q_chunk = (past + lax.broadcasted_iota(jnp.int32, (rows, 1), 0) % ls) // CHUNK

    def update(s, lat_b):
        m_old = m_ref[...]
        m_new = jnp.maximum(m_old, jnp.max(s, axis=1, keepdims=True))
        alpha = jnp.exp(m_old - m_new)
        p = jnp.exp(s - m_new)
        l_ref[...] = alpha * l_ref[...] + jnp.sum(p, axis=1, keepdims=True)
        acc_ref[...] = alpha * acc_ref[...] + _dot(p.astype(BF16), lat_b)
        m_ref[...] = m_new

    lat_b = latp_ref[...].astype(BF16)
    kr_b = krpast_ref[...].astype(BF16)
    s = _dot_nt(qlat_ref[...], lat_b) + _dot_nt(qrp_ref[:, :QK_ROPE], kr_b)
    k_pos = k * tk + lax.broadcasted_iota(jnp.int32, (rows, tk), 1)
    s = jnp.where(k_pos // CHUNK <= q_chunk, s, -jnp.inf)
    update(s, lat_b)

    @pl.when(k == nk - 1)
    def _():
        latn_b = latn_ref[...].astype(BF16)
        s2 = _dot_nt(qlat_ref[...], latn_b) + _dot_nt(qrp_ref[...], krn_ref[...])
        j = lax.broadcasted_iota(jnp.int32, (rows, n_new_pad), 1)
        ok = ((past + j) // CHUNK <= q_chunk) & (j < ls)
        update(jnp.where(ok, s2, -jnp.inf), latn_b)
        o_lat = (acc_ref[...] / l_ref[...]).astype(BF16)
        for h in range(MLA_HEADS):
            o_ref[:, h * V_DIM:(h + 1) * V_DIM] = _dot(o_lat[h * ls:(h + 1) * ls, :], wuv_ref[h]).astype(BF16)


def _attn_sample(q, pw, lat_past, kr_past, lat_new, krp_new, *, tk):
    Bs, Ls, _ = q.shape
    past = lat_past.shape[1]
    n_new_pad = LANES
    lat_new = jnp.pad(lat_new, ((0, 0), (0, n_new_pad - Ls), (0, 0)))
    krp_new = jnp.pad(krp_new, ((0, 0), (0, n_new_pad - Ls), (0, 0)))
    rows = MLA_HEADS * Ls
    return pl.pallas_call(
        functools.partial(_attn_sample_kernel, ls=Ls, past=past, tk=tk, n_new_pad=n_new_pad),
        grid=(Bs, past // tk),
        in_specs=[
            pl.BlockSpec((None, Ls, MLA_HEADS * QK_PAD), lambda b, k: (b, 0, 0)),
            _const_spec((MLA_HEADS, QK_NOPE, KV_LORA)),
            _const_spec((MLA_HEADS, KV_LORA, V_DIM)),
            pl.BlockSpec((None, tk, KV_LORA), lambda b, k: (b, k, 0)),
            pl.BlockSpec((None, tk, QK_ROPE), lambda b, k: (b, k, 0)),
            pl.BlockSpec((None, n_new_pad, KV_LORA), lambda b, k: (b, 0, 0)),
            pl.BlockSpec((None, n_new_pad, LANES), lambda b, k: (b, 0, 0)),
        ],
        out_specs=pl.BlockSpec((None, Ls, D_ATTN), lambda b, k: (b, 0, 0)),
        out_shape=jax.ShapeDtypeStruct((Bs, Ls, D_ATTN), BF16),
        scratch_shapes=[pltpu.VMEM((rows, KV_LORA), BF16), pltpu.VMEM((rows, LANES), BF16),
                        pltpu.VMEM((rows, 1), F32), pltpu.VMEM((rows, 1), F32),
                        pltpu.VMEM((rows, KV_LORA), F32)],
        compiler_params=pltpu.CompilerParams(dimension_semantics=("arbitrary", "arbitrary"),
                                             vmem_limit_bytes=VMEM_LIMIT),
        name="attn_sample",
    )(q, pw["w_ukt"], pw["w_uvh"], lat_past, kr_past, lat_new, krp_new)


def _ssd_consts(q):
    hb = MXU_DIM // q
    hq = SSM_HEADS * q
    lane = np.arange(hq)
    ltri = (np.arange(q)[None, :] <= np.arange(q)[:, None]).astype(np.float32)
    sel_c = (np.arange(LANES)[:, None] == (lane // q)[None, :]).astype(np.float32)
    sel_p = (np.arange(LANES)[:, None] == (np.arange(D_SSM) // SSM_HEAD_DIM)[None, :]).astype(np.float32)
    diag = (np.arange(q)[:, None] == (lane % q)[None, :]).astype(np.float32)
    causal = (np.arange(q)[:, None] >= (lane % q)[None, :]).astype(np.float32)
    bd_rows = np.arange(hb * q) // q
    bd_cols = np.arange(hb * SSM_HEAD_DIM) // SSM_HEAD_DIM
    bdmask = (bd_rows[:, None] == bd_cols[None, :]).astype(np.float32)
    return dict(ltri=jnp.asarray(ltri, BF16), sel_c=jnp.asarray(sel_c, BF16),
                sel_p=jnp.asarray(sel_p, BF16), diag=jnp.asarray(diag, F32),
                causal=jnp.asarray(causal, F32), bdmask=jnp.asarray(bdmask, BF16))


def _ssd_kernel(xbc_ref, dt_ref, z_ref, cprev_ref, h0_ref, cw_ref, cb_ref, a_ref, dskip_ref, norm_ref,
                ltri_ref, selc_ref, selp_ref, diag_ref, causal_ref, bdmask_ref,
                y_ref, cout_ref, hout_ref,
                ext_ref, u_ref, ht_ref, *, q, tt):
    t = pl.program_id(1)
    nt = pl.num_programs(1)
    hb = MXU_DIM // q
    nblk = SSM_HEADS // hb
    gw = HEADS_PER_GROUP * SSM_HEAD_DIM
    pad = 8

    @pl.when(t == 0)
    def _():
        ext_ref[0:pad, :] = jnp.zeros((pad, CONV_CH), F32)
        ext_ref[pad - (CONV_K - 1):pad, :] = cprev_ref[...]
        ht_ref[...] = h0_ref[...].T

    xbc = xbc_ref[...]
    ext_ref[pad:pad + tt, :] = xbc
    cout_ref[...] = xbc[tt - (CONV_K - 1):, :]
    conv = cb_ref[...] + cw_ref[0:1, :] * ext_ref[pad - 3:pad - 3 + tt, :]
    for k in range(1, CONV_K):
        conv = conv + cw_ref[k:k + 1, :] * ext_ref[pad - 3 + k:pad - 3 + k + tt, :]
    u_ref[...] = _silu(conv)
    ext_ref[pad - (CONV_K - 1):pad, :] = xbc[tt - (CONV_K - 1):, :]

    a_row = a_ref[...]
    d_row = dskip_ref[...]
    norm_row = norm_ref[...]

    def chunk(c, carry):
        r0 = pl.multiple_of(c * q, q)
        u = u_ref[pl.ds(r0, q), :]
        xs = u[:, :D_SSM]
        dt = dt_ref[pl.ds(r0, q), :]
        a_cs = _exact_dot_l(ltri_ref[...], dt * a_row)
        ap = _exact_dot(a_cs, selp_ref[...])
        dtp = _exact_dot(dt, selp_ref[...])
        a_i = ap if q == SSM_HEAD_DIM else _exact_dot(a_cs, selc_ref[...])
        a_j = jnp.sum(a_i * diag_ref[...], axis=0, keepdims=True)
        decay = jnp.exp(jnp.where(causal_ref[...] > 0.5, a_i - a_j, -jnp.inf))
        ap_last = ap[q - 1:q, :]
        xdt = (xs * dtp).astype(BF16)
        xw = (xs * (jnp.exp(ap_last - ap) * dtp)).astype(BF16)
        e_ap = jnp.exp(ap)

        cb_parts, y_off_parts = [], []
        for g in range(SSM_GROUPS):
            bm = u[:, D_SSM + g * D_STATE:D_SSM + (g + 1) * D_STATE].astype(BF16)
            cm = u[:, D_SSM + (SSM_GROUPS + g) * D_STATE:D_SSM + (SSM_GROUPS + g + 1) * D_STATE].astype(BF16)
            cb_parts.append(_dot_nt(cm, jnp.concatenate([bm] * HEADS_PER_GROUP, axis=0)))
            gl = slice(g * gw, (g + 1) * gw)
            ht_g = ht_ref[:, gl]
            y_off_parts.append(_dot(cm, ht_g.astype(BF16)) * e_ap[:, gl])
            st = _dot_tn(bm, xw[:, gl])
            ht_ref[:, gl] = ht_g * jnp.exp(ap_last[:, gl]) + st
        w_all = (jnp.concatenate(cb_parts, axis=1) * decay).astype(BF16)
        y_diag_parts = []
        for b in range(nblk):
            x_b = xdt[:, b * hb * SSM_HEAD_DIM:(b + 1) * hb * SSM_HEAD_DIM]
            bd = jnp.concatenate([x_b] * hb, axis=0) * bdmask_ref[...]
            y_diag_parts.append(_dot(w_all[:, b * MXU_DIM:(b + 1) * MXU_DIM], bd))
        y = jnp.concatenate(y_diag_parts, axis=1) + jnp.concatenate(y_off_parts, axis=1)

        y = y + d_row * xs
        y = y * _silu(z_ref[pl.ds(r0, q), :].astype(F32))
        outs = []
        for g in range(SSM_GROUPS):
            yg = y[:, g * gw:(g + 1) * gw]
            outs.append(yg * lax.rsqrt(jnp.mean(yg * yg, axis=1, keepdims=True) + RMS_EPS))
        y_ref[pl.ds(r0, q), :] = (jnp.concatenate(outs, axis=1) * norm_row).astype(BF16)
        return carry

    lax.fori_loop(0, tt // q, chunk, 0)

    @pl.when(t == nt - 1)
    def _():
        hout_ref[...] = ht_ref[...].T


def _ssd(xbc, dt, z, conv_prev, h0, pw, *, q, tt):
    B, L, _ = xbc.shape
    cs = _ssd_consts(q)
    hq = SSM_HEADS * q
    hb = MXU_DIM // q
    tile = lambda w: pl.BlockSpec((None, tt, w), lambda b, t: (b, t, 0))
    per_b = lambda s: pl.BlockSpec((None,) + s, lambda b, t: (b, 0, 0))
    return pl.pallas_call(
        functools.partial(_ssd_kernel, q=q, tt=tt),
        grid=(B, L // tt),
        in_specs=[tile(CONV_CH), tile(LANES), tile(D_SSM), per_b((CONV_K - 1, CONV_CH)),
                  per_b((D_SSM, D_STATE)),
                  _const_spec((CONV_K, CONV_CH)), _const_spec((1, CONV_CH)), _const_spec((1, LANES)),
                  _const_spec((1, D_SSM)), _const_spec((1, D_SSM)),
                  _const_spec((q, q)), _const_spec((LANES, hq)), _const_spec((LANES, D_SSM)),
                  _const_spec((q, hq)), _const_spec((q, hq)),
                  _const_spec((MXU_DIM, hb * SSM_HEAD_DIM))],
        out_specs=[tile(D_SSM), per_b((CONV_K - 1, CONV_CH)), per_b((D_SSM, D_STATE))],
        out_shape=[jax.ShapeDtypeStruct((B, L, D_SSM), BF16),
                   jax.ShapeDtypeStruct((B, CONV_K - 1, CONV_CH), F32),
                   jax.ShapeDtypeStruct((B, D_SSM, D_STATE), F32)],
        scratch_shapes=[pltpu.VMEM((8 + tt, CONV_CH), F32), pltpu.VMEM((tt, CONV_CH), F32),
                        pltpu.VMEM((D_STATE, D_SSM), F32)],
        compiler_params=pltpu.CompilerParams(dimension_semantics=("arbitrary", "arbitrary"),
                                             vmem_limit_bytes=VMEM_LIMIT),
        name="ssd",
    )(xbc, dt, z, conv_prev, h0, pw["conv_w"], pw["conv_b"], pw["a_neg"], pw["d_skip"], pw["ssm_norm"],
      cs["ltri"], cs["sel_c"], cs["sel_p"], cs["diag"], cs["causal"], cs["bdmask"])


R_OFF = N_EXPERT_GROUPS


def _outproj_kernel(x_ref, attn_ref, yssm_ref, g0_ref, b0_ref, wo_a_ref, wo_s_ref, g1_ref, b1_ref,
                    wr_hi_ref, wr_lo_ref, br_ref, h_ref, route_ref):
    xn = _layernorm(x_ref[...], g0_ref[...], b0_ref[...])
    mixed = _dot(attn_ref[...], wo_a_ref[...]) + _dot(yssm_ref[...], wo_s_ref[...])
    h = _layernorm(ALPHA * xn + mixed, g1_ref[...], b1_ref[...])
    h_ref[...] = h

    h_hi = h.astype(BF16)
    h_lo = (h - h_hi.astype(F32)).astype(BF16)
    lg = (_dot(h_hi, wr_hi_ref[...]) + _dot(h_hi, wr_lo_ref[...]) + _dot(h_lo, wr_hi_ref[...])
          + br_ref[...])
    lane = lax.broadcasted_iota(jnp.int32, lg.shape, 1)
    big = jnp.int32(1 << 20)
    gl = jnp.where(lane < N_EXPERT_GROUPS, lg, -jnp.inf)
    gmax = jnp.max(gl, axis=1, keepdims=True)
    grp = jnp.min(jnp.where(gl == gmax, lane, big), axis=1, keepdims=True)
    g_w = 1.0 / jnp.sum(jnp.exp(gl - gmax), axis=1, keepdims=True)
    in_grp = (lane >= R_OFF) & (lane < R_OFF + N_EXPERTS) & ((lane - R_OFF) // EXPERTS_PER_GROUP == grp)
    el = jnp.where(in_grp, lg, -jnp.inf)
    emax = jnp.max(el, axis=1, keepdims=True)
    ee = jnp.exp(el - emax)
    prob = jnp.where(in_grp, ee / jnp.sum(ee, axis=1, keepdims=True), -1.0)
    p1 = jnp.max(prob, axis=1, keepdims=True)
    i1 = jnp.min(jnp.where(prob == p1, lane, big), axis=1, keepdims=True)
    prob2 = jnp.where(lane == i1, -1.0, prob)
    p2 = jnp.max(prob2, axis=1, keepdims=True)
    i2 = jnp.min(jnp.where(prob2 == p2, lane, big), axis=1, keepdims=True)
    denom = p1 + p2
    route = jnp.where(lane == 0, (i1 - R_OFF).astype(F32),
                      jnp.where(lane == 1, (i2 - R_OFF).astype(F32),
                                jnp.where(lane == 2, g_w * p1 / denom,
                                          jnp.where(lane == 3, g_w * p2 / denom, 0.0))))
    route_ref[...] = route


def _outproj(x, attn, yssm, pw, *, tm):
    T = x.shape[0]
    row = lambda i: (i, 0)
    return pl.pallas_call(
        _outproj_kernel,
        grid=(T // tm,),
        in_specs=[pl.BlockSpec((tm, D_MODEL), row), pl.BlockSpec((tm, D_ATTN), row),
                  pl.BlockSpec((tm, D_SSM), row),
                  _const_spec((1, D_MODEL)), _const_spec((1, D_MODEL)),
                  _const_spec((D_ATTN, D_MODEL)), _const_spec((D_SSM, D_MODEL)),
                  _const_spec((1, D_MODEL)), _const_spec((1, D_MODEL)),
                  _const_spec((D_MODEL, LANES)), _const_spec((D_MODEL, LANES)), _const_spec((1, LANES))],
        out_specs=[pl.BlockSpec((tm, D_MODEL), row), pl.BlockSpec((tm, LANES), row)],
        out_shape=[jax.ShapeDtypeStruct((T, D_MODEL), F32), jax.ShapeDtypeStruct((T, LANES), F32)],
        compiler_params=pltpu.CompilerParams(dimension_semantics=("arbitrary",),
                                             vmem_limit_bytes=VMEM_LIMIT),
        name="outproj",
    )(x, attn, yssm, pw["ln0_g"], pw["ln0_b"], pw["w_o_a"], pw["w_o_s"], pw["ln1_g"], pw["ln1_b"],
      pw["w_r_hi"], pw["w_r_lo"], pw["b_r"])


def _experts_kernel(blk_e_ref, n_used_ref, tok_ref, tok_next_ref, h_hbm, roww_ref, wg_ref, wu_ref, wd_ref,
                    y_ref, xbuf, sem, wg_b, wu_b, wd_b, prev_e, *, rb):
    i = pl.program_id(0)
    n_used = n_used_ref[0]
    slot = i % 2

    def row_copy(tok, dst_slot, r):
        return pltpu.make_async_copy(h_hbm.at[pl.ds(tok, 1)], xbuf.at[dst_slot, pl.ds(r, 1)],
                                     sem.at[dst_slot])

    def issue(idx_ref, dst_slot):
        def body(r, c):
            row_copy(idx_ref[r], dst_slot, r).start()
            return c
        lax.fori_loop(0, rb, body, 0)

    @pl.when(i == 0)
    def _():
        prev_e[0] = -1
        issue(tok_ref, 0)

    @pl.when(i + 1 < n_used)
    def _():
        issue(tok_next_ref, 1 - slot)

    @pl.when(i < n_used)
    def _():
        pltpu.make_async_copy(h_hbm.at[pl.ds(0, rb)], xbuf.at[slot], sem.at[slot]).wait()
        e = blk_e_ref[i]

        @pl.when(e != prev_e[0])
        def _():
            wg_b[...] = wg_ref[...].astype(BF16)
            wu_b[...] = wu_ref[...].astype(BF16)
            wd_b[...] = wd_ref[...].astype(BF16)
            prev_e[0] = e

        xb = xbuf[slot].astype(BF16)
        hid = (_silu(_dot(xb, wg_b[...])) * _dot(xb, wu_b[...])).astype(BF16)
        y_ref[...] = _dot(hid, wd_b[...]) * roww_ref[...]

    @pl.when(i >= n_used)
    def _():
        y_ref[...] = jnp.zeros(y_ref.shape, F32)


def _experts(h_all, row_tok, row_w, blk_e, n_used, w_gate, w_up, w_down, *, rb):
    n_blk = blk_e.shape[0]
    n_rows = n_blk * rb
    last = n_blk - 1
    grid_spec = pltpu.PrefetchScalarGridSpec(
        num_scalar_prefetch=2,
        grid=(n_blk,),
        in_specs=[
            pl.BlockSpec((rb,), lambda i, be, nu: (jnp.minimum(i, nu[0] - 1),), memory_space=pltpu.SMEM),
            pl.BlockSpec((rb,), lambda i, be, nu: (jnp.minimum(i + 1, nu[0] - 1),), memory_space=pltpu.SMEM),
            pl.BlockSpec(memory_space=pl.ANY),
            pl.BlockSpec((rb, 1), lambda i, be, nu: (jnp.minimum(i, nu[0] - 1), 0)),
            pl.BlockSpec((None, D_MODEL, D_EXPERT), lambda i, be, nu: (be[jnp.minimum(i, nu[0] - 1)], 0, 0)),
            pl.BlockSpec((None, D_MODEL, D_EXPERT), lambda i, be, nu: (be[jnp.minimum(i, nu[0] - 1)], 0, 0)),
            pl.BlockSpec((None, D_EXPERT, D_MODEL), lambda i, be, nu: (be[jnp.minimum(i, nu[0] - 1)], 0, 0)),
        ],
        out_specs=pl.BlockSpec((rb, D_MODEL), lambda i, be, nu: (i, 0)),
        scratch_shapes=[pltpu.VMEM((2, rb, D_MODEL), F32), pltpu.SemaphoreType.DMA((2,)),
                        pltpu.VMEM((D_MODEL, D_EXPERT), BF16), pltpu.VMEM((D_MODEL, D_EXPERT), BF16),
                        pltpu.VMEM((D_EXPERT, D_MODEL), BF16), pltpu.SMEM((1,), jnp.int32)],
    )
    del last
    return pl.pallas_call(
        functools.partial(_experts_kernel, rb=rb),
        grid_spec=grid_spec,
        out_shape=jax.ShapeDtypeStruct((n_rows, D_MODEL), F32),
        compiler_params=pltpu.CompilerParams(dimension_semantics=("arbitrary",),
                                             vmem_limit_bytes=VMEM_LIMIT),
        name="experts",
    )(blk_e, n_used, row_tok, row_tok, h_all, row_w, w_gate, w_up, w_down)


def _combine_kernel(pos_ref, pos_next_ref, y_hbm, h_ref, g2_ref, b2_ref, op_ref, os_ref, ybuf, sem,
                    *, tm, n_prompt_tiles):
    i = pl.program_id(0)
    n = pl.num_programs(0)
    slot = i % 2
    rows = TOP_K * tm

    def issue(idx_ref, dst_slot):
        def body(r, c):
            pltpu.make_async_copy(y_hbm.at[pl.ds(idx_ref[r], 1)], ybuf.at[dst_slot, pl.ds(r, 1)],
                                  sem.at[dst_slot]).start()
            return c
        lax.fori_loop(0, rows, body, 0)

    @pl.when(i == 0)
    def _():
        issue(pos_ref, 0)

    @pl.when(i + 1 < n)
    def _():
        issue(pos_next_ref, 1 - slot)

    pltpu.make_async_copy(y_hbm.at[pl.ds(0, rows)], ybuf.at[slot], sem.at[slot]).wait()
    f = ybuf[slot, 0:tm, :] + ybuf[slot, tm:rows, :]
    out = _layernorm(ALPHA * h_ref[...] + f, g2_ref[...], b2_ref[...])

    @pl.when(i < n_prompt_tiles)
    def _():
        op_ref[...] = out

    @pl.when(i >= n_prompt_tiles)
    def _():
        os_ref[...] = out


def _combine(y_rows, pos_tiles, h_all, pw, *, tm, t_prompt):
    T = h_all.shape[0]
    n = T // tm
    npt = t_prompt // tm
    rows = TOP_K * tm
    return pl.pallas_call(
        functools.partial(_combine_kernel, tm=tm, n_prompt_tiles=npt),
        grid=(n,),
        in_specs=[
            pl.BlockSpec((rows,), lambda i: (i,), memory_space=pltpu.SMEM),
            pl.BlockSpec((rows,), lambda i: (jnp.minimum(i + 1, n - 1),), memory_space=pltpu.SMEM),
            pl.BlockSpec(memory_space=pl.ANY),
            pl.BlockSpec((tm, D_MODEL), lambda i: (i, 0)),
            _const_spec((1, D_MODEL)), _const_spec((1, D_MODEL)),
        ],
        out_specs=[pl.BlockSpec((tm, D_MODEL), lambda i: (jnp.minimum(i, npt - 1), 0)),
                   pl.BlockSpec((tm, D_MODEL), lambda i: (jnp.maximum(i - npt, 0), 0))],
        out_shape=[jax.ShapeDtypeStruct((t_prompt, D_MODEL), F32),
                   jax.ShapeDtypeStruct((T - t_prompt, D_MODEL), F32)],
        scratch_shapes=[pltpu.VMEM((2, rows, D_MODEL), F32), pltpu.SemaphoreType.DMA((2,))],
        compiler_params=pltpu.CompilerParams(dimension_semantics=("arbitrary",),
                                             vmem_limit_bytes=VMEM_LIMIT),
        name="combine",
    )(pos_tiles, pos_tiles, y_rows, h_all, pw["ln2_g"], pw["ln2_b"])


def _prep_weights(ln0_g, ln0_b, w_in, q_norm, w_uq, kv_norm, w_uk, w_uv, conv_w, conv_b, dt_bias, a_log,
                  d_skip, ssm_norm, w_o, ln1_g, ln1_b, w_rg, b_rg, w_re, b_re, ln2_g, ln2_b):
    s_q, s_kv, s_kr, s_z, s_xbc = Q_LORA, Q_LORA + KV_LORA, Q_LORA + KV_LORA + QK_ROPE, \
        Q_LORA + KV_LORA + QK_ROPE + D_SSM, Q_LORA + KV_LORA + QK_ROPE + D_SSM + CONV_CH
    zc = lambda n: jnp.zeros((D_MODEL, n), F32)
    w_in_p = jnp.concatenate([
        w_in[:, :s_q], w_in[:, s_q:s_kv], w_in[:, s_kr:s_z], w_in[:, s_z:s_xbc],
        w_in[:, s_kv:s_kr], zc(LANES - QK_ROPE), w_in[:, s_xbc:], zc(LANES - SSM_HEADS)], axis=1)
    wq = w_uq.reshape(Q_LORA, MLA_HEADS, QK_DIM) * ATTN_SCALE
    wq = jnp.concatenate([wq, jnp.zeros((Q_LORA, MLA_HEADS, QK_PAD - QK_DIM), F32)], axis=2)
    w_ukv = jnp.concatenate([w_uk.reshape(KV_LORA, D_ATTN), w_uv.reshape(KV_LORA, D_ATTN)], axis=1)
    w_r = jnp.concatenate([w_rg, w_re, jnp.zeros((D_MODEL, LANES - R_OFF - N_EXPERTS), F32)], axis=1)
    w_r_hi = w_r.astype(BF16)
    row = lambda v: v.reshape(1, -1)
    pad_row = lambda v: jnp.pad(v, (0, LANES - v.shape[0])).reshape(1, LANES)
    return dict(
        ln0_g=row(ln0_g), ln0_b=row(ln0_b), w_in=w_in_p.astype(BF16),
        q_norm=row(q_norm), kv_norm=row(kv_norm),
        w_uq=wq.reshape(Q_LORA, MLA_HEADS * QK_PAD).astype(BF16), w_ukv=w_ukv.astype(BF16),
        w_ukt=jnp.transpose(w_uk, (1, 2, 0)).astype(BF16),
        w_uvh=jnp.transpose(w_uv, (1, 0, 2)).astype(BF16),
        conv_w=conv_w, conv_b=row(conv_b), dt_bias=pad_row(dt_bias),
        a_neg=pad_row(-jnp.exp(a_log)), d_skip=row(jnp.repeat(d_skip, SSM_HEAD_DIM)),
        ssm_norm=row(ssm_norm),
        w_o_a=w_o[:D_ATTN].astype(BF16), w_o_s=w_o[D_ATTN:].astype(BF16),
        ln1_g=row(ln1_g), ln1_b=row(ln1_b),
        w_r_hi=w_r_hi, w_r_lo=(w_r - w_r_hi.astype(F32)).astype(BF16),
        b_r=pad_row(jnp.concatenate([b_rg, b_re])),
        ln2_g=row(ln2_g), ln2_b=row(ln2_b),
    )


def _rope_tables(pos):
    half = QK_ROPE // 2
    inv_freq = ROPE_THETA ** (-jnp.arange(half, dtype=F32) / half)
    ang = pos.astype(F32)[:, None] * inv_freq[None, :]
    cos, sin = jnp.cos(ang), jnp.sin(ang)
    zeros = jnp.zeros((pos.shape[0], LANES - QK_ROPE), F32)
    return (jnp.concatenate([cos, cos, zeros], axis=1), jnp.concatenate([-sin, sin, zeros], axis=1))


def _dispatch_tables(route, *, rb, n_blk):
    T = route.shape[0]
    A = T * TOP_K
    e_flat = route[:, :TOP_K].astype(jnp.int32).reshape(A)
    w_flat = route[:, TOP_K:2 * TOP_K].reshape(A)
    onehot = (e_flat[:, None] == jnp.arange(N_EXPERTS, dtype=jnp.int32)[None, :]).astype(jnp.int32)
    csum = jnp.cumsum(onehot, axis=0)
    rank = jnp.sum((csum - onehot) * onehot, axis=1)
    counts = csum[-1]
    start = jnp.cumsum(counts) - counts
    pcounts = (counts + rb - 1) // rb * rb
    pend = jnp.cumsum(pcounts)
    pstart = pend - pcounts
    pos = pstart[e_flat] + rank
    order = jnp.argsort(e_flat, stable=True)
    r = jnp.arange(n_blk * rb, dtype=jnp.int32)
    blk_e = jnp.clip(jnp.searchsorted(pend, jnp.arange(n_blk, dtype=jnp.int32) * rb, side="right"),
                     0, N_EXPERTS - 1).astype(jnp.int32)
    e_of_row = blk_e[r // rb]
    idx_in_e = r - pstart[e_of_row]
    valid = (idx_in_e < counts[e_of_row]) & (r < pend[-1])
    src = jnp.clip(start[e_of_row] + idx_in_e, 0, A - 1)
    a_of_row = order[src]
    row_tok = jnp.where(valid, a_of_row // TOP_K, 0).astype(jnp.int32)
    row_w = jnp.where(valid, w_flat[a_of_row], 0.0).reshape(-1, 1)
    n_used = (pend[-1] // rb).astype(jnp.int32).reshape(1)
    return row_tok, row_w, blk_e, n_used, pos.astype(jnp.int32)


def _largest_tile(n, cap):
    t = cap
    while n % t:
        t //= 2
    return t


def kernel(x_prompt, x_sample, cache_kv_latent, cache_k_rope, state_conv, state_ssm, ln0_g, ln0_b, w_in, q_norm, w_uq, kv_norm, w_uk, w_uv, conv_w, conv_b, dt_bias, a_log, d_skip, ssm_norm, w_o, ln1_g, ln1_b, w_rg, b_rg, w_re, b_re, w_gate, w_up, w_down, ln2_g, ln2_b):
    B, L, _ = x_prompt.shape
    Bs, Ls, _ = x_sample.shape
    past = cache_kv_latent.shape[2]
    Tp, Ts = B * L, Bs * Ls
    pw = _prep_weights(ln0_g, ln0_b, w_in[0], q_norm[0], w_uq[0], kv_norm[0], w_uk[0], w_uv[0], conv_w[0],
                       conv_b[0], dt_bias[0], a_log[0], d_skip[0], ssm_norm[0], w_o[0], ln1_g[0], ln1_b[0],
                       w_rg[0], b_rg[0], w_re[0], b_re[0], ln2_g[0], ln2_b[0])

    tm_p = _largest_tile(L, 256)
    xp = x_prompt.reshape(Tp, D_MODEL)
    cos_p, sin_p = _rope_tables(jnp.arange(L, dtype=jnp.int32))
    q_p, kv_p, lat_p, kr_p, krp_p, z_p, xbc_p, dt_p = _inproj(
        xp, pw, cos_p, sin_p, tm=tm_p, pos_blocks=L // tm_p, with_kv=True)
    attn_p = _attn_prompt(q_p.reshape(B, L, -1), kv_p.reshape(B, L, -1), krp_p.reshape(B, L, -1),
                          tq=_largest_tile(L, 512))
    q_ssd = min(CHUNK, L)
    y_p, conv_p, ssm_p = _ssd(
        xbc_p.reshape(B, L, -1), dt_p.reshape(B, L, -1), z_p.reshape(B, L, -1),
        jnp.zeros((B, CONV_K - 1, CONV_CH), F32), jnp.zeros((B, D_SSM, D_STATE), F32), pw,
        q=q_ssd, tt=_largest_tile(L, 256))
    h_p, route_p = _outproj(xp, attn_p.reshape(Tp, -1), y_p.reshape(Tp, -1), pw, tm=tm_p)

    xs = x_sample.reshape(Ts, D_MODEL)
    cos_s, sin_s = _rope_tables(past + jnp.arange(Ls, dtype=jnp.int32))
    tm_s = _largest_tile(Ts, 256)
    assert tm_s % Ls == 0
    q_s, _, lat_s, kr_s, krp_s, z_s, xbc_s, dt_s = _inproj(
        xs, pw, jnp.tile(cos_s, (tm_s // Ls, 1)), jnp.tile(sin_s, (tm_s // Ls, 1)),
        tm=tm_s, pos_blocks=1, with_kv=False)
    attn_s = _attn_sample(q_s.reshape(Bs, Ls, -1), pw, cache_kv_latent[0], cache_k_rope[0],
                          lat_s.reshape(Bs, Ls, -1), krp_s.reshape(Bs, Ls, -1),
                          tk=_largest_tile(past, 1024))
    y_s, conv_s, ssm_s = _ssd(
        xbc_s.reshape(Bs, Ls, -1), dt_s.reshape(Bs, Ls, -1), z_s.reshape(Bs, Ls, -1),
        state_conv[0], state_ssm[0].reshape(Bs, D_SSM, D_STATE), pw, q=min(CHUNK, Ls), tt=Ls)
    h_s, route_s = _outproj(xs, attn_s.reshape(Ts, -1), y_s.reshape(Ts, -1), pw, tm=tm_s)

    h_all = jnp.concatenate([h_p, h_s], axis=0)
    route = jnp.concatenate([route_p, route_s], axis=0)
    T = Tp + Ts
    rb = 256
    n_blk = (T * TOP_K + N_EXPERTS * (rb - 1) + rb - 1) // rb
    row_tok, row_w, blk_e, n_used, pos = _dispatch_tables(route, rb=rb, n_blk=n_blk)
    y_rows = _experts(h_all, row_tok, row_w, blk_e, n_used, w_gate[0], w_up[0], w_down[0], rb=rb)
    tm_c = _largest_tile(math.gcd(Tp, Ts), 256)
    pos_tiles = pos.reshape(T // tm_c, tm_c, TOP_K).transpose(0, 2, 1).reshape(-1)
    out_p, out_s = _combine(y_rows, pos_tiles, h_all, pw, tm=tm_c, t_prompt=Tp)

    return (out_p.reshape(B, L, D_MODEL), out_s.reshape(Bs, Ls, D_MODEL),
            lat_p.reshape(1, B, L, KV_LORA), lat_s.reshape(1, Bs, Ls, KV_LORA),
            kr_p.reshape(1, B, L, QK_ROPE), kr_s.reshape(1, Bs, Ls, QK_ROPE),
            conv_p[None], conv_s[None],
            ssm_p.reshape(1, B, SSM_HEADS, SSM_HEAD_DIM, D_STATE),
            ssm_s.reshape(1, Bs, SSM_HEADS, SSM_HEAD_DIM, D_STATE))
```

```python
import functools
import math

import jax
import jax.numpy as jnp
import numpy as np
from jax import lax
from jax.experimental import pallas as pl
from jax.experimental.pallas import tpu as pltpu

F32 = jnp.float32
BF16 = jnp.bfloat16

D_MODEL = 2048
CHUNK = 64
MLA_HEADS = 8
QK_NOPE = 128
QK_ROPE = 64
QK_DIM = QK_NOPE + QK_ROPE
V_DIM = 128
Q_LORA = 512
KV_LORA = 512
ROPE_THETA = 10000.0
ATTN_SCALE = QK_DIM ** -0.5
D_ATTN = MLA_HEADS * V_DIM
D_SSM = 1024
SSM_HEAD_DIM = 64
SSM_HEADS = D_SSM // SSM_HEAD_DIM
SSM_GROUPS = 2
HEADS_PER_GROUP = SSM_HEADS // SSM_GROUPS
D_STATE = 128
CONV_K = 4
CONV_CH = D_SSM + 2 * SSM_GROUPS * D_STATE
N_EXPERT_GROUPS = 8
EXPERTS_PER_GROUP = 8
N_EXPERTS = N_EXPERT_GROUPS * EXPERTS_PER_GROUP
TOP_K = 2
D_EXPERT = 512
DEPTH = 1
ALPHA = (2 * DEPTH) ** 0.25
RMS_EPS = 1e-6
LN_EPS = 1e-5

LANES = 128
MXU_DIM = 256
VMEM_LIMIT = 56 * 1024 * 1024

QK_PAD = 2 * LANES
C_Q, C_KV, C_Z, C_XBC, C_KR, C_DT = 0, 512, 1024, 2048, 3584, 3712
D_IN_PAD = 3840


def _const_spec(shape):
    nd = len(shape)
    return pl.BlockSpec(shape, lambda *_: (0,) * nd, pipeline_mode=pl.Buffered(1))


def _split3(a):
    hi = a.astype(BF16)
    r1 = a - hi.astype(F32)
    mid = r1.astype(BF16)
    lo = (r1 - mid.astype(F32)).astype(BF16)
    return hi, mid, lo


def _dot(a, b):
    return jnp.dot(a, b, preferred_element_type=F32)


def _dot_nt(a, b):
    return lax.dot_general(a, b, (((1,), (1,)), ((), ())), preferred_element_type=F32)


def _dot_tn(a, b):
    return lax.dot_general(a, b, (((0,), (0,)), ((), ())), preferred_element_type=F32)


def _exact_dot(a_f32, sel_bf16):
    hi, mid, lo = _split3(a_f32)
    return _dot(hi, sel_bf16) + _dot(mid, sel_bf16) + _dot(lo, sel_bf16)


def _exact_dot_l(sel_bf16, a_f32):
    hi, mid, lo = _split3(a_f32)
    return _dot(sel_bf16, hi) + _dot(sel_bf16, mid) + _dot(sel_bf16, lo)


def _layernorm(x, g, b):
    mu = jnp.mean(x, axis=-1, keepdims=True)
    xc = x - mu
    var = jnp.mean(xc * xc, axis=-1, keepdims=True)
    return xc * lax.rsqrt(var + LN_EPS) * g + b


def _rmsnorm(x, g):
    return x * lax.rsqrt(jnp.mean(x * x, axis=-1, keepdims=True) + RMS_EPS) * g


def _silu(x):
    return x * (1.0 / (1.0 + jnp.exp(-x)))


def _rope128(x, cos_t, sin_t):
    lane = lax.broadcasted_iota(jnp.int32, x.shape, 1)
    partner = jnp.where(lane < QK_ROPE // 2, pltpu.roll(x, LANES - QK_ROPE // 2, 1),
                        pltpu.roll(x, QK_ROPE // 2, 1))
    return x * cos_t + partner * sin_t


def _inproj_kernel(x_ref, g0_ref, b0_ref, w_in_ref, qn_ref, kvn_ref, w_uq_ref, w_ukv_ref,
                   cos_ref, sin_ref, dtb_ref,
                   q_ref, kv_ref, lat_ref, kr_ref, krp_ref, z_ref, xbc_ref, dt_ref, *, with_kv):
    xn = _layernorm(x_ref[...], g0_ref[...], b0_ref[...]).astype(BF16)
    cos_t = cos_ref[...]
    sin_t = sin_ref[...]

    c_q = _dot(xn, w_in_ref[:, C_Q:C_Q + Q_LORA])
    qb = _rmsnorm(c_q, qn_ref[...]).astype(BF16)
    for h in range(MLA_HEADS):
        qh = _dot(qb, w_uq_ref[:, h * QK_PAD:(h + 1) * QK_PAD])
        q_ref[:, h * QK_PAD:h * QK_PAD + LANES] = qh[:, :LANES].astype(BF16)
        q_ref[:, h * QK_PAD + LANES:(h + 1) * QK_PAD] = _rope128(qh[:, LANES:], cos_t, sin_t).astype(BF16)

    c_kv = _dot(xn, w_in_ref[:, C_KV:C_KV + KV_LORA])
    lat = _rmsnorm(c_kv, kvn_ref[...])
    lat_ref[...] = lat
    if with_kv:
        kv_ref[...] = _dot(lat.astype(BF16), w_ukv_ref[...]).astype(BF16)
    else:
        kv_ref[...] = jnp.zeros(kv_ref.shape, BF16)

    k_r = _rope128(_dot(xn, w_in_ref[:, C_KR:C_KR + LANES]), cos_t, sin_t)
    kr_ref[...] = k_r[:, :QK_ROPE]
    krp_ref[...] = k_r.astype(BF16)

    z_ref[...] = _dot(xn, w_in_ref[:, C_Z:C_Z + D_SSM]).astype(BF16)
    xbc_ref[...] = _dot(xn, w_in_ref[:, C_XBC:C_XBC + CONV_CH])

    dt_raw = _dot(xn, w_in_ref[:, C_DT:C_DT + LANES]) + dtb_ref[...]
    dt = jnp.maximum(dt_raw, 0.0) + jnp.log1p(jnp.exp(-jnp.abs(dt_raw)))
    lane = lax.broadcasted_iota(jnp.int32, dt.shape, 1)
    dt_ref[...] = jnp.where(lane < SSM_HEADS, dt, 0.0)


def _inproj(x, pw, cos_t, sin_t, *, tm, pos_blocks, with_kv):
    T = x.shape[0]
    n = T // tm
    row = lambda i: (i, 0)
    pos = lambda i: (i % pos_blocks, 0)
    kv_cols = 2 * D_ATTN if with_kv else LANES
    out_shape = [
        jax.ShapeDtypeStruct((T, MLA_HEADS * QK_PAD), BF16),
        jax.ShapeDtypeStruct((T, kv_cols), BF16),
        jax.ShapeDtypeStruct((T, KV_LORA), F32),
        jax.ShapeDtypeStruct((T, QK_ROPE), F32),
        jax.ShapeDtypeStruct((T, LANES), BF16),
        jax.ShapeDtypeStruct((T, D_SSM), BF16),
        jax.ShapeDtypeStruct((T, CONV_CH), F32),
        jax.ShapeDtypeStruct((T, LANES), F32),
    ]
    out_specs = [pl.BlockSpec((tm, s.shape[1]), row) for s in out_shape]
    in_specs = [
        pl.BlockSpec((tm, D_MODEL), row),
        _const_spec((1, D_MODEL)), _const_spec((1, D_MODEL)),
        _const_spec((D_MODEL, D_IN_PAD)),
        _const_spec((1, Q_LORA)), _const_spec((1, KV_LORA)),
        _const_spec((Q_LORA, MLA_HEADS * QK_PAD)),
        _const_spec((KV_LORA, 2 * D_ATTN)),
        pl.BlockSpec((tm, LANES), pos), pl.BlockSpec((tm, LANES), pos),
        _const_spec((1, LANES)),
    ]
    return pl.pallas_call(
        functools.partial(_inproj_kernel, with_kv=with_kv),
        grid=(n,), in_specs=in_specs, out_specs=out_specs, out_shape=out_shape,
        compiler_params=pltpu.CompilerParams(dimension_semantics=("arbitrary",),
                                             vmem_limit_bytes=VMEM_LIMIT),
        name="inproj",
    )(x, pw["ln0_g"], pw["ln0_b"], pw["w_in"], pw["q_norm"], pw["kv_norm"], pw["w_uq"],
      pw["w_ukv"], cos_t, sin_t, pw["dt_bias"])


def _attn_prompt_kernel(qi_ref, ki_ref, q_ref, kv_ref, krp_ref, o_ref, *scratch, tq, tk):
    m_refs = scratch[0:MLA_HEADS]
    acc_refs = scratch[MLA_HEADS:2 * MLA_HEADS]
    p_id = pl.program_id(1)
    qi = qi_ref[p_id]
    ki = ki_ref[p_id]
    k_last = ((qi + 1) * tq - 1) // tk

    @pl.when(ki == 0)
    def _():
        for h in range(MLA_HEADS):
            m_refs[h][...] = jnp.full(m_refs[h].shape, -jnp.inf, F32)
            acc_refs[h][...] = jnp.zeros(acc_refs[h].shape, F32)

    def step(masked):
        krp = krp_ref[...]
        ones_col = (lax.broadcasted_iota(jnp.int32, (tk, LANES), 1) == 0).astype(BF16)
        if masked:
            r = (qi * tq + lax.broadcasted_iota(jnp.int32, (tq, tk), 0)) // CHUNK
            c = (ki * tk + lax.broadcasted_iota(jnp.int32, (tq, tk), 1)) // CHUNK
            allowed = c <= r
        for h in range(MLA_HEADS):
            qh = q_ref[:, h * QK_PAD:(h + 1) * QK_PAD]
            kh = jnp.concatenate([kv_ref[:, h * QK_NOPE:(h + 1) * QK_NOPE], krp], axis=1)
            s = _dot_nt(qh, kh)
            if masked:
                s = jnp.where(allowed, s, -jnp.inf)
            m_old = m_refs[h][...]
            m_new = jnp.maximum(m_old, jnp.max(s, axis=1, keepdims=True))
            alpha = jnp.exp2(m_old - m_new)
            p = jnp.exp2(s - m_new).astype(BF16)
            v_aug = jnp.concatenate([kv_ref[:, D_ATTN + h * V_DIM:D_ATTN + (h + 1) * V_DIM], ones_col], axis=1)
            acc_refs[h][...] = alpha * acc_refs[h][...] + _dot(p, v_aug)
            m_refs[h][...] = m_new

    @pl.when(ki < k_last)
    def _():
        step(False)

    @pl.when(ki == k_last)
    def _():
        step(True)
        for h in range(MLA_HEADS):
            acc = acc_refs[h][...]
            o_ref[:, h * V_DIM:(h + 1) * V_DIM] = (acc[:, :V_DIM] / acc[:, V_DIM:V_DIM + 1]).astype(BF16)


def _attn_prompt(q, kv, krp, *, tq, tk):
    B, L, _ = q.shape
    nq = L // tq
    pairs = [(i, j) for i in range(nq) for j in range(((i + 1) * tq - 1) // tk + 1)]
    qi_tab = jnp.asarray(np.array([p[0] for p in pairs], np.int32))
    ki_tab = jnp.asarray(np.array([p[1] for p in pairs], np.int32))
    grid_spec = pltpu.PrefetchScalarGridSpec(
        num_scalar_prefetch=2,
        grid=(B, len(pairs)),
        in_specs=[
            pl.BlockSpec((None, tq, MLA_HEADS * QK_PAD), lambda b, p, qi, ki: (b, qi[p], 0)),
            pl.BlockSpec((None, tk, 2 * D_ATTN), lambda b, p, qi, ki: (b, ki[p], 0)),
            pl.BlockSpec((None, tk, LANES), lambda b, p, qi, ki: (b, ki[p], 0)),
        ],
        out_specs=pl.BlockSpec((None, tq, D_ATTN), lambda b, p, qi, ki: (b, qi[p], 0)),
        scratch_shapes=([pltpu.VMEM((tq, 1), F32)] * MLA_HEADS
                        + [pltpu.VMEM((tq, 2 * V_DIM), F32)] * MLA_HEADS),
    )
    return pl.pallas_call(
        functools.partial(_attn_prompt_kernel, tq=tq, tk=tk),
        grid_spec=grid_spec,
        out_shape=jax.ShapeDtypeStruct((B, L, D_ATTN), BF16),
        compiler_params=pltpu.CompilerParams(dimension_semantics=("arbitrary", "arbitrary"),
                                             vmem_limit_bytes=VMEM_LIMIT),
        name="attn_prompt",
    )(qi_tab, ki_tab, q, kv, krp)


def _attn_sample_kernel(q_ref, wukt_ref, wuv_ref, latp_ref, krpast_ref, latn_ref, krn_ref, o_ref,
                        qlat_ref, qrp_ref, m_ref, l_ref, acc_ref, *, ls, past, tk, n_new_pad):
    k = pl.program_id(1)
    nk = pl.num_programs(1)
    rows = MLA_HEADS * ls

    @pl.when(k == 0)
    def _():
        for h in range(MLA_HEADS):
            qn = q_ref[:, h * QK_PAD:h * QK_PAD + LANES]
            qlat_ref[h * ls:(h + 1) * ls, :] = _dot(qn, wukt_ref[h]).astype(BF16)
            qrp_ref[h * ls:(h + 1) * ls, :] = q_ref[:, h * QK_PAD + LANES:(h + 1) * QK_PAD]
        m_ref[...] = jnp.full(m_ref.shape, -jnp.inf, F32)
        l_ref[...] = jnp.zeros(l_ref.shape, F32)
        acc_ref[...] = jnp.zeros(acc_ref.shape, F32)

    q_chunk = (past + lax.broadcasted_iota(jnp.int32, (rows, 1), 0) % ls) // CHUNK

    def update(s, lat_b):
        m_old = m_ref[...]
        m_new = jnp.maximum(m_old, jnp.max(s, axis=1, keepdims=True))
        alpha = jnp.exp2(m_old - m_new)
        p = jnp.exp2(s - m_new)
        l_ref[...] = alpha * l_ref[...] + jnp.sum(p, axis=1, keepdims=True)
        acc_ref[...] = alpha * acc_ref[...] + _dot(p.astype(BF16), lat_b)
        m_ref[...] = m_new

    lat_b = latp_ref[...].astype(BF16)
    kr_b = krpast_ref[...].astype(BF16)
    s = _dot_nt(qlat_ref[...], lat_b) + _dot_nt(qrp_ref[:, :QK_ROPE], kr_b)
    k_pos = k * tk + lax.broadcasted_iota(jnp.int32, (rows, tk), 1)
    s = jnp.where(k_pos // CHUNK <= q_chunk, s, -jnp.inf)
    update(s, lat_b)

    @pl.when(k == nk - 1)
    def _():
        latn_b = latn_ref[...].astype(BF16)
        s2 = _dot_nt(qlat_ref[...], latn_b) + _dot_nt(qrp_ref[...], krn_ref[...])
        j = lax.broadcasted_iota(jnp.int32, (rows, n_new_pad), 1)
        ok = ((past + j) // CHUNK <= q_chunk) & (j < ls)
        update(jnp.where(ok, s2, -jnp.inf), latn_b)
        o_lat = (acc_ref[...] / l_ref[...]).astype(BF16)
        for h in range(MLA_HEADS):
            o_ref[:, h * V_DIM:(h + 1) * V_DIM] = _dot(o_lat[h * ls:(h + 1) * ls, :], wuv_ref[h]).astype(BF16)


def _attn_sample(q, pw, lat_past, kr_past, lat_new, krp_new, *, tk):
    Bs, Ls, _ = q.shape
    past = lat_past.shape[1]
    n_new_pad = LANES
    lat_new = jnp.pad(lat_new, ((0, 0), (0, n_new_pad - Ls), (0, 0)))
    krp_new = jnp.pad(krp_new, ((0, 0), (0, n_new_pad - Ls), (0, 0)))
    rows = MLA_HEADS * Ls
    return pl.pallas_call(
        functools.partial(_attn_sample_kernel, ls=Ls, past=past, tk=tk, n_new_pad=n_new_pad),
        grid=(Bs, past // tk),
        in_specs=[
            pl.BlockSpec((None, Ls, MLA_HEADS * QK_PAD), lambda b, k: (b, 0, 0)),
            _const_spec((MLA_HEADS, QK_NOPE, KV_LORA)),
            _const_spec((MLA_HEADS, KV_LORA, V_DIM)),
            pl.BlockSpec((None, tk, KV_LORA), lambda b, k: (b, k, 0)),
            pl.BlockSpec((None, tk, QK_ROPE), lambda b, k: (b, k, 0)),
            pl.BlockSpec((None, n_new_pad, KV_LORA), lambda b, k: (b, 0, 0)),
            pl.BlockSpec((None, n_new_pad, LANES), lambda b, k: (b, 0, 0)),
        ],
        out_specs=pl.BlockSpec((None, Ls, D_ATTN), lambda b, k: (b, 0, 0)),
        out_shape=jax.ShapeDtypeStruct((Bs, Ls, D_ATTN), BF16),
        scratch_shapes=[pltpu.VMEM((rows, KV_LORA), BF16), pltpu.VMEM((rows, LANES), BF16),
                        pltpu.VMEM((rows, 1), F32), pltpu.VMEM((rows, 1), F32),
                        pltpu.VMEM((rows, KV_LORA), F32)],
        compiler_params=pltpu.CompilerParams(dimension_semantics=("arbitrary", "arbitrary"),
                                             vmem_limit_bytes=VMEM_LIMIT),
        name="attn_sample",
    )(q, pw["w_ukt"], pw["w_uvh"], lat_past, kr_past, lat_new, krp_new)


def _ssd_consts(q):
    hb = MXU_DIM // q
    hq = SSM_HEADS * q
    lane = np.arange(hq)
    ltri = (np.arange(q)[None, :] <= np.arange(q)[:, None]).astype(np.float32)
    sel_c = (np.arange(LANES)[:, None] == (lane // q)[None, :]).astype(np.float32)
    sel_p = (np.arange(LANES)[:, None] == (np.arange(D_SSM) // SSM_HEAD_DIM)[None, :]).astype(np.float32)
    diag = (np.arange(q)[:, None] == (lane % q)[None, :]).astype(np.float32)
    causal = (np.arange(q)[:, None] >= (lane % q)[None, :]).astype(np.float32)
    bd_rows = np.arange(hb * q) // q
    bd_cols = np.arange(hb * SSM_HEAD_DIM) // SSM_HEAD_DIM
    bdmask = (bd_rows[:, None] == bd_cols[None, :]).astype(np.float32)
    return dict(ltri=jnp.asarray(ltri, BF16), sel_c=jnp.asarray(sel_c, BF16),
                sel_p=jnp.asarray(sel_p, BF16), diag=jnp.asarray(diag, F32),
                causal=jnp.asarray(causal, F32), bdmask=jnp.asarray(bdmask, BF16))


def _ssd_kernel(xbc_ref, dt_ref, z_ref, cprev_ref, h0_ref, cw_ref, cb_ref, a_ref, dskip_ref, norm_ref,
                ltri_ref, selc_ref, selp_ref, diag_ref, causal_ref, bdmask_ref,
                y_ref, cout_ref, hout_ref,
                ext_ref, u_ref, ht_ref, *, q, tt):
    t = pl.program_id(1)
    nt = pl.num_programs(1)
    hb = MXU_DIM // q
    nblk = SSM_HEADS // hb
    gw = HEADS_PER_GROUP * SSM_HEAD_DIM
    pad = 8

    @pl.when(t == 0)
    def _():
        ext_ref[0:pad, :] = jnp.zeros((pad, CONV_CH), F32)
        ext_ref[pad - (CONV_K - 1):pad, :] = cprev_ref[...]
        ht_ref[...] = h0_ref[...].T

    xbc = xbc_ref[...]
    ext_ref[pad:pad + tt, :] = xbc
    cout_ref[...] = xbc[tt - (CONV_K - 1):, :]
    conv = cb_ref[...] + cw_ref[0:1, :] * ext_ref[pad - 3:pad - 3 + tt, :]
    for k in range(1, CONV_K):
        conv = conv + cw_ref[k:k + 1, :] * ext_ref[pad - 3 + k:pad - 3 + k + tt, :]
    u_ref[...] = _silu(conv)
    ext_ref[pad - (CONV_K - 1):pad, :] = xbc[tt - (CONV_K - 1):, :]

    a_row = a_ref[...]
    d_row = dskip_ref[...]
    norm_row = norm_ref[...]

    def chunk(c, carry):
        r0 = pl.multiple_of(c * q, q)
        u = u_ref[pl.ds(r0, q), :]
        xs = u[:, :D_SSM]
        dt = dt_ref[pl.ds(r0, q), :]
        a_cs = _exact_dot_l(ltri_ref[...], dt * a_row)
        ap = _exact_dot(a_cs, selp_ref[...])
        dtp = _exact_dot(dt, selp_ref[...])
        a_i = ap if q == SSM_HEAD_DIM else _exact_dot(a_cs, selc_ref[...])
        a_j = jnp.sum(a_i * diag_ref[...], axis=0, keepdims=True)
        decay = jnp.exp(jnp.where(causal_ref[...] > 0.5, a_i - a_j, -jnp.inf))
        ap_last = ap[q - 1:q, :]
        xdt = (xs * dtp).astype(BF16)
        xw = (xs * (jnp.exp(ap_last - ap) * dtp)).astype(BF16)
        e_ap = jnp.exp(ap)

        cb_parts, y_off_parts = [], []
        for g in range(SSM_GROUPS):
            bm = u[:, D_SSM + g * D_STATE:D_SSM + (g + 1) * D_STATE].astype(BF16)
            cm = u[:, D_SSM + (SSM_GROUPS + g) * D_STATE:D_SSM + (SSM_GROUPS + g + 1) * D_STATE].astype(BF16)
            cb_parts.append(_dot_nt(cm, jnp.concatenate([bm] * HEADS_PER_GROUP, axis=0)))
            gl = slice(g * gw, (g + 1) * gw)
            ht_g = ht_ref[:, gl]
            y_off_parts.append(_dot(cm, ht_g.astype(BF16)) * e_ap[:, gl])
            st = _dot_tn(bm, xw[:, gl])
            ht_ref[:, gl] = ht_g * jnp.exp(ap_last[:, gl]) + st
        w_all = (jnp.concatenate(cb_parts, axis=1) * decay).astype(BF16)
        y_diag_parts = []
        for b in range(nblk):
            x_b = xdt[:, b * hb * SSM_HEAD_DIM:(b + 1) * hb * SSM_HEAD_DIM]
            bd = jnp.concatenate([x_b] * hb, axis=0) * bdmask_ref[...]
            y_diag_parts.append(_dot(w_all[:, b * MXU_DIM:(b + 1) * MXU_DIM], bd))
        y = jnp.concatenate(y_diag_parts, axis=1) + jnp.concatenate(y_off_parts, axis=1)

        y = y + d_row * xs
        y = y * _silu(z_ref[pl.ds(r0, q), :].astype(F32))
        outs = []
        for g in range(SSM_GROUPS):
            yg = y[:, g * gw:(g + 1) * gw]
            outs.append(yg * lax.rsqrt(jnp.mean(yg * yg, axis=1, keepdims=True) + RMS_EPS))
        y_ref[pl.ds(r0, q), :] = (jnp.concatenate(outs, axis=1) * norm_row).astype(BF16)
        return carry

    lax.fori_loop(0, tt // q, chunk, 0)

    @pl.when(t == nt - 1)
    def _():
        hout_ref[...] = ht_ref[...].T


def _ssd(xbc, dt, z, conv_prev, h0, pw, *, q, tt):
    B, L, _ = xbc.shape
    cs = _ssd_consts(q)
    hq = SSM_HEADS * q
    hb = MXU_DIM // q
    tile = lambda w: pl.BlockSpec((None, tt, w), lambda b, t: (b, t, 0))
    per_b = lambda s: pl.BlockSpec((None,) + s, lambda b, t: (b, 0, 0))
    return pl.pallas_call(
        functools.partial(_ssd_kernel, q=q, tt=tt),
        grid=(B, L // tt),
        in_specs=[tile(CONV_CH), tile(LANES), tile(D_SSM), per_b((CONV_K - 1, CONV_CH)),
                  per_b((D_SSM, D_STATE)),
                  _const_spec((CONV_K, CONV_CH)), _const_spec((1, CONV_CH)), _const_spec((1, LANES)),
                  _const_spec((1, D_SSM)), _const_spec((1, D_SSM)),
                  _const_spec((q, q)), _const_spec((LANES, hq)), _const_spec((LANES, D_SSM)),
                  _const_spec((q, hq)), _const_spec((q, hq)),
                  _const_spec((MXU_DIM, hb * SSM_HEAD_DIM))],
        out_specs=[tile(D_SSM), per_b((CONV_K - 1, CONV_CH)), per_b((D_SSM, D_STATE))],
        out_shape=[jax.ShapeDtypeStruct((B, L, D_SSM), BF16),
                   jax.ShapeDtypeStruct((B, CONV_K - 1, CONV_CH), F32),
                   jax.ShapeDtypeStruct((B, D_SSM, D_STATE), F32)],
        scratch_shapes=[pltpu.VMEM((8 + tt, CONV_CH), F32), pltpu.VMEM((tt, CONV_CH), F32),
                        pltpu.VMEM((D_STATE, D_SSM), F32)],
        compiler_params=pltpu.CompilerParams(dimension_semantics=("arbitrary", "arbitrary"),
                                             vmem_limit_bytes=VMEM_LIMIT),
        name="ssd",
    )(xbc, dt, z, conv_prev, h0, pw["conv_w"], pw["conv_b"], pw["a_neg"], pw["d_skip"], pw["ssm_norm"],
      cs["ltri"], cs["sel_c"], cs["sel_p"], cs["diag"], cs["causal"], cs["bdmask"])


R_OFF = N_EXPERT_GROUPS


def _outproj_kernel(xp_ref, xs_ref, attnp_ref, attns_ref, yp_ref, ys_ref, g0_ref, b0_ref, wo_a_ref, wo_s_ref,
                    g1_ref, b1_ref, wr_hi_ref, wr_lo_ref, br_ref, ltri_ref, h_ref, route_ref, cnt_ref,
                    *, n_prompt_tiles):
    i = pl.program_id(0)
    is_p = i < n_prompt_tiles
    x = jnp.where(is_p, xp_ref[...], xs_ref[...])
    attn = jnp.where(is_p, attnp_ref[...], attns_ref[...])
    yssm = jnp.where(is_p, yp_ref[...], ys_ref[...])

    @pl.when(i == 0)
    def _():
        cnt_ref[...] = jnp.zeros(cnt_ref.shape, F32)

    xn = _layernorm(x, g0_ref[...], b0_ref[...])
    mixed = _dot(attn, wo_a_ref[...]) + _dot(yssm, wo_s_ref[...])
    h = _layernorm(ALPHA * xn + mixed, g1_ref[...], b1_ref[...])
    h_ref[...] = h

    h_hi = h.astype(BF16)
    h_lo = (h - h_hi.astype(F32)).astype(BF16)
    lg = (_dot(h_hi, wr_hi_ref[...]) + _dot(h_hi, wr_lo_ref[...]) + _dot(h_lo, wr_hi_ref[...])
          + br_ref[...])
    lane = lax.broadcasted_iota(jnp.int32, lg.shape, 1)
    big = jnp.int32(1 << 20)
    gl = jnp.where(lane < N_EXPERT_GROUPS, lg, -jnp.inf)
    gmax = jnp.max(gl, axis=1, keepdims=True)
    grp = jnp.min(jnp.where(gl == gmax, lane, big), axis=1, keepdims=True)
    g_w = 1.0 / jnp.sum(jnp.exp(gl - gmax), axis=1, keepdims=True)
    in_grp = (lane >= R_OFF) & (lane < R_OFF + N_EXPERTS) & ((lane - R_OFF) // EXPERTS_PER_GROUP == grp)
    el = jnp.where(in_grp, lg, -jnp.inf)
    emax = jnp.max(el, axis=1, keepdims=True)
    ee = jnp.exp(el - emax)
    prob = jnp.where(in_grp, ee / jnp.sum(ee, axis=1, keepdims=True), -1.0)
    p1 = jnp.max(prob, axis=1, keepdims=True)
    i1 = jnp.min(jnp.where(prob == p1, lane, big), axis=1, keepdims=True)
    prob2 = jnp.where(lane == i1, -1.0, prob)
    p2 = jnp.max(prob2, axis=1, keepdims=True)
    i2 = jnp.min(jnp.where(prob2 == p2, lane, big), axis=1, keepdims=True)
    denom = p1 + p2
    oh1 = (lane == i1).astype(F32)
    oh2 = (lane == i2).astype(F32)
    oh = oh1 + oh2
    before = _dot(ltri_ref[...], oh.astype(BF16)) + cnt_ref[0:1, :]
    rank1 = jnp.sum(before * oh1, axis=1, keepdims=True)
    rank2 = jnp.sum(before * oh2, axis=1, keepdims=True)
    cnt_ref[...] = cnt_ref[...] + jnp.sum(oh, axis=0, keepdims=True)
    route = jnp.where(lane == 0, (i1 - R_OFF).astype(F32),
                      jnp.where(lane == 1, (i2 - R_OFF).astype(F32),
                                jnp.where(lane == 2, g_w * p1 / denom,
                                          jnp.where(lane == 3, g_w * p2 / denom,
                                                    jnp.where(lane == 4, rank1,
                                                              jnp.where(lane == 5, rank2, 0.0))))))
    route_ref[...] = route


def _outproj(xp, xs, attn_p, attn_s, y_p, y_s, pw, *, tm):
    Tp, Ts = xp.shape[0], xs.shape[0]
    npt, nst = Tp // tm, Ts // tm
    T = Tp + Ts
    row = lambda i: (i, 0)
    prow = lambda i: (jnp.minimum(i, npt - 1), 0)
    srow = lambda i: (jnp.maximum(i - npt, 0), 0)
    ltri = jnp.asarray(np.tril(np.ones((tm, tm), np.float32), -1), BF16)
    return pl.pallas_call(
        functools.partial(_outproj_kernel, n_prompt_tiles=npt),
        grid=(npt + nst,),
        in_specs=[pl.BlockSpec((tm, D_MODEL), prow), pl.BlockSpec((tm, D_MODEL), srow),
                  pl.BlockSpec((tm, D_ATTN), prow), pl.BlockSpec((tm, D_ATTN), srow),
                  pl.BlockSpec((tm, D_SSM), prow), pl.BlockSpec((tm, D_SSM), srow),
                  _const_spec((1, D_MODEL)), _const_spec((1, D_MODEL)),
                  _const_spec((D_ATTN, D_MODEL)), _const_spec((D_SSM, D_MODEL)),
                  _const_spec((1, D_MODEL)), _const_spec((1, D_MODEL)),
                  _const_spec((D_MODEL, LANES)), _const_spec((D_MODEL, LANES)), _const_spec((1, LANES)),
                  _const_spec((tm, tm))],
        out_specs=[pl.BlockSpec((tm, D_MODEL), row), pl.BlockSpec((tm, LANES), row),
                   pl.BlockSpec((8, LANES), lambda i: (0, 0))],
        out_shape=[jax.ShapeDtypeStruct((T, D_MODEL), F32), jax.ShapeDtypeStruct((T, LANES), F32),
                   jax.ShapeDtypeStruct((8, LANES), F32)],
        compiler_params=pltpu.CompilerParams(dimension_semantics=("arbitrary",),
                                             vmem_limit_bytes=VMEM_LIMIT),
        name="outproj",
    )(xp, xs, attn_p, attn_s, y_p, y_s, pw["ln0_g"], pw["ln0_b"], pw["w_o_a"], pw["w_o_s"],
      pw["ln1_g"], pw["ln1_b"], pw["w_r_hi"], pw["w_r_lo"], pw["b_r"], ltri)


def _experts_kernel(blk_e_ref, n_used_ref, tok_ref, tok_next_ref, h_hbm, wg_ref, wu_ref, wd_ref,
                    y_ref, xbuf, sem, wg_b, wu_b, wd_b, prev_e, *, rb):
    i = pl.program_id(0)
    n = pl.num_programs(0)
    slot = i % 2

    def row_copy(tok, dst_slot, r):
        return pltpu.make_async_copy(h_hbm.at[pl.ds(tok, 1)], xbuf.at[dst_slot, pl.ds(r, 1)],
                                     sem.at[dst_slot])

    def wait_block(s):
        pltpu.make_async_copy(h_hbm.at[pl.ds(0, rb)], xbuf.at[s], sem.at[s]).wait()

    @pl.when(i == 0)
    def _():
        prev_e[0] = -1

        def body(r, c):
            row_copy(tok_ref[r], 0, r).start()
            return c
        lax.fori_loop(0, rb, body, 0)

    wait_block(slot)
    e = blk_e_ref[jnp.minimum(i, n_used_ref[0] - 1)]

    @pl.when(e != prev_e[0])
    def _():
        wg_b[...] = wg_ref[...].astype(BF16)
        wu_b[...] = wu_ref[...].astype(BF16)
        wd_b[...] = wd_ref[...].astype(BF16)
        prev_e[0] = e

    xb = xbuf[slot].astype(BF16)
    for r in range(rb):
        row_copy(tok_next_ref[r], 1 - slot, r).start()
    hid =(_silu(_dot(xb, wg_b[...])) * _dot(xb, wu_b[...])).astype(BF16)
    y_ref[...] = _dot(hid, wd_b[...])

    @pl.when(i == n - 1)
    def _():
        wait_block(1 - slot)


def _experts(h_all, row_tok, blk_e, n_used, w_gate, w_up, w_down, *, rb):
    n_blk = blk_e.shape[0]
    n_rows = n_blk * rb
    grid_spec = pltpu.PrefetchScalarGridSpec(
        num_scalar_prefetch=2,
        grid=(n_blk,),
        in_specs=[
            pl.BlockSpec((rb,), lambda i, be, nu: (jnp.minimum(i, nu[0] - 1),), memory_space=pltpu.SMEM),
            pl.BlockSpec((rb,), lambda i, be, nu: (jnp.minimum(i + 1, nu[0] - 1),), memory_space=pltpu.SMEM),
            pl.BlockSpec(memory_space=pl.ANY),
            pl.BlockSpec((None, D_MODEL, D_EXPERT), lambda i, be, nu: (be[jnp.minimum(i, nu[0] - 1)], 0, 0)),
            pl.BlockSpec((None, D_MODEL, D_EXPERT), lambda i, be, nu: (be[jnp.minimum(i, nu[0] - 1)], 0, 0)),
            pl.BlockSpec((None, D_EXPERT, D_MODEL), lambda i, be, nu: (be[jnp.minimum(i, nu[0] - 1)], 0, 0)),
        ],
        out_specs=pl.BlockSpec((rb, D_MODEL), lambda i, be, nu: (i, 0)),
        scratch_shapes=[pltpu.VMEM((2, rb, D_MODEL), F32), pltpu.SemaphoreType.DMA((2,)),
                        pltpu.VMEM((D_MODEL, D_EXPERT), BF16), pltpu.VMEM((D_MODEL, D_EXPERT), BF16),
                        pltpu.VMEM((D_EXPERT, D_MODEL), BF16), pltpu.SMEM((1,), jnp.int32)],
    )
    return pl.pallas_call(
        functools.partial(_experts_kernel, rb=rb),
        grid_spec=grid_spec,
        out_shape=jax.ShapeDtypeStruct((n_rows, D_MODEL), F32),
        compiler_params=pltpu.CompilerParams(dimension_semantics=("arbitrary",),
                                             vmem_limit_bytes=VMEM_LIMIT),
        name="experts",
    )(blk_e, n_used, row_tok, row_tok, h_all, w_gate, w_up, w_down)


def _combine_kernel(pos_ref, pos_next_ref, y_hbm, h_ref, route_ref, g2_ref, b2_ref, op_ref, os_ref, ybuf, sem,
                    *, tm, n_prompt_tiles):
    i = pl.program_id(0)
    n = pl.num_programs(0)
    slot = i % 2
    rows = TOP_K * tm

    def row_copy(idx, dst_slot, r):
        return pltpu.make_async_copy(y_hbm.at[pl.ds(idx, 1)], ybuf.at[dst_slot, pl.ds(r, 1)],
                                     sem.at[dst_slot])

    def wait_tile(s):
        pltpu.make_async_copy(y_hbm.at[pl.ds(0, rows)], ybuf.at[s], sem.at[s]).wait()

    @pl.when(i == 0)
    def _():
        def body(r, c):
            row_copy(pos_ref[r], 0, r).start()
            return c
        lax.fori_loop(0, rows, body, 0)

    wait_tile(slot)
    gate1 = route_ref[:, 2:3]
    gate2 = route_ref[:, 3:4]
    f = ybuf[slot, 0:tm, :] * gate1 + ybuf[slot, tm:rows, :] * gate2
    for r in range(rows):
        row_copy(pos_next_ref[r], 1 - slot, r).start()
    out = _layernorm(ALPHA * h_ref[...] + f, g2_ref[...], b2_ref[...])

    @pl.when(i < n_prompt_tiles)
    def _():
        op_ref[...] = out

    @pl.when(i >= n_prompt_tiles)
    def _():
        os_ref[...] = out

    @pl.when(i == n - 1)
    def _():
        wait_tile(1 - slot)


def _combine(y_rows, pos_tiles, h_all, route, pw, *, tm, t_prompt):
    T = h_all.shape[0]
    n = T // tm
    npt = t_prompt // tm
    rows = TOP_K * tm
    return pl.pallas_call(
        functools.partial(_combine_kernel, tm=tm, n_prompt_tiles=npt),
        grid=(n,),
        in_specs=[
            pl.BlockSpec((rows,), lambda i: (i,), memory_space=pltpu.SMEM),
            pl.BlockSpec((rows,), lambda i: (jnp.minimum(i + 1, n - 1),), memory_space=pltpu.SMEM),
            pl.BlockSpec(memory_space=pl.ANY),
            pl.BlockSpec((tm, D_MODEL), lambda i: (i, 0)),
            pl.BlockSpec((tm, LANES), lambda i: (i, 0)),
            _const_spec((1, D_MODEL)), _const_spec((1, D_MODEL)),
        ],
        out_specs=[pl.BlockSpec((tm, D_MODEL), lambda i: (jnp.minimum(i, npt - 1), 0)),
                   pl.BlockSpec((tm, D_MODEL), lambda i: (jnp.maximum(i - npt, 0), 0))],
        out_shape=[jax.ShapeDtypeStruct((t_prompt, D_MODEL), F32),
                   jax.ShapeDtypeStruct((T - t_prompt, D_MODEL), F32)],
        scratch_shapes=[pltpu.VMEM((2, rows, D_MODEL), F32), pltpu.SemaphoreType.DMA((2,))],
        compiler_params=pltpu.CompilerParams(dimension_semantics=("arbitrary",),
                                             vmem_limit_bytes=VMEM_LIMIT),
        name="combine",
    )(pos_tiles, pos_tiles, y_rows, h_all, route, pw["ln2_g"], pw["ln2_b"])


def _prep_weights(ln0_g, ln0_b, w_in, q_norm, w_uq, kv_norm, w_uk, w_uv, conv_w, conv_b, dt_bias, a_log,
                  d_skip, ssm_norm, w_o, ln1_g, ln1_b, w_rg, b_rg, w_re, b_re, ln2_g, ln2_b):
    s_q, s_kv, s_kr, s_z, s_xbc = Q_LORA, Q_LORA + KV_LORA, Q_LORA + KV_LORA + QK_ROPE, \
        Q_LORA + KV_LORA + QK_ROPE + D_SSM, Q_LORA + KV_LORA + QK_ROPE + D_SSM + CONV_CH
    zc = lambda n: jnp.zeros((D_MODEL, n), F32)
    w_in_p = jnp.concatenate([
        w_in[:, :s_q], w_in[:, s_q:s_kv], w_in[:, s_kr:s_z], w_in[:, s_z:s_xbc],
        w_in[:, s_kv:s_kr], zc(LANES - QK_ROPE), w_in[:, s_xbc:], zc(LANES - SSM_HEADS)], axis=1)
    wq = w_uq.reshape(Q_LORA, MLA_HEADS, QK_DIM) * (ATTN_SCALE * math.log2(math.e))
    wq = jnp.concatenate([wq, jnp.zeros((Q_LORA, MLA_HEADS, QK_PAD - QK_DIM), F32)], axis=2)
    w_ukv = jnp.concatenate([w_uk.reshape(KV_LORA, D_ATTN), w_uv.reshape(KV_LORA, D_ATTN)], axis=1)
    w_r = jnp.concatenate([w_rg, w_re, jnp.zeros((D_MODEL, LANES - R_OFF - N_EXPERTS), F32)], axis=1)
    w_r_hi = w_r.astype(BF16)
    row = lambda v: v.reshape(1, -1)
    pad_row = lambda v: jnp.pad(v, (0, LANES - v.shape[0])).reshape(1, LANES)
    return dict(
        ln0_g=row(ln0_g), ln0_b=row(ln0_b), w_in=w_in_p.astype(BF16),
        q_norm=row(q_norm), kv_norm=row(kv_norm),
        w_uq=wq.reshape(Q_LORA, MLA_HEADS * QK_PAD).astype(BF16), w_ukv=w_ukv.astype(BF16),
        w_ukt=jnp.transpose(w_uk, (1, 2, 0)).astype(BF16),
        w_uvh=jnp.transpose(w_uv, (1, 0, 2)).astype(BF16),
        conv_w=conv_w, conv_b=row(conv_b), dt_bias=pad_row(dt_bias),
        a_neg=pad_row(-jnp.exp(a_log)), d_skip=row(jnp.repeat(d_skip, SSM_HEAD_DIM)),
        ssm_norm=row(ssm_norm),
        w_o_a=w_o[:D_ATTN].astype(BF16), w_o_s=w_o[D_ATTN:].astype(BF16),
        ln1_g=row(ln1_g), ln1_b=row(ln1_b),
        w_r_hi=w_r_hi, w_r_lo=(w_r - w_r_hi.astype(F32)).astype(BF16),
        b_r=pad_row(jnp.concatenate([b_rg, b_re])),
        ln2_g=row(ln2_g), ln2_b=row(ln2_b),
    )


def _rope_tables(pos):
    half = QK_ROPE // 2
    inv_freq = ROPE_THETA ** (-jnp.arange(half, dtype=F32) / half)
    ang = pos.astype(F32)[:, None] * inv_freq[None, :]
    cos, sin = jnp.cos(ang), jnp.sin(ang)
    zeros = jnp.zeros((pos.shape[0], LANES - QK_ROPE), F32)
    return (jnp.concatenate([cos, cos, zeros], axis=1), jnp.concatenate([-sin, sin, zeros], axis=1))


def _dispatch_tables(route, counts, *, rb, n_blk):
    T = route.shape[0]
    experts = jnp.arange(N_EXPERTS, dtype=jnp.int32)
    e = route[:, 0:TOP_K].astype(jnp.int32)
    rank = route[:, 4:4 + TOP_K].astype(jnp.int32)
    counts = counts.astype(jnp.int32)
    pcounts = (counts + rb - 1) // rb * rb
    pend = jnp.cumsum(pcounts)
    pstart = pend - pcounts
    pos = jnp.sum(jnp.where(e[:, :, None] == experts, pstart, 0), axis=2) + rank
    blk_start = jnp.arange(n_blk, dtype=jnp.int32) * rb
    blk_e = jnp.minimum(jnp.sum((pend[None, :] <= blk_start[:, None]).astype(jnp.int32), axis=1),
                        N_EXPERTS - 1)
    n_used = (pend[-1] // rb).reshape(1)
    tok = jnp.broadcast_to(jnp.arange(T, dtype=jnp.int32)[:, None], (T, TOP_K))
    row_tok = jnp.zeros((n_blk * rb,), jnp.int32).at[pos.reshape(-1)].set(
        tok.reshape(-1), unique_indices=True, mode="drop")
    return row_tok, blk_e, n_used, pos


def _largest_tile(n, cap):
    t = cap
    while n % t:
        t //= 2
    return t


def kernel(x_prompt, x_sample, cache_kv_latent, cache_k_rope, state_conv, state_ssm, ln0_g, ln0_b, w_in, q_norm, w_uq, kv_norm, w_uk, w_uv, conv_w, conv_b, dt_bias, a_log, d_skip, ssm_norm, w_o, ln1_g, ln1_b, w_rg, b_rg, w_re, b_re, w_gate, w_up, w_down, ln2_g, ln2_b):
    B, L, _ = x_prompt.shape
    Bs, Ls, _ = x_sample.shape
    past = cache_kv_latent.shape[2]
    Tp, Ts = B * L, Bs * Ls
    pw = _prep_weights(ln0_g, ln0_b, w_in[0], q_norm[0], w_uq[0], kv_norm[0], w_uk[0], w_uv[0], conv_w[0],
                       conv_b[0], dt_bias[0], a_log[0], d_skip[0], ssm_norm[0], w_o[0], ln1_g[0], ln1_b[0],
                       w_rg[0], b_rg[0], w_re[0], b_re[0], ln2_g[0], ln2_b[0])

    tm_p = _largest_tile(L, 256)
    xp = x_prompt.reshape(Tp, D_MODEL)
    cos_p, sin_p = _rope_tables(jnp.arange(L, dtype=jnp.int32))
    q_p, kv_p, lat_p, kr_p, krp_p, z_p, xbc_p, dt_p = _inproj(
        xp, pw, cos_p, sin_p, tm=tm_p, pos_blocks=L // tm_p, with_kv=True)
    attn_p = _attn_prompt(q_p.reshape(B, L, -1), kv_p.reshape(B, L, -1), krp_p.reshape(B, L, -1),
                          tq=_largest_tile(L, 512), tk=_largest_tile(L, 1024))
    q_ssd = min(CHUNK, L)
    y_p, conv_p, ssm_p = _ssd(
        xbc_p.reshape(B, L, -1), dt_p.reshape(B, L, -1), z_p.reshape(B, L, -1),
        jnp.zeros((B, CONV_K - 1, CONV_CH), F32), jnp.zeros((B, D_SSM, D_STATE), F32), pw,
        q=q_ssd, tt=_largest_tile(L, 256))

    xs = x_sample.reshape(Ts, D_MODEL)
    cos_s, sin_s = _rope_tables(past + jnp.arange(Ls, dtype=jnp.int32))
    tm_s = _largest_tile(Ts, 256)
    assert tm_s % Ls == 0
    q_s, _, lat_s, kr_s, krp_s, z_s, xbc_s, dt_s = _inproj(
        xs, pw, jnp.tile(cos_s, (tm_s // Ls, 1)), jnp.tile(sin_s, (tm_s // Ls, 1)),
        tm=tm_s, pos_blocks=1, with_kv=False)
    attn_s = _attn_sample(q_s.reshape(Bs, Ls, -1), pw, cache_kv_latent[0], cache_k_rope[0],
                          lat_s.reshape(Bs, Ls, -1), krp_s.reshape(Bs, Ls, -1),
                          tk=_largest_tile(past, 1024))
    y_s, conv_s, ssm_s = _ssd(
        xbc_s.reshape(Bs, Ls, -1), dt_s.reshape(Bs, Ls, -1), z_s.reshape(Bs, Ls, -1),
        state_conv[0], state_ssm[0].reshape(Bs, D_SSM, D_STATE), pw, q=min(CHUNK, Ls), tt=Ls)

    T = Tp + Ts
    tm_c = _largest_tile(math.gcd(Tp, Ts), 256)
    h_all, route, counts = _outproj(xp, xs, attn_p.reshape(Tp, -1), attn_s.reshape(Ts, -1),
                                    y_p.reshape(Tp, -1), y_s.reshape(Ts, -1), pw, tm=tm_c)
    rb = 256
    n_blk = (T * TOP_K + N_EXPERTS * (rb - 1) + rb - 1) // rb
    row_tok, blk_e, n_used, pos = _dispatch_tables(route, counts[0, R_OFF:R_OFF + N_EXPERTS], rb=rb, n_blk=n_blk)
    y_rows = _experts(h_all, row_tok, blk_e, n_used, w_gate[0], w_up[0], w_down[0], rb=rb)
    pos_tiles = pos.reshape(T // tm_c, tm_c, TOP_K).transpose(0, 2, 1).reshape(-1)
    out_p, out_s = _combine(y_rows, pos_tiles, h_all, route, pw, tm=tm_c, t_prompt=Tp)

    return (out_p.reshape(B, L, D_MODEL), out_s.reshape(Bs, Ls, D_MODEL),
            lat_p.reshape(1, B, L, KV_LORA), lat_s.reshape(1, Bs, Ls, KV_LORA),
            kr_p.reshape(1, B, L, QK_ROPE), kr_s.reshape(1, Bs, Ls, QK_ROPE),
            conv_p[None], conv_s[None],
            ssm_p.reshape(1, B, SSM_HEADS, SSM_HEAD_DIM, D_STATE),
            ssm_s.reshape(1, Bs, SSM_HEADS, SSM_HEAD_DIM, D_STATE))
```

```python
import functools
import math

import jax
import jax.numpy as jnp
import numpy as np
from jax import lax
from jax.experimental import pallas as pl
from jax.experimental.pallas import tpu as pltpu

F32 = jnp.float32
BF16 = jnp.bfloat16

D_MODEL = 2048
CHUNK = 64
MLA_HEADS = 8
QK_NOPE = 128
QK_ROPE = 64
QK_DIM = QK_NOPE + QK_ROPE
V_DIM = 128
Q_LORA = 512
KV_LORA = 512
ROPE_THETA = 10000.0
ATTN_SCALE = QK_DIM ** -0.5
D_ATTN = MLA_HEADS * V_DIM
D_SSM = 1024
SSM_HEAD_DIM = 64
SSM_HEADS = D_SSM // SSM_HEAD_DIM
SSM_GROUPS = 2
HEADS_PER_GROUP = SSM_HEADS // SSM_GROUPS
D_STATE = 128
CONV_K = 4
CONV_CH = D_SSM + 2 * SSM_GROUPS * D_STATE
N_EXPERT_GROUPS = 8
EXPERTS_PER_GROUP = 8
N_EXPERTS = N_EXPERT_GROUPS * EXPERTS_PER_GROUP
TOP_K = 2
D_EXPERT = 512
DEPTH = 1
ALPHA = (2 * DEPTH) ** 0.25
RMS_EPS = 1e-6
LN_EPS = 1e-5

LANES = 128
MXU_DIM = 256
VMEM_LIMIT = 56 * 1024 * 1024

QK_PAD = 2 * LANES
C_Q, C_KV, C_Z, C_XBC, C_KR, C_DT = 0, 512, 1024, 2048, 3584, 3712
D_IN_PAD = 3840


def _const_spec(shape):
    nd = len(shape)
    return pl.BlockSpec(shape, lambda *_: (0,) * nd, pipeline_mode=pl.Buffered(1))


def _split3(a):
    hi = a.astype(BF16)
    r1 = a - hi.astype(F32)
    mid = r1.astype(BF16)
    lo = (r1 - mid.astype(F32)).astype(BF16)
    return hi, mid, lo


def _dot(a, b):
    return jnp.dot(a, b, preferred_element_type=F32)


def _dot_nt(a, b):
    return lax.dot_general(a, b, (((1,), (1,)), ((), ())), preferred_element_type=F32)


def _dot_tn(a, b):
    return lax.dot_general(a, b, (((0,), (0,)), ((), ())), preferred_element_type=F32)


def _exact_dot(a_f32, sel_bf16):
    hi, mid, lo = _split3(a_f32)
    return _dot(hi, sel_bf16) + _dot(mid, sel_bf16) + _dot(lo, sel_bf16)


def _exact_dot_l(sel_bf16, a_f32):
    hi, mid, lo = _split3(a_f32)
    return _dot(sel_bf16, hi) + _dot(sel_bf16, mid) + _dot(sel_bf16, lo)


def _layernorm(x, g, b):
    mu = jnp.mean(x, axis=-1, keepdims=True)
    xc = x - mu
    var = jnp.mean(xc * xc, axis=-1, keepdims=True)
    return xc * lax.rsqrt(var + LN_EPS) * g + b


def _rmsnorm(x, g):
    return x * lax.rsqrt(jnp.mean(x * x, axis=-1, keepdims=True) + RMS_EPS) * g


def _silu(x):
    return x * (1.0 / (1.0 + jnp.exp(-x)))


def _pack_bf16_pairs(x):
    n = x.shape[1] // 2
    bits = lax.bitcast_convert_type(x, jnp.uint32)
    return (bits[:, n:] & jnp.uint32(0xFFFF0000)) | (bits[:, :n] >> 16)


def _unpack_bf16_pairs(u):
    lo = lax.bitcast_convert_type(u << 16, F32)
    hi = lax.bitcast_convert_type(u & jnp.uint32(0xFFFF0000), F32)
    return jnp.concatenate([lo, hi], axis=1)


def _rope128(x, cos_t, sin_t):
    lane = lax.broadcasted_iota(jnp.int32, x.shape, 1)
    partner = jnp.where(lane < QK_ROPE // 2, pltpu.roll(x, LANES - QK_ROPE // 2, 1),
                        pltpu.roll(x, QK_ROPE // 2, 1))
    return x * cos_t + partner * sin_t


def _inproj_kernel(x_ref, g0_ref, b0_ref, w_in_ref, qn_ref, kvn_ref, w_uq_ref, w_ukv_ref,
                   cos_ref, sin_ref, dtb_ref,
                   q_ref, kv_ref, lat_ref, kr_ref, krp_ref, z_ref, xbc_ref, dt_ref, *, with_kv):
    xn = _layernorm(x_ref[...], g0_ref[...], b0_ref[...]).astype(BF16)
    cos_t = cos_ref[...]
    sin_t = sin_ref[...]

    c_q = _dot(xn, w_in_ref[:, C_Q:C_Q + Q_LORA])
    qb = _rmsnorm(c_q, qn_ref[...]).astype(BF16)
    for h in range(MLA_HEADS):
        qh = _dot(qb, w_uq_ref[:, h * QK_PAD:(h + 1) * QK_PAD])
        q_ref[:, h * QK_PAD:h * QK_PAD + LANES] = qh[:, :LANES].astype(BF16)
        q_ref[:, h * QK_PAD + LANES:(h + 1) * QK_PAD] = _rope128(qh[:, LANES:], cos_t, sin_t).astype(BF16)

    c_kv = _dot(xn, w_in_ref[:, C_KV:C_KV + KV_LORA])
    lat = _rmsnorm(c_kv, kvn_ref[...])
    lat_ref[...] = lat
    if with_kv:
        kv_ref[...] = _dot(lat.astype(BF16), w_ukv_ref[...]).astype(BF16)
    else:
        kv_ref[...] = jnp.zeros(kv_ref.shape, BF16)

    k_r = _rope128(_dot(xn, w_in_ref[:, C_KR:C_KR + LANES]), cos_t, sin_t)
    kr_ref[...] = k_r[:, :QK_ROPE]
    krp_ref[...] = k_r.astype(BF16)

    z_ref[...] = _dot(xn, w_in_ref[:, C_Z:C_Z + D_SSM]).astype(BF16)
    xbc_ref[...] = _dot(xn, w_in_ref[:, C_XBC:C_XBC + CONV_CH])

    dt_raw = _dot(xn, w_in_ref[:, C_DT:C_DT + LANES]) + dtb_ref[...]
    dt = jnp.maximum(dt_raw, 0.0) + jnp.log1p(jnp.exp(-jnp.abs(dt_raw)))
    lane = lax.broadcasted_iota(jnp.int32, dt.shape, 1)
    dt_ref[...] = jnp.where(lane < SSM_HEADS, dt, 0.0)


def _inproj(x, pw, cos_t, sin_t, *, tm, pos_blocks, with_kv):
    T = x.shape[0]
    n = T // tm
    row = lambda i: (i, 0)
    pos = lambda i: (i % pos_blocks, 0)
    kv_cols = 2 * D_ATTN if with_kv else LANES
    out_shape = [
        jax.ShapeDtypeStruct((T, MLA_HEADS * QK_PAD), BF16),
        jax.ShapeDtypeStruct((T, kv_cols), BF16),
        jax.ShapeDtypeStruct((T, KV_LORA), F32),
        jax.ShapeDtypeStruct((T, QK_ROPE), F32),
        jax.ShapeDtypeStruct((T, LANES), BF16),
        jax.ShapeDtypeStruct((T, D_SSM), BF16),
        jax.ShapeDtypeStruct((T, CONV_CH), F32),
        jax.ShapeDtypeStruct((T, LANES), F32),
    ]
    out_specs = [pl.BlockSpec((tm, s.shape[1]), row) for s in out_shape]
    in_specs = [
        pl.BlockSpec((tm, D_MODEL), row),
        _const_spec((1, D_MODEL)), _const_spec((1, D_MODEL)),
        _const_spec((D_MODEL, D_IN_PAD)),
        _const_spec((1, Q_LORA)), _const_spec((1, KV_LORA)),
        _const_spec((Q_LORA, MLA_HEADS * QK_PAD)),
        _const_spec((KV_LORA, 2 * D_ATTN)),
        pl.BlockSpec((tm, LANES), pos), pl.BlockSpec((tm, LANES), pos),
        _const_spec((1, LANES)),
    ]
    return pl.pallas_call(
        functools.partial(_inproj_kernel, with_kv=with_kv),
        grid=(n,), in_specs=in_specs, out_specs=out_specs, out_shape=out_shape,
        compiler_params=pltpu.CompilerParams(dimension_semantics=("arbitrary",),
                                             vmem_limit_bytes=VMEM_LIMIT),
        name="inproj",
    )(x, pw["ln0_g"], pw["ln0_b"], pw["w_in"], pw["q_norm"], pw["kv_norm"], pw["w_uq"],
      pw["w_ukv"], cos_t, sin_t, pw["dt_bias"])


def _attn_prompt_kernel(qi_ref, ki_ref, q_ref, kv_ref, krp_ref, o_ref, *scratch, tq, tk):
    m_refs = scratch[0:MLA_HEADS]
    acc_refs = scratch[MLA_HEADS:2 * MLA_HEADS]
    p_id = pl.program_id(1)
    qi = qi_ref[p_id]
    ki = ki_ref[p_id]
    k_last = ((qi + 1) * tq - 1) // tk

    @pl.when(ki == 0)
    def _():
        for h in range(MLA_HEADS):
            m_refs[h][...] = jnp.full(m_refs[h].shape, -jnp.inf, F32)
            acc_refs[h][...] = jnp.zeros(acc_refs[h].shape, F32)

    def step(masked):
        krp = krp_ref[...]
        ones_col = (lax.broadcasted_iota(jnp.int32, (tk, LANES), 1) == 0).astype(BF16)
        if masked:
            r = (qi * tq + lax.broadcasted_iota(jnp.int32, (tq, tk), 0)) // CHUNK
            c = (ki * tk + lax.broadcasted_iota(jnp.int32, (tq, tk), 1)) // CHUNK
            allowed = c <= r
        for h in range(MLA_HEADS):
            qh = q_ref[:, h * QK_PAD:(h + 1) * QK_PAD]
            kh = jnp.concatenate([kv_ref[:, h * QK_NOPE:(h + 1) * QK_NOPE], krp], axis=1)
            s = _dot_nt(qh, kh)
            if masked:
                s = jnp.where(allowed, s, -jnp.inf)
            m_old = m_refs[h][...]
            m_new = jnp.maximum(m_old, jnp.max(s, axis=1, keepdims=True))
            alpha = jnp.exp2(m_old - m_new)
            p = jnp.exp2(s - m_new).astype(BF16)
            v_aug = jnp.concatenate([kv_ref[:, D_ATTN + h * V_DIM:D_ATTN + (h + 1) * V_DIM], ones_col], axis=1)
            acc_refs[h][...] = alpha * acc_refs[h][...] + _dot(p, v_aug)
            m_refs[h][...] = m_new

    @pl.when(ki < k_last)
    def _():
        step(False)

    @pl.when(ki == k_last)
    def _():
        step(True)
        for h in range(MLA_HEADS):
            acc = acc_refs[h][...]
            o_ref[:, h * V_DIM:(h + 1) * V_DIM] = (acc[:, :V_DIM] / acc[:, V_DIM:V_DIM + 1]).astype(BF16)


def _attn_prompt(q, kv, krp, *, tq, tk):
    B, L, _ = q.shape
    nq = L // tq
    pairs = [(i, j) for i in range(nq) for j in range(((i + 1) * tq - 1) // tk + 1)]
    qi_tab = jnp.asarray(np.array([p[0] for p in pairs], np.int32))
    ki_tab = jnp.asarray(np.array([p[1] for p in pairs], np.int32))
    grid_spec = pltpu.PrefetchScalarGridSpec(
        num_scalar_prefetch=2,
        grid=(B, len(pairs)),
        in_specs=[
            pl.BlockSpec((None, tq, MLA_HEADS * QK_PAD), lambda b, p, qi, ki: (b, qi[p], 0)),
            pl.BlockSpec((None, tk, 2 * D_ATTN), lambda b, p, qi, ki: (b, ki[p], 0)),
            pl.BlockSpec((None, tk, LANES), lambda b, p, qi, ki: (b, ki[p], 0)),
        ],
        out_specs=pl.BlockSpec((None, tq, D_ATTN), lambda b, p, qi, ki: (b, qi[p], 0)),
        scratch_shapes=([pltpu.VMEM((tq, 1), F32)] * MLA_HEADS
                        + [pltpu.VMEM((tq, 2 * V_DIM), F32)] * MLA_HEADS),
    )
    return pl.pallas_call(
        functools.partial(_attn_prompt_kernel, tq=tq, tk=tk),
        grid_spec=grid_spec,
        out_shape=jax.ShapeDtypeStruct((B, L, D_ATTN), BF16),
        compiler_params=pltpu.CompilerParams(dimension_semantics=("arbitrary", "arbitrary"),
                                             vmem_limit_bytes=VMEM_LIMIT),
        name="attn_prompt",
    )(qi_tab, ki_tab, q, kv, krp)


def _attn_sample_kernel(q_ref, wukt_ref, wuv_ref, latp_ref, krpast_ref, latn_ref, krn_ref, o_ref,
                        qlat_ref, qrp_ref, m_ref, l_ref, acc_ref, *, ls, past, tk, n_new_pad):
    k = pl.program_id(1)
    nk = pl.num_programs(1)
    rows = MLA_HEADS * ls

    @pl.when(k == 0)
    def _():
        for h in range(MLA_HEADS):
            qn = q_ref[:, h * QK_PAD:h * QK_PAD + LANES]
            qlat_ref[h * ls:(h + 1) * ls, :] = _dot(qn, wukt_ref[h]).astype(BF16)
            qrp_ref[h * ls:(h + 1) * ls, :] = q_ref[:, h * QK_PAD + LANES:(h + 1) * QK_PAD]
        m_ref[...] = jnp.full(m_ref.shape, -jnp.inf, F32)
        l_ref[...] = jnp.zeros(l_ref.shape, F32)
        acc_ref[...] = jnp.zeros(acc_ref.shape, F32)

    q_chunk = (past + lax.broadcasted_iota(jnp.int32, (rows, 1), 0) % ls) // CHUNK

    def update(s, lat_b):
        m_old = m_ref[...]
        m_new = jnp.maximum(m_old, jnp.max(s, axis=1, keepdims=True))
        alpha = jnp.exp2(m_old - m_new)
        p = jnp.exp2(s - m_new)
        l_ref[...] = alpha * l_ref[...] + jnp.sum(p, axis=1, keepdims=True)
        acc_ref[...] = alpha * acc_ref[...] + _dot(p.astype(BF16), lat_b)
        m_ref[...] = m_new

    lat_b = latp_ref[...].astype(BF16)
    kr_b = krpast_ref[...].astype(BF16)
    s = _dot_nt(qlat_ref[...], lat_b) + _dot_nt(qrp_ref[:, :QK_ROPE], kr_b)
    k_pos = k * tk + lax.broadcasted_iota(jnp.int32, (rows, tk), 1)
    s = jnp.where(k_pos // CHUNK <= q_chunk, s, -jnp.inf)
    update(s, lat_b)

    @pl.when(k == nk - 1)
    def _():
        latn_b = latn_ref[...].astype(BF16)
        s2 = _dot_nt(qlat_ref[...], latn_b) + _dot_nt(qrp_ref[...], krn_ref[...])
        j = lax.broadcasted_iota(jnp.int32, (rows, n_new_pad), 1)
        ok = ((past + j) // CHUNK <= q_chunk) & (j < ls)
        update(jnp.where(ok, s2, -jnp.inf), latn_b)
        o_lat = (acc_ref[...] / l_ref[...]).astype(BF16)
        for h in range(MLA_HEADS):
            o_ref[:, h * V_DIM:(h + 1) * V_DIM] = _dot(o_lat[h * ls:(h + 1) * ls, :], wuv_ref[h]).astype(BF16)


def _attn_sample(q, pw, lat_past, kr_past, lat_new, krp_new, *, tk):
    Bs, Ls, _ = q.shape
    past = lat_past.shape[1]
    n_new_pad = LANES
    lat_new = jnp.pad(lat_new, ((0, 0), (0, n_new_pad - Ls), (0, 0)))
    krp_new = jnp.pad(krp_new, ((0, 0), (0, n_new_pad - Ls), (0, 0)))
    rows = MLA_HEADS * Ls
    return pl.pallas_call(
        functools.partial(_attn_sample_kernel, ls=Ls, past=past, tk=tk, n_new_pad=n_new_pad),
        grid=(Bs, past // tk),
        in_specs=[
            pl.BlockSpec((None, Ls, MLA_HEADS * QK_PAD), lambda b, k: (b, 0, 0)),
            _const_spec((MLA_HEADS, QK_NOPE, KV_LORA)),
            _const_spec((MLA_HEADS, KV_LORA, V_DIM)),
            pl.BlockSpec((None, tk, KV_LORA), lambda b, k: (b, k, 0)),
            pl.BlockSpec((None, tk, QK_ROPE), lambda b, k: (b, k, 0)),
            pl.BlockSpec((None, n_new_pad, KV_LORA), lambda b, k: (b, 0, 0)),
            pl.BlockSpec((None, n_new_pad, LANES), lambda b, k: (b, 0, 0)),
        ],
        out_specs=pl.BlockSpec((None, Ls, D_ATTN), lambda b, k: (b, 0, 0)),
        out_shape=jax.ShapeDtypeStruct((Bs, Ls, D_ATTN), BF16),
        scratch_shapes=[pltpu.VMEM((rows, KV_LORA), BF16), pltpu.VMEM((rows, LANES), BF16),
                        pltpu.VMEM((rows, 1), F32), pltpu.VMEM((rows, 1), F32),
                        pltpu.VMEM((rows, KV_LORA), F32)],
        compiler_params=pltpu.CompilerParams(dimension_semantics=("arbitrary", "arbitrary"),
                                             vmem_limit_bytes=VMEM_LIMIT),
        name="attn_sample",
    )(q, pw["w_ukt"], pw["w_uvh"], lat_past, kr_past, lat_new, krp_new)


def _ssd_consts(q):
    hb = MXU_DIM // q
    hq = SSM_HEADS * q
    lane = np.arange(hq)
    ltri = (np.arange(q)[None, :] <= np.arange(q)[:, None]).astype(np.float32)
    sel_c = (np.arange(LANES)[:, None] == (lane // q)[None, :]).astype(np.float32)
    sel_p = (np.arange(LANES)[:, None] == (np.arange(D_SSM) // SSM_HEAD_DIM)[None, :]).astype(np.float32)
    diag = (np.arange(q)[:, None] == (lane % q)[None, :]).astype(np.float32)
    causal = (np.arange(q)[:, None] >= (lane % q)[None, :]).astype(np.float32)
    bd_rows = np.arange(hb * q) // q
    bd_cols = np.arange(hb * SSM_HEAD_DIM) // SSM_HEAD_DIM
    bdmask = (bd_rows[:, None] == bd_cols[None, :]).astype(np.float32)
    return dict(ltri=jnp.asarray(ltri, BF16), sel_c=jnp.asarray(sel_c, BF16),
                sel_p=jnp.asarray(sel_p, BF16), diag=jnp.asarray(diag, F32),
                causal=jnp.asarray(causal, F32), bdmask=jnp.asarray(bdmask, BF16))


def _ssd_kernel(xbc_ref, dt_ref, z_ref, cprev_ref, h0_ref, cw_ref, cb_ref, a_ref, dskip_ref, norm_ref,
                ltri_ref, selc_ref, selp_ref, diag_ref, causal_ref, bdmask_ref,
                y_ref, cout_ref, hout_ref,
                ext_ref, u_ref, ht_ref, *, q, tt):
    t = pl.program_id(1)
    nt = pl.num_programs(1)
    hb = MXU_DIM // q
    nblk = SSM_HEADS // hb
    gw = HEADS_PER_GROUP * SSM_HEAD_DIM
    pad = 8

    @pl.when(t == 0)
    def _():
        ext_ref[0:pad, :] = jnp.zeros((pad, CONV_CH), F32)
        ext_ref[pad - (CONV_K - 1):pad, :] = cprev_ref[...]
        ht_ref[...] = h0_ref[...].T

    xbc = xbc_ref[...]
    ext_ref[pad:pad + tt, :] = xbc
    cout_ref[...] = xbc[tt - (CONV_K - 1):, :]
    conv = cb_ref[...] + cw_ref[0:1, :] * ext_ref[pad - 3:pad - 3 + tt, :]
    for k in range(1, CONV_K):
        conv = conv + cw_ref[k:k + 1, :] * ext_ref[pad - 3 + k:pad - 3 + k + tt, :]
    u_ref[...] = _silu(conv)
    ext_ref[pad - (CONV_K - 1):pad, :] = xbc[tt - (CONV_K - 1):, :]

    a_row = a_ref[...]
    d_row = dskip_ref[...]
    norm_row = norm_ref[...]

    def chunk(c, carry):
        r0 = pl.multiple_of(c * q, q)
        u = u_ref[pl.ds(r0, q), :]
        xs = u[:, :D_SSM]
        dt = dt_ref[pl.ds(r0, q), :]
        a_cs = _exact_dot_l(ltri_ref[...], dt * a_row)
        ap = _exact_dot(a_cs, selp_ref[...])
        dtp = _exact_dot(dt, selp_ref[...])
        a_i = ap if q == SSM_HEAD_DIM else _exact_dot(a_cs, selc_ref[...])
        a_j = jnp.sum(a_i * diag_ref[...], axis=0, keepdims=True)
        decay = jnp.exp(jnp.where(causal_ref[...] > 0.5, a_i - a_j, -jnp.inf))
        ap_last = ap[q - 1:q, :]
        xdt = (xs * dtp).astype(BF16)
        xw = (xs * (jnp.exp(ap_last - ap) * dtp)).astype(BF16)
        e_ap = jnp.exp(ap)

        cb_parts, y_off_parts = [], []
        for g in range(SSM_GROUPS):
            bm = u[:, D_SSM + g * D_STATE:D_SSM + (g + 1) * D_STATE].astype(BF16)
            cm = u[:, D_SSM + (SSM_GROUPS + g) * D_STATE:D_SSM + (SSM_GROUPS + g + 1) * D_STATE].astype(BF16)
            cb_parts.append(_dot_nt(cm, jnp.concatenate([bm] * HEADS_PER_GROUP, axis=0)))
            gl = slice(g * gw, (g + 1) * gw)
            ht_g = ht_ref[:, gl]
            y_off_parts.append(_dot(cm, ht_g.astype(BF16)) * e_ap[:, gl])
            st = _dot_tn(bm, xw[:, gl])
            ht_ref[:, gl] = ht_g * jnp.exp(ap_last[:, gl]) + st
        w_all = (jnp.concatenate(cb_parts, axis=1) * decay).astype(BF16)
        y_diag_parts = []
        for b in range(nblk):
            x_b = xdt[:, b * hb * SSM_HEAD_DIM:(b + 1) * hb * SSM_HEAD_DIM]
            bd = jnp.concatenate([x_b] * hb, axis=0) * bdmask_ref[...]
            y_diag_parts.append(_dot(w_all[:, b * MXU_DIM:(b + 1) * MXU_DIM], bd))
        y = jnp.concatenate(y_diag_parts, axis=1) + jnp.concatenate(y_off_parts, axis=1)

        y = y + d_row * xs
        y = y * _silu(z_ref[pl.ds(r0, q), :].astype(F32))
        outs = []
        for g in range(SSM_GROUPS):
            yg = y[:, g * gw:(g + 1) * gw]
            outs.append(yg * lax.rsqrt(jnp.mean(yg * yg, axis=1, keepdims=True) + RMS_EPS))
        y_ref[pl.ds(r0, q), :] = (jnp.concatenate(outs, axis=1) * norm_row).astype(BF16)
        return carry

    lax.fori_loop(0, tt // q, chunk, 0, unroll=True)

    @pl.when(t == nt - 1)
    def _():
        hout_ref[...] = ht_ref[...].T


def _ssd(xbc, dt, z, conv_prev, h0, pw, *, q, tt):
    B, L, _ = xbc.shape
    cs = _ssd_consts(q)
    hq = SSM_HEADS * q
    hb = MXU_DIM // q
    tile = lambda w: pl.BlockSpec((None, tt, w), lambda b, t: (b, t, 0))
    per_b = lambda s: pl.BlockSpec((None,) + s, lambda b, t: (b, 0, 0))
    return pl.pallas_call(
        functools.partial(_ssd_kernel, q=q, tt=tt),
        grid=(B, L // tt),
        in_specs=[tile(CONV_CH), tile(LANES), tile(D_SSM), per_b((CONV_K - 1, CONV_CH)),
                  per_b((D_SSM, D_STATE)),
                  _const_spec((CONV_K, CONV_CH)), _const_spec((1, CONV_CH)), _const_spec((1, LANES)),
                  _const_spec((1, D_SSM)), _const_spec((1, D_SSM)),
                  _const_spec((q, q)), _const_spec((LANES, hq)), _const_spec((LANES, D_SSM)),
                  _const_spec((q, hq)), _const_spec((q, hq)),
                  _const_spec((MXU_DIM, hb * SSM_HEAD_DIM))],
        out_specs=[tile(D_SSM), per_b((CONV_K - 1, CONV_CH)), per_b((D_SSM, D_STATE))],
        out_shape=[jax.ShapeDtypeStruct((B, L, D_SSM), BF16),
                   jax.ShapeDtypeStruct((B, CONV_K - 1, CONV_CH), F32),
                   jax.ShapeDtypeStruct((B, D_SSM, D_STATE), F32)],
        scratch_shapes=[pltpu.VMEM((8 + tt, CONV_CH), F32), pltpu.VMEM((tt, CONV_CH), F32),
                        pltpu.VMEM((D_STATE, D_SSM), F32)],
        compiler_params=pltpu.CompilerParams(dimension_semantics=("arbitrary", "arbitrary"),
                                             vmem_limit_bytes=VMEM_LIMIT),
        name="ssd",
    )(xbc, dt, z, conv_prev, h0, pw["conv_w"], pw["conv_b"], pw["a_neg"], pw["d_skip"], pw["ssm_norm"],
      cs["ltri"], cs["sel_c"], cs["sel_p"], cs["diag"], cs["causal"], cs["bdmask"])


R_OFF = N_EXPERT_GROUPS


def _outproj_kernel(xp_ref, xs_ref, attnp_ref, attns_ref, yp_ref, ys_ref, g0_ref, b0_ref, wo_a_ref, wo_s_ref,
                    g1_ref, b1_ref, wr_ref, br_ref, ltri_ref, h_ref, hpk_ref, route_ref, cnt_ref,
                    *, n_prompt_tiles):
    i = pl.program_id(0)
    is_p = i < n_prompt_tiles
    x = jnp.where(is_p, xp_ref[...], xs_ref[...])
    attn = jnp.where(is_p, attnp_ref[...], attns_ref[...])
    yssm = jnp.where(is_p, yp_ref[...], ys_ref[...])

    @pl.when(i == 0)
    def _():
        cnt_ref[...] = jnp.zeros(cnt_ref.shape, F32)

    xn = _layernorm(x, g0_ref[...], b0_ref[...])
    mixed = _dot(attn, wo_a_ref[...]) + _dot(yssm, wo_s_ref[...])
    h = _layernorm(ALPHA * xn + mixed, g1_ref[...], b1_ref[...])
    h_ref[...] = h

    h_hi = h.astype(BF16)
    h_hi32 = h_hi.astype(F32)
    hpk_ref[...] = _pack_bf16_pairs(h_hi32)
    h_lo = (h - h_hi32).astype(BF16)
    tm = h.shape[0]
    prod = _dot(jnp.concatenate([h_hi, h_lo], axis=0), wr_ref[...])
    lg = (prod[:tm, :LANES] + prod[:tm, LANES:]) + (prod[tm:, :LANES] + prod[tm:, LANES:]) + br_ref[...]
    lane = lax.broadcasted_iota(jnp.int32, lg.shape, 1)
    big = jnp.int32(1 << 20)
    gl = jnp.where(lane < N_EXPERT_GROUPS, lg, -jnp.inf)
    gmax = jnp.max(gl, axis=1, keepdims=True)
    grp = jnp.min(jnp.where(gl == gmax, lane, big), axis=1, keepdims=True)
    g_w = 1.0 / jnp.sum(jnp.exp(gl - gmax), axis=1, keepdims=True)
    in_grp = (lane >= R_OFF) & (lane < R_OFF + N_EXPERTS) & ((lane - R_OFF) // EXPERTS_PER_GROUP == grp)
    el = jnp.where(in_grp, lg, -jnp.inf)
    emax = jnp.max(el, axis=1, keepdims=True)
    ee = jnp.exp(el - emax)
    prob = jnp.where(in_grp, ee / jnp.sum(ee, axis=1, keepdims=True), -1.0)
    p1 = jnp.max(prob, axis=1, keepdims=True)
    i1 = jnp.min(jnp.where(prob == p1, lane, big), axis=1, keepdims=True)
    prob2 = jnp.where(lane == i1, -1.0, prob)
    p2 = jnp.max(prob2, axis=1, keepdims=True)
    i2 = jnp.min(jnp.where(prob2 == p2, lane, big), axis=1, keepdims=True)
    denom = p1 + p2
    oh1 = (lane == i1).astype(F32)
    oh2 = (lane == i2).astype(F32)
    oh = oh1 + oh2
    before = _dot(ltri_ref[...], oh.astype(BF16)) + cnt_ref[0:1, :]
    rank1 = jnp.sum(before * oh1, axis=1, keepdims=True)
    rank2 = jnp.sum(before * oh2, axis=1, keepdims=True)
    cnt_ref[...] = cnt_ref[...] + jnp.sum(oh, axis=0, keepdims=True)
    route = jnp.where(lane == 0, (i1 - R_OFF).astype(F32),
                      jnp.where(lane == 1, (i2 - R_OFF).astype(F32),
                                jnp.where(lane == 2, g_w * p1 / denom,
                                          jnp.where(lane == 3, g_w * p2 / denom,
                                                    jnp.where(lane == 4, rank1,
                                                              jnp.where(lane == 5, rank2, 0.0))))))
    route_ref[...] = route


def _outproj(xp, xs, attn_p, attn_s, y_p, y_s, pw, *, tm):
    Tp, Ts = xp.shape[0], xs.shape[0]
    npt, nst = Tp // tm, Ts // tm
    T = Tp + Ts
    row = lambda i: (i, 0)
    prow = lambda i: (jnp.minimum(i, npt - 1), 0)
    srow = lambda i: (jnp.maximum(i - npt, 0), 0)
    ltri = jnp.asarray(np.tril(np.ones((tm, tm), np.float32), -1), BF16)
    return pl.pallas_call(
        functools.partial(_outproj_kernel, n_prompt_tiles=npt),
        grid=(npt + nst,),
        in_specs=[pl.BlockSpec((tm, D_MODEL), prow), pl.BlockSpec((tm, D_MODEL), srow),
                  pl.BlockSpec((tm, D_ATTN), prow), pl.BlockSpec((tm, D_ATTN), srow),
                  pl.BlockSpec((tm, D_SSM), prow), pl.BlockSpec((tm, D_SSM), srow),
                  _const_spec((1, D_MODEL)), _const_spec((1, D_MODEL)),
                  _const_spec((D_ATTN, D_MODEL)), _const_spec((D_SSM, D_MODEL)),
                  _const_spec((1, D_MODEL)), _const_spec((1, D_MODEL)),
                  _const_spec((D_MODEL, 2 * LANES)), _const_spec((1, LANES)),
                  _const_spec((tm, tm))],
        out_specs=[pl.BlockSpec((tm, D_MODEL), row), pl.BlockSpec((tm, D_MODEL // 2), row),
                   pl.BlockSpec((tm, LANES), row), pl.BlockSpec((8, LANES), lambda i: (0, 0))],
        out_shape=[jax.ShapeDtypeStruct((T, D_MODEL), F32), jax.ShapeDtypeStruct((T, D_MODEL // 2), jnp.uint32),
                   jax.ShapeDtypeStruct((T, LANES), F32), jax.ShapeDtypeStruct((8, LANES), F32)],
        compiler_params=pltpu.CompilerParams(dimension_semantics=("arbitrary",),
                                             vmem_limit_bytes=VMEM_LIMIT),
        name="outproj",
    )(xp, xs, attn_p, attn_s, y_p, y_s, pw["ln0_g"], pw["ln0_b"], pw["w_o_a"], pw["w_o_s"],
      pw["ln1_g"], pw["ln1_b"], pw["w_r"], pw["b_r"], ltri)


def _experts_kernel(blk_e_ref, n_used_ref, tok_ref, tok_next_ref, h_hbm, wg_ref, wu_ref, wd_ref,
                    y_ref, xbuf, sem, wg_b, wu_b, wd_b, prev_e, *, rb):
    i = pl.program_id(0)
    n_used = n_used_ref[0]
    slot = i % 2

    def row_copy(tok, dst_slot, r):
        return pltpu.make_async_copy(h_hbm.at[pl.ds(tok, 1)], xbuf.at[dst_slot, pl.ds(r, 1)],
                                     sem.at[dst_slot])

    def wait_block(s):
        pltpu.make_async_copy(h_hbm.at[pl.ds(0, rb)], xbuf.at[s], sem.at[s]).wait()

    @pl.when(i == 0)
    def _():
        prev_e[0] = -1

        def body(r, c):
            row_copy(tok_ref[r], 0, r).start()
            return c
        lax.fori_loop(0, rb, body, 0)

    @pl.when(i < n_used)
    def _():
        wait_block(slot)
        e = blk_e_ref[i]

        @pl.when(e != prev_e[0])
        def _():
            wg_b[...] = wg_ref[...].astype(BF16)
            wu_b[...] = wu_ref[...].astype(BF16)
            wd_b[...] = wd_ref[...].astype(BF16)
            prev_e[0] = e

        xb = _unpack_bf16_pairs(xbuf[slot]).astype(BF16)
        for r in range(rb):
            row_copy(tok_next_ref[r], 1 - slot, r).start()
        hid = (_silu(_dot(xb, wg_b[...])) * _dot(xb, wu_b[...])).astype(BF16)
        y = _dot(hid, wd_b[...])
        y_ref[...] = _pack_bf16_pairs(y.astype(BF16).astype(F32))

        @pl.when(i == n_used - 1)
        def _():
            wait_block(1 - slot)

    @pl.when(i >= n_used)
    def _():
        y_ref[...] = jnp.zeros(y_ref.shape, jnp.uint32)


def _experts(hpk_all, row_tok, blk_e, n_used, w_gate, w_up, w_down, *, rb):
    n_blk = blk_e.shape[0]
    n_rows = n_blk * rb
    grid_spec = pltpu.PrefetchScalarGridSpec(
        num_scalar_prefetch=2,
        grid=(n_blk,),
        in_specs=[
            pl.BlockSpec((rb,), lambda i, be, nu: (jnp.minimum(i, nu[0] - 1),), memory_space=pltpu.SMEM),
            pl.BlockSpec((rb,), lambda i, be, nu: (jnp.minimum(i + 1, nu[0] - 1),), memory_space=pltpu.SMEM),
            pl.BlockSpec(memory_space=pl.ANY),
            pl.BlockSpec((None, D_MODEL, D_EXPERT), lambda i, be, nu: (be[jnp.minimum(i, nu[0] - 1)], 0, 0)),
            pl.BlockSpec((None, D_MODEL, D_EXPERT), lambda i, be, nu: (be[jnp.minimum(i, nu[0] - 1)], 0, 0)),
            pl.BlockSpec((None, D_EXPERT, D_MODEL), lambda i, be, nu: (be[jnp.minimum(i, nu[0] - 1)], 0, 0)),
        ],
        out_specs=pl.BlockSpec((rb, D_MODEL // 2), lambda i, be, nu: (i, 0)),
        scratch_shapes=[pltpu.VMEM((2, rb, D_MODEL // 2), jnp.uint32), pltpu.SemaphoreType.DMA((2,)),
                        pltpu.VMEM((D_MODEL, D_EXPERT), BF16), pltpu.VMEM((D_MODEL, D_EXPERT), BF16),
                        pltpu.VMEM((D_EXPERT, D_MODEL), BF16), pltpu.SMEM((1,), jnp.int32)],
    )
    return pl.pallas_call(
        functools.partial(_experts_kernel, rb=rb),
        grid_spec=grid_spec,
        out_shape=jax.ShapeDtypeStruct((n_rows, D_MODEL // 2), jnp.uint32),
        compiler_params=pltpu.CompilerParams(dimension_semantics=("arbitrary",),
                                             vmem_limit_bytes=VMEM_LIMIT),
        name="experts",
    )(blk_e, n_used, row_tok, row_tok, hpk_all, w_gate, w_up, w_down)


def _combine_kernel(pos_ref, pos_next_ref, y_hbm, h_ref, route_ref, g2_ref, b2_ref, op_ref, os_ref, ybuf, sem,
                    *, tm, n_prompt_tiles):
    i = pl.program_id(0)
    n = pl.num_programs(0)
    slot = i % 2
    rows = TOP_K * tm

    def row_copy(idx, dst_slot, r):
        return pltpu.make_async_copy(y_hbm.at[pl.ds(idx, 1)], ybuf.at[dst_slot, pl.ds(r, 1)],
                                     sem.at[dst_slot])

    def wait_tile(s):
        pltpu.make_async_copy(y_hbm.at[pl.ds(0, rows)], ybuf.at[s], sem.at[s]).wait()

    @pl.when(i == 0)
    def _():
        def body(r, c):
            row_copy(pos_ref[r], 0, r).start()
            return c
        lax.fori_loop(0, rows, body, 0)

    wait_tile(slot)
    gate1 = route_ref[:, 2:3]
    gate2 = route_ref[:, 3:4]
    f = (_unpack_bf16_pairs(ybuf[slot, 0:tm, :]) * gate1
         + _unpack_bf16_pairs(ybuf[slot, tm:rows, :]) * gate2)
    for r in range(rows):
        row_copy(pos_next_ref[r], 1 - slot, r).start()
    out = _layernorm(ALPHA * h_ref[...] + f, g2_ref[...], b2_ref[...])

    @pl.when(i < n_prompt_tiles)
    def _():
        op_ref[...] = out

    @pl.when(i >= n_prompt_tiles)
    def _():
        os_ref[...] = out

    @pl.when(i == n - 1)
    def _():
        wait_tile(1 - slot)


def _combine(y_rows, pos_tiles, h_all, route, pw, *, tm, t_prompt):
    T = h_all.shape[0]
    n = T // tm
    npt = t_prompt // tm
    rows = TOP_K * tm
    return pl.pallas_call(
        functools.partial(_combine_kernel, tm=tm, n_prompt_tiles=npt),
        grid=(n,),
        in_specs=[
            pl.BlockSpec((rows,), lambda i: (i,), memory_space=pltpu.SMEM),
            pl.BlockSpec((rows,), lambda i: (jnp.minimum(i + 1, n - 1),), memory_space=pltpu.SMEM),
            pl.BlockSpec(memory_space=pl.ANY),
            pl.BlockSpec((tm, D_MODEL), lambda i: (i, 0)),
            pl.BlockSpec((tm, LANES), lambda i: (i, 0)),
            _const_spec((1, D_MODEL)), _const_spec((1, D_MODEL)),
        ],
        out_specs=[pl.BlockSpec((tm, D_MODEL), lambda i: (jnp.minimum(i, npt - 1), 0)),
                   pl.BlockSpec((tm, D_MODEL), lambda i: (jnp.maximum(i - npt, 0), 0))],
        out_shape=[jax.ShapeDtypeStruct((t_prompt, D_MODEL), F32),
                   jax.ShapeDtypeStruct((T - t_prompt, D_MODEL), F32)],
        scratch_shapes=[pltpu.VMEM((2, rows, D_MODEL // 2), jnp.uint32), pltpu.SemaphoreType.DMA((2,))],
        compiler_params=pltpu.CompilerParams(dimension_semantics=("arbitrary",),
                                             vmem_limit_bytes=VMEM_LIMIT),
        name="combine",
    )(pos_tiles, pos_tiles, y_rows, h_all, route, pw["ln2_g"], pw["ln2_b"])


def _prep_weights(ln0_g, ln0_b, w_in, q_norm, w_uq, kv_norm, w_uk, w_uv, conv_w, conv_b, dt_bias, a_log,
                  d_skip, ssm_norm, w_o, ln1_g, ln1_b, w_rg, b_rg, w_re, b_re, ln2_g, ln2_b):
    s_q, s_kv, s_kr, s_z, s_xbc = Q_LORA, Q_LORA + KV_LORA, Q_LORA + KV_LORA + QK_ROPE, \
        Q_LORA + KV_LORA + QK_ROPE + D_SSM, Q_LORA + KV_LORA + QK_ROPE + D_SSM + CONV_CH
    zc = lambda n: jnp.zeros((D_MODEL, n), F32)
    w_in_p = jnp.concatenate([
        w_in[:, :s_q], w_in[:, s_q:s_kv], w_in[:, s_kr:s_z], w_in[:, s_z:s_xbc],
        w_in[:, s_kv:s_kr], zc(LANES - QK_ROPE), w_in[:, s_xbc:], zc(LANES - SSM_HEADS)], axis=1)
    wq = w_uq.reshape(Q_LORA, MLA_HEADS, QK_DIM) * (ATTN_SCALE * math.log2(math.e))
    wq = jnp.concatenate([wq, jnp.zeros((Q_LORA, MLA_HEADS, QK_PAD - QK_DIM), F32)], axis=2)
    w_ukv = jnp.concatenate([w_uk.reshape(KV_LORA, D_ATTN), w_uv.reshape(KV_LORA, D_ATTN)], axis=1)
    w_r = jnp.concatenate([w_rg, w_re, jnp.zeros((D_MODEL, LANES - R_OFF - N_EXPERTS), F32)], axis=1)
    w_r_hi = w_r.astype(BF16)
    row = lambda v: v.reshape(1, -1)
    pad_row = lambda v: jnp.pad(v, (0, LANES - v.shape[0])).reshape(1, LANES)
    return dict(
        ln0_g=row(ln0_g), ln0_b=row(ln0_b), w_in=w_in_p.astype(BF16),
        q_norm=row(q_norm), kv_norm=row(kv_norm),
        w_uq=wq.reshape(Q_LORA, MLA_HEADS * QK_PAD).astype(BF16), w_ukv=w_ukv.astype(BF16),
        w_ukt=jnp.transpose(w_uk, (1, 2, 0)).astype(BF16),
        w_uvh=jnp.transpose(w_uv, (1, 0, 2)).astype(BF16),
        conv_w=conv_w, conv_b=row(conv_b), dt_bias=pad_row(dt_bias),
        a_neg=pad_row(-jnp.exp(a_log)), d_skip=row(jnp.repeat(d_skip, SSM_HEAD_DIM)),
        ssm_norm=row(ssm_norm),
        w_o_a=w_o[:D_ATTN].astype(BF16), w_o_s=w_o[D_ATTN:].astype(BF16),
        ln1_g=row(ln1_g), ln1_b=row(ln1_b),
        w_r=jnp.concatenate([w_r_hi, (w_r - w_r_hi.astype(F32)).astype(BF16)], axis=1),
        b_r=pad_row(jnp.concatenate([b_rg, b_re])),
        ln2_g=row(ln2_g), ln2_b=row(ln2_b),
    )


def _rope_tables(pos):
    half = QK_ROPE // 2
    inv_freq = ROPE_THETA ** (-jnp.arange(half, dtype=F32) / half)
    ang = pos.astype(F32)[:, None] * inv_freq[None, :]
    cos, sin = jnp.cos(ang), jnp.sin(ang)
    zeros = jnp.zeros((pos.shape[0], LANES - QK_ROPE), F32)
    return (jnp.concatenate([cos, cos, zeros], axis=1), jnp.concatenate([-sin, sin, zeros], axis=1))


def _dispatch_tables(route, counts, *, rb, n_blk):
    T = route.shape[0]
    experts = jnp.arange(N_EXPERTS, dtype=jnp.int32)
    e = route[:, 0:TOP_K].astype(jnp.int32)
    rank = route[:, 4:4 + TOP_K].astype(jnp.int32)
    counts = counts.astype(jnp.int32)
    pcounts = (counts + rb - 1) // rb * rb
    pend = jnp.cumsum(pcounts)
    pstart = pend - pcounts
    pos = jnp.sum(jnp.where(e[:, :, None] == experts, pstart, 0), axis=2) + rank
    blk_start = jnp.arange(n_blk, dtype=jnp.int32) * rb
    blk_e = jnp.minimum(jnp.sum((pend[None, :] <= blk_start[:, None]).astype(jnp.int32), axis=1),
                        N_EXPERTS - 1)
    n_used = (pend[-1] // rb).reshape(1)
    tok = jnp.broadcast_to(jnp.arange(T, dtype=jnp.int32)[:, None], (T, TOP_K))
    row_tok = jnp.zeros((n_blk * rb,), jnp.int32).at[pos.reshape(-1)].set(
        tok.reshape(-1), unique_indices=True, mode="drop")
    return row_tok, blk_e, n_used, pos


def _largest_tile(n, cap):
    t = cap
    while n % t:
        t //= 2
    return t


def kernel(x_prompt, x_sample, cache_kv_latent, cache_k_rope, state_conv, state_ssm, ln0_g, ln0_b, w_in, q_norm, w_uq, kv_norm, w_uk, w_uv, conv_w, conv_b, dt_bias, a_log, d_skip, ssm_norm, w_o, ln1_g, ln1_b, w_rg, b_rg, w_re, b_re, w_gate, w_up, w_down, ln2_g, ln2_b):
    B, L, _ = x_prompt.shape
    Bs, Ls, _ = x_sample.shape
    past = cache_kv_latent.shape[2]
    Tp, Ts = B * L, Bs * Ls
    pw = _prep_weights(ln0_g, ln0_b, w_in[0], q_norm[0], w_uq[0], kv_norm[0], w_uk[0], w_uv[0], conv_w[0],
                       conv_b[0], dt_bias[0], a_log[0], d_skip[0], ssm_norm[0], w_o[0], ln1_g[0], ln1_b[0],
                       w_rg[0], b_rg[0], w_re[0], b_re[0], ln2_g[0], ln2_b[0])

    tm_p = _largest_tile(L, 256)
    xp = x_prompt.reshape(Tp, D_MODEL)
    cos_p, sin_p = _rope_tables(jnp.arange(L, dtype=jnp.int32))
    q_p, kv_p, lat_p, kr_p, krp_p, z_p, xbc_p, dt_p = _inproj(
        xp, pw, cos_p, sin_p, tm=tm_p, pos_blocks=L // tm_p, with_kv=True)
    attn_p = _attn_prompt(q_p.reshape(B, L, -1), kv_p.reshape(B, L, -1), krp_p.reshape(B, L, -1),
                          tq=_largest_tile(L, 512), tk=_largest_tile(L, 1024))
    q_ssd = min(CHUNK, L)
    y_p, conv_p, ssm_p = _ssd(
        xbc_p.reshape(B, L, -1), dt_p.reshape(B, L, -1), z_p.reshape(B, L, -1),
        jnp.zeros((B, CONV_K - 1, CONV_CH), F32), jnp.zeros((B, D_SSM, D_STATE), F32), pw,
        q=q_ssd, tt=_largest_tile(L, 512))

    xs = x_sample.reshape(Ts, D_MODEL)
    cos_s, sin_s = _rope_tables(past + jnp.arange(Ls, dtype=jnp.int32))
    tm_s = _largest_tile(Ts, 256)
    assert tm_s % Ls == 0
    q_s, _, lat_s, kr_s, krp_s, z_s, xbc_s, dt_s = _inproj(
        xs, pw, jnp.tile(cos_s, (tm_s // Ls, 1)), jnp.tile(sin_s, (tm_s // Ls, 1)),
        tm=tm_s, pos_blocks=1, with_kv=False)
    attn_s = _attn_sample(q_s.reshape(Bs, Ls, -1), pw, cache_kv_latent[0], cache_k_rope[0],
                          lat_s.reshape(Bs, Ls, -1), krp_s.reshape(Bs, Ls, -1),
                          tk=_largest_tile(past, 1024))
    y_s, conv_s, ssm_s = _ssd(
        xbc_s.reshape(Bs, Ls, -1), dt_s.reshape(Bs, Ls, -1), z_s.reshape(Bs, Ls, -1),
        state_conv[0], state_ssm[0].reshape(Bs, D_SSM, D_STATE), pw, q=min(CHUNK, Ls), tt=Ls)

    T = Tp + Ts
    tm_c = _largest_tile(math.gcd(Tp, Ts), 256)
    h_all, hpk_all, route, counts = _outproj(xp, xs, attn_p.reshape(Tp, -1), attn_s.reshape(Ts, -1),
                                             y_p.reshape(Tp, -1), y_s.reshape(Ts, -1), pw, tm=tm_c)
    rb = 256
    n_blk = (T * TOP_K + N_EXPERTS * (rb - 1) + rb - 1) // rb
    row_tok, blk_e, n_used, pos = _dispatch_tables(route, counts[0, R_OFF:R_OFF + N_EXPERTS], rb=rb, n_blk=n_blk)
    y_rows = _experts(hpk_all, row_tok, blk_e, n_used, w_gate[0], w_up[0], w_down[0], rb=rb)
    pos_tiles = pos.reshape(T // tm_c, tm_c, TOP_K).transpose(0, 2, 1).reshape(-1)
    out_p, out_s = _combine(y_rows, pos_tiles, h_all, route, pw, tm=tm_c, t_prompt=Tp)

    return (out_p.reshape(B, L, D_MODEL), out_s.reshape(Bs, Ls, D_MODEL),
            lat_p.reshape(1, B, L, KV_LORA), lat_s.reshape(1, Bs, Ls, KV_LORA),
            kr_p.reshape(1, B, L, QK_ROPE), kr_s.reshape(1, Bs, Ls, QK_ROPE),
            conv_p[None], conv_s[None],
            ssm_p.reshape(1, B, SSM_HEADS, SSM_HEAD_DIM, D_STATE),
            ssm_s.reshape(1, Bs, SSM_HEADS, SSM_HEAD_DIM, D_STATE))
```

```python
import functools
import math

import jax
import jax.numpy as jnp
import numpy as np
from jax import lax
from jax.experimental import pallas as pl
from jax.experimental.pallas import tpu as pltpu

F32 = jnp.float32
BF16 = jnp.bfloat16

D_MODEL = 2048
CHUNK = 64
MLA_HEADS = 8
QK_NOPE = 128
QK_ROPE = 64
QK_DIM = QK_NOPE + QK_ROPE
V_DIM = 128
Q_LORA = 512
KV_LORA = 512
ROPE_THETA = 10000.0
ATTN_SCALE = QK_DIM ** -0.5
D_ATTN = MLA_HEADS * V_DIM
D_SSM = 1024
SSM_HEAD_DIM = 64
SSM_HEADS = D_SSM // SSM_HEAD_DIM
SSM_GROUPS = 2
HEADS_PER_GROUP = SSM_HEADS // SSM_GROUPS
D_STATE = 128
CONV_K = 4
CONV_CH = D_SSM + 2 * SSM_GROUPS * D_STATE
N_EXPERT_GROUPS = 8
EXPERTS_PER_GROUP = 8
N_EXPERTS = N_EXPERT_GROUPS * EXPERTS_PER_GROUP
TOP_K = 2
D_EXPERT = 512
DEPTH = 1
ALPHA = (2 * DEPTH) ** 0.25
RMS_EPS = 1e-6
LN_EPS = 1e-5

LANES = 128
MXU_DIM = 256
VMEM_LIMIT = 56 * 1024 * 1024

QK_PAD = 2 * LANES
SUM_ROWS = 16
C_Q, C_KV, C_Z, C_XBC, C_KR, C_DT = 0, 512, 1024, 2048, 3584, 3712
D_IN_PAD = 3840


def _const_spec(shape):
    nd = len(shape)
    return pl.BlockSpec(shape, lambda *_: (0,) * nd, pipeline_mode=pl.Buffered(1))


def _split3(a):
    hi = a.astype(BF16)
    r1 = a - hi.astype(F32)
    mid = r1.astype(BF16)
    lo = (r1 - mid.astype(F32)).astype(BF16)
    return hi, mid, lo


def _dot(a, b):
    return jnp.dot(a, b, preferred_element_type=F32)


def _dot_nt(a, b):
    return lax.dot_general(a, b, (((1,), (1,)), ((), ())), preferred_element_type=F32)


def _dot_tn(a, b):
    return lax.dot_general(a, b, (((0,), (0,)), ((), ())), preferred_element_type=F32)


def _exact_dot(a_f32, sel_bf16):
    hi, mid, lo = _split3(a_f32)
    return _dot(hi, sel_bf16) + _dot(mid, sel_bf16) + _dot(lo, sel_bf16)


def _exact_dot_l(sel_bf16, a_f32):
    hi, mid, lo = _split3(a_f32)
    return _dot(sel_bf16, hi) + _dot(sel_bf16, mid) + _dot(sel_bf16, lo)


def _layernorm(x, g, b):
    mu = jnp.mean(x, axis=-1, keepdims=True)
    xc = x - mu
    var = jnp.mean(xc * xc, axis=-1, keepdims=True)
    return xc * lax.rsqrt(var + LN_EPS) * g + b


def _rmsnorm(x, g):
    return x * lax.rsqrt(jnp.mean(x * x, axis=-1, keepdims=True) + RMS_EPS) * g


def _silu(x):
    return x * (1.0 / (1.0 + jnp.exp(-x)))


def _pack_bf16_pairs(x):
    n = x.shape[1] // 2
    bits = lax.bitcast_convert_type(x, jnp.uint32)
    return (bits[:, n:] & jnp.uint32(0xFFFF0000)) | (bits[:, :n] >> 16)


def _unpack_bf16_pairs(u):
    lo = lax.bitcast_convert_type(u << 16, F32)
    hi = lax.bitcast_convert_type(u & jnp.uint32(0xFFFF0000), F32)
    return jnp.concatenate([lo, hi], axis=1)


def _rope128(x, cos_t, sin_t):
    lane = lax.broadcasted_iota(jnp.int32, x.shape, 1)
    partner = jnp.where(lane < QK_ROPE // 2, pltpu.roll(x, LANES - QK_ROPE // 2, 1),
                        pltpu.roll(x, QK_ROPE // 2, 1))
    return x * cos_t + partner * sin_t


def _inproj_kernel(x_ref, g0_ref, b0_ref, w_in_ref, qn_ref, kvn_ref, w_uq_ref, w_uk_ref, w_uvt_ref,
                   cos_ref, sin_ref, dtb_ref,
                   q_ref, k_ref, vt_ref, lat_ref, kr_ref, krp_ref, z_ref, xbc_ref, dt_ref, *, with_kv):
    xn = _layernorm(x_ref[...], g0_ref[...], b0_ref[...]).astype(BF16)
    cos_t = cos_ref[...]
    sin_t = sin_ref[...]

    c_q = _dot(xn, w_in_ref[:, C_Q:C_Q + Q_LORA])
    qb = _rmsnorm(c_q, qn_ref[...]).astype(BF16)
    for h in range(MLA_HEADS):
        qh = _dot(qb, w_uq_ref[:, h * QK_PAD:(h + 1) * QK_PAD])
        q_ref[:, h * QK_PAD:h * QK_PAD + LANES] = qh[:, :LANES].astype(BF16)
        q_ref[:, h * QK_PAD + LANES:(h + 1) * QK_PAD] = _rope128(qh[:, LANES:], cos_t, sin_t).astype(BF16)

    c_kv = _dot(xn, w_in_ref[:, C_KV:C_KV + KV_LORA])
    lat = _rmsnorm(c_kv, kvn_ref[...])
    lat_ref[...] = lat
    if with_kv:
        lat_b = lat.astype(BF16)
        k_ref[...] = _dot(lat_b, w_uk_ref[...]).astype(BF16)
        vt_ref[...] = _dot_nt(w_uvt_ref[...], lat_b).astype(BF16)
    else:
        k_ref[...] = jnp.zeros(k_ref.shape, BF16)
        vt_ref[...] = jnp.zeros(vt_ref.shape, BF16)

    k_r = _rope128(_dot(xn, w_in_ref[:, C_KR:C_KR + LANES]), cos_t, sin_t)
    kr_ref[...] = k_r[:, :QK_ROPE]
    krp_ref[...] = k_r.astype(BF16)

    z_ref[...] = _dot(xn, w_in_ref[:, C_Z:C_Z + D_SSM]).astype(BF16)
    xbc_ref[...] = _dot(xn, w_in_ref[:, C_XBC:C_XBC + CONV_CH])

    dt_raw = _dot(xn, w_in_ref[:, C_DT:C_DT + LANES]) + dtb_ref[...]
    dt = jnp.maximum(dt_raw, 0.0) + jnp.log1p(jnp.exp(-jnp.abs(dt_raw)))
    lane = lax.broadcasted_iota(jnp.int32, dt.shape, 1)
    dt_ref[...] = jnp.where(lane < SSM_HEADS, dt, 0.0)


def _inproj(x, pw, cos_t, sin_t, *, tm, pos_blocks, with_kv):
    T = x.shape[0]
    n = T // tm
    row = lambda i: (i, 0)
    pos = lambda i: (i % pos_blocks, 0)
    k_cols, vt_rows = (D_ATTN, D_ATTN) if with_kv else (LANES, 8)
    out_shape = [
        jax.ShapeDtypeStruct((T, MLA_HEADS * QK_PAD), BF16),
        jax.ShapeDtypeStruct((T, k_cols), BF16),
        jax.ShapeDtypeStruct((n // pos_blocks, vt_rows, pos_blocks * tm), BF16),
        jax.ShapeDtypeStruct((T, KV_LORA), F32),
        jax.ShapeDtypeStruct((T, QK_ROPE), F32),
        jax.ShapeDtypeStruct((T, LANES), BF16),
        jax.ShapeDtypeStruct((T, D_SSM), BF16),
        jax.ShapeDtypeStruct((T, CONV_CH), F32),
        jax.ShapeDtypeStruct((T, LANES), F32),
    ]
    out_specs = [pl.BlockSpec((tm, s.shape[1]), row) if len(s.shape) == 2 else
                 pl.BlockSpec((None, vt_rows, tm), lambda i: (i // pos_blocks, 0, i % pos_blocks))
                 for s in out_shape]
    in_specs = [
        pl.BlockSpec((tm, D_MODEL), row),
        _const_spec((1, D_MODEL)), _const_spec((1, D_MODEL)),
        _const_spec((D_MODEL, D_IN_PAD)),
        _const_spec((1, Q_LORA)), _const_spec((1, KV_LORA)),
        _const_spec((Q_LORA, MLA_HEADS * QK_PAD)),
        _const_spec((KV_LORA, D_ATTN)), _const_spec((D_ATTN, KV_LORA)),
        pl.BlockSpec((tm, LANES), pos), pl.BlockSpec((tm, LANES), pos),
        _const_spec((1, LANES)),
    ]
    return pl.pallas_call(
        functools.partial(_inproj_kernel, with_kv=with_kv),
        grid=(n,), in_specs=in_specs, out_specs=out_specs, out_shape=out_shape,
        compiler_params=pltpu.CompilerParams(dimension_semantics=("arbitrary",),
                                             vmem_limit_bytes=VMEM_LIMIT),
        name="inproj",
    )(x, pw["ln0_g"], pw["ln0_b"], pw["w_in"], pw["q_norm"], pw["kv_norm"], pw["w_uq"],
      pw["w_uk2"], pw["w_uvt"], cos_t, sin_t, pw["dt_bias"])


def _attn_prompt_kernel(qi_ref, ki_ref, q_ref, k_ref, krp_ref, vt_ref, o_ref, *scratch, tq, tk):
    m_refs = scratch[0:MLA_HEADS]
    acc_refs = scratch[MLA_HEADS:2 * MLA_HEADS]
    p_id = pl.program_id(1)
    qi = qi_ref[p_id]
    ki = ki_ref[p_id]
    k_last = ((qi + 1) * tq - 1) // tk

    @pl.when(ki == 0)
    def _():
        for h in range(MLA_HEADS):
            m_refs[h][...] = jnp.full(m_refs[h].shape, -jnp.inf, F32)
            acc_refs[h][...] = jnp.zeros(acc_refs[h].shape, F32)

    def step(masked):
        krp = krp_ref[...]
        ones_rows = jnp.ones((SUM_ROWS, tk), BF16)
        if masked:
            kc = (ki * tk + lax.broadcasted_iota(jnp.int32, (tk, tq), 0)) // CHUNK
            qc = (qi * tq + lax.broadcasted_iota(jnp.int32, (tk, tq), 1)) // CHUNK
            allowed = kc <= qc
        for h in range(MLA_HEADS):
            qh = q_ref[:, h * QK_PAD:(h + 1) * QK_PAD]
            kh = jnp.concatenate([k_ref[:, h * QK_NOPE:(h + 1) * QK_NOPE], krp], axis=1)
            s = _dot_nt(kh, qh)
            if masked:
                s = jnp.where(allowed, s, -jnp.inf)
            m_old = m_refs[h][...]
            m_new = jnp.maximum(m_old, jnp.max(s, axis=0, keepdims=True))
            alpha = jnp.exp2(m_old - m_new)
            p = jnp.exp2(s - m_new).astype(BF16)
            vt_aug = jnp.concatenate([vt_ref[h * V_DIM:(h + 1) * V_DIM, :], ones_rows], axis=0)
            acc_refs[h][...] = alpha * acc_refs[h][...] + _dot(vt_aug, p)
            m_refs[h][...] = m_new

    @pl.when(ki < k_last)
    def _():
        step(False)

    @pl.when(ki == k_last)
    def _():
        step(True)
        for h in range(MLA_HEADS):
            acc = acc_refs[h][...]
            o_t = acc[:V_DIM, :] / acc[V_DIM:V_DIM + 1, :]
            o_ref[:, h * V_DIM:(h + 1) * V_DIM] = o_t.T.astype(BF16)


def _attn_prompt(q, k, krp, vt, *, tq, tk):
    B, L, _ = q.shape
    nq = L // tq
    pairs = [(i, j) for i in range(nq) for j in range(((i + 1) * tq - 1) // tk + 1)]
    qi_tab = jnp.asarray(np.array([p[0] for p in pairs], np.int32))
    ki_tab = jnp.asarray(np.array([p[1] for p in pairs], np.int32))
    grid_spec = pltpu.PrefetchScalarGridSpec(
        num_scalar_prefetch=2,
        grid=(B, len(pairs)),
        in_specs=[
            pl.BlockSpec((None, tq, MLA_HEADS * QK_PAD), lambda b, p, qi, ki: (b, qi[p], 0)),
            pl.BlockSpec((None, tk, D_ATTN), lambda b, p, qi, ki: (b, ki[p], 0)),
            pl.BlockSpec((None, tk, LANES), lambda b, p, qi, ki: (b, ki[p], 0)),
            pl.BlockSpec((None, D_ATTN, tk), lambda b, p, qi, ki: (b, 0, ki[p])),
        ],
        out_specs=pl.BlockSpec((None, tq, D_ATTN), lambda b, p, qi, ki: (b, qi[p], 0)),
        scratch_shapes=([pltpu.VMEM((1, tq), F32)] * MLA_HEADS
                        + [pltpu.VMEM((V_DIM + SUM_ROWS, tq), F32)] * MLA_HEADS),
    )
    return pl.pallas_call(
        functools.partial(_attn_prompt_kernel, tq=tq, tk=tk),
        grid_spec=grid_spec,
        out_shape=jax.ShapeDtypeStruct((B, L, D_ATTN), BF16),
        compiler_params=pltpu.CompilerParams(dimension_semantics=("arbitrary", "arbitrary"),
                                             vmem_limit_bytes=VMEM_LIMIT),
        name="attn_prompt",
    )(qi_tab, ki_tab, q, k, krp, vt)


def _attn_sample_kernel(q_ref, wukt_ref, wuv_ref, latp_ref, krpast_ref, latn_ref, krn_ref, o_ref,
                        qlat_ref, qrp_ref, m_ref, l_ref, acc_ref, *, ls, past, tk, n_new_pad):
    k = pl.program_id(1)
    nk = pl.num_programs(1)
    rows = MLA_HEADS * ls

    @pl.when(k == 0)
    def _():
        for h in range(MLA_HEADS):
            qn = q_ref[:, h * QK_PAD:h * QK_PAD + LANES]
            qlat_ref[h * ls:(h + 1) * ls, :] = _dot(qn, wukt_ref[h]).astype(BF16)
            qrp_ref[h * ls:(h + 1) * ls, :] = q_ref[:, h * QK_PAD + LANES:(h + 1) * QK_PAD]
        m_ref[...] = jnp.full(m_ref.shape, -jnp.inf, F32)
        l_ref[...] = jnp.zeros(l_ref.shape, F32)
        acc_ref[...] = jnp.zeros(acc_ref.shape, F32)

    q_chunk = (past + lax.broadcasted_iota(jnp.int32, (rows, 1), 0) % ls) // CHUNK

    def update(s, lat_b):
        m_old = m_ref[...]
        m_new = jnp.maximum(m_old, jnp.max(s, axis=1, keepdims=True))
        alpha = jnp.exp2(m_old - m_new)
        p = jnp.exp2(s - m_new)
        l_ref[...] = alpha * l_ref[...] + jnp.sum(p, axis=1, keepdims=True)
        acc_ref[...] = alpha * acc_ref[...] + _dot(p.astype(BF16), lat_b)
        m_ref[...] = m_new

    lat_b = latp_ref[...].astype(BF16)
    kr_b = krpast_ref[...].astype(BF16)
    s = _dot_nt(qlat_ref[...], lat_b) + _dot_nt(qrp_ref[:, :QK_ROPE], kr_b)
    k_pos = k * tk + lax.broadcasted_iota(jnp.int32, (rows, tk), 1)
    s = jnp.where(k_pos // CHUNK <= q_chunk, s, -jnp.inf)
    update(s, lat_b)

    @pl.when(k == nk - 1)
    def _():
        latn_b = latn_ref[...].astype(BF16)
        s2 = _dot_nt(qlat_ref[...], latn_b) + _dot_nt(qrp_ref[...], krn_ref[...])
        j = lax.broadcasted_iota(jnp.int32, (rows, n_new_pad), 1)
        ok = ((past + j) // CHUNK <= q_chunk) & (j < ls)
        update(jnp.where(ok, s2, -jnp.inf), latn_b)
        o_lat = (acc_ref[...] / l_ref[...]).astype(BF16)
        for h in range(MLA_HEADS):
            o_ref[:, h * V_DIM:(h + 1) * V_DIM] = _dot(o_lat[h * ls:(h + 1) * ls, :], wuv_ref[h]).astype(BF16)


def _attn_sample(q, pw, lat_past, kr_past, lat_new, krp_new, *, tk):
    Bs, Ls, _ = q.shape
    past = lat_past.shape[1]
    n_new_pad = LANES
    lat_new = jnp.pad(lat_new, ((0, 0), (0, n_new_pad - Ls), (0, 0)))
    krp_new = jnp.pad(krp_new, ((0, 0), (0, n_new_pad - Ls), (0, 0)))
    rows = MLA_HEADS * Ls
    return pl.pallas_call(
        functools.partial(_attn_sample_kernel, ls=Ls, past=past, tk=tk, n_new_pad=n_new_pad),
        grid=(Bs, past // tk),
        in_specs=[
            pl.BlockSpec((None, Ls, MLA_HEADS * QK_PAD), lambda b, k: (b, 0, 0)),
            _const_spec((MLA_HEADS, QK_NOPE, KV_LORA)),
            _const_spec((MLA_HEADS, KV_LORA, V_DIM)),
            pl.BlockSpec((None, tk, KV_LORA), lambda b, k: (b, k, 0)),
            pl.BlockSpec((None, tk, QK_ROPE), lambda b, k: (b, k, 0)),
            pl.BlockSpec((None, n_new_pad, KV_LORA), lambda b, k: (b, 0, 0)),
            pl.BlockSpec((None, n_new_pad, LANES), lambda b, k: (b, 0, 0)),
        ],
        out_specs=pl.BlockSpec((None, Ls, D_ATTN), lambda b, k: (b, 0, 0)),
        out_shape=jax.ShapeDtypeStruct((Bs, Ls, D_ATTN), BF16),
        scratch_shapes=[pltpu.VMEM((rows, KV_LORA), BF16), pltpu.VMEM((rows, LANES), BF16),
                        pltpu.VMEM((rows, 1), F32), pltpu.VMEM((rows, 1), F32),
                        pltpu.VMEM((rows, KV_LORA), F32)],
        compiler_params=pltpu.CompilerParams(dimension_semantics=("arbitrary", "arbitrary"),
                                             vmem_limit_bytes=VMEM_LIMIT),
        name="attn_sample",
    )(q, pw["w_ukt"], pw["w_uvh"], lat_past, kr_past, lat_new, krp_new)


def _ssd_consts(q):
    hb = MXU_DIM // q
    hq = SSM_HEADS * q
    lane = np.arange(hq)
    ltri = (np.arange(q)[None, :] <= np.arange(q)[:, None]).astype(np.float32)
    sel_c = (np.arange(LANES)[:, None] == (lane // q)[None, :]).astype(np.float32)
    sel_p = (np.arange(LANES)[:, None] == (np.arange(D_SSM) // SSM_HEAD_DIM)[None, :]).astype(np.float32)
    diag = (np.arange(q)[:, None] == (lane % q)[None, :]).astype(np.float32)
    causal = (np.arange(q)[:, None] >= (lane % q)[None, :]).astype(np.float32)
    bd_rows = np.arange(hb * q) // q
    bd_cols = np.arange(hb * SSM_HEAD_DIM) // SSM_HEAD_DIM
    bdmask = (bd_rows[:, None] == bd_cols[None, :]).astype(np.float32)
    return dict(ltri=jnp.asarray(ltri, BF16), sel_c=jnp.asarray(sel_c, BF16),
                sel_p=jnp.asarray(sel_p, BF16), diag=jnp.asarray(diag, F32),
                causal=jnp.asarray(causal, F32), bdmask=jnp.asarray(bdmask, BF16))


def _ssd_kernel(xbc_ref, dt_ref, z_ref, cprev_ref, h0_ref, cw_ref, cb_ref, a_ref, dskip_ref, norm_ref,
                ltri_ref, selc_ref, selp_ref, diag_ref, causal_ref, bdmask_ref,
                y_ref, cout_ref, hout_ref,
                ext_ref, u_ref, ht_ref, *, q, tt):
    t = pl.program_id(1)
    nt = pl.num_programs(1)
    hb = MXU_DIM // q
    nblk = SSM_HEADS // hb
    gw = HEADS_PER_GROUP * SSM_HEAD_DIM
    pad = 8

    @pl.when(t == 0)
    def _():
        ext_ref[0:pad, :] = jnp.zeros((pad, CONV_CH), F32)
        ext_ref[pad - (CONV_K - 1):pad, :] = cprev_ref[...]
        ht_ref[...] = h0_ref[...].T

    xbc = xbc_ref[...]
    ext_ref[pad:pad + tt, :] = xbc
    cout_ref[...] = xbc[tt - (CONV_K - 1):, :]
    conv = cb_ref[...] + cw_ref[0:1, :] * ext_ref[pad - 3:pad - 3 + tt, :]
    for k in range(1, CONV_K):
        conv = conv + cw_ref[k:k + 1, :] * ext_ref[pad - 3 + k:pad - 3 + k + tt, :]
    u_ref[...] = _silu(conv)
    ext_ref[pad - (CONV_K - 1):pad, :] = xbc[tt - (CONV_K - 1):, :]

    a_row = a_ref[...]
    d_row = dskip_ref[...]
    norm_row = norm_ref[...]

    def chunk(c, carry):
        r0 = pl.multiple_of(c * q, q)
        u = u_ref[pl.ds(r0, q), :]
        xs = u[:, :D_SSM]
        dt = dt_ref[pl.ds(r0, q), :]
        a_cs = _exact_dot_l(ltri_ref[...], dt * a_row)
        ap = _exact_dot(a_cs, selp_ref[...])
        dtp = _exact_dot(dt, selp_ref[...])
        a_i = ap if q == SSM_HEAD_DIM else _exact_dot(a_cs, selc_ref[...])
        a_j = jnp.sum(a_i * diag_ref[...], axis=0, keepdims=True)
        decay = jnp.exp(jnp.where(causal_ref[...] > 0.5, a_i - a_j, -jnp.inf))
        ap_last = ap[q - 1:q, :]
        xdt = (xs * dtp).astype(BF16)
        xw = (xs * (jnp.exp(ap_last - ap) * dtp)).astype(BF16)
        e_ap = jnp.exp(ap)

        cb_parts, y_off_parts = [], []
        for g in range(SSM_GROUPS):
            bm = u[:, D_SSM + g * D_STATE:D_SSM + (g + 1) * D_STATE].astype(BF16)
            cm = u[:, D_SSM + (SSM_GROUPS + g) * D_STATE:D_SSM + (SSM_GROUPS + g + 1) * D_STATE].astype(BF16)
            cb_parts.append(_dot_nt(cm, jnp.concatenate([bm] * HEADS_PER_GROUP, axis=0)))
            gl = slice(g * gw, (g + 1) * gw)
            ht_g = ht_ref[:, gl]
            y_off_parts.append(_dot(cm, ht_g.astype(BF16)) * e_ap[:, gl])
            st = _dot_tn(bm, xw[:, gl])
            ht_ref[:, gl] = ht_g * jnp.exp(ap_last[:, gl]) + st
        w_all = (jnp.concatenate(cb_parts, axis=1) * decay).astype(BF16)
        y_diag_parts = []
        for b in range(nblk):
            x_b = xdt[:, b * hb * SSM_HEAD_DIM:(b + 1) * hb * SSM_HEAD_DIM]
            bd = jnp.concatenate([x_b] * hb, axis=0) * bdmask_ref[...]
            y_diag_parts.append(_dot(w_all[:, b * MXU_DIM:(b + 1) * MXU_DIM], bd))
        y = jnp.concatenate(y_diag_parts, axis=1) + jnp.concatenate(y_off_parts, axis=1)

        y = y + d_row * xs
        y = y * _silu(z_ref[pl.ds(r0, q), :].astype(F32))
        outs = []
        for g in range(SSM_GROUPS):
            yg = y[:, g * gw:(g + 1) * gw]
            outs.append(yg * lax.rsqrt(jnp.mean(yg * yg, axis=1, keepdims=True) + RMS_EPS))
        y_ref[pl.ds(r0, q), :] = (jnp.concatenate(outs, axis=1) * norm_row).astype(BF16)
        return carry

    lax.fori_loop(0, tt // q, chunk, 0, unroll=True)

    @pl.when(t == nt - 1)
    def _():
        hout_ref[...] = ht_ref[...].T


def _ssd(xbc, dt, z, conv_prev, h0, pw, *, q, tt):
    B, L, _ = xbc.shape
    cs = _ssd_consts(q)
    hq = SSM_HEADS * q
    hb = MXU_DIM // q
    tile = lambda w: pl.BlockSpec((None, tt, w), lambda b, t: (b, t, 0))
    per_b = lambda s: pl.BlockSpec((None,) + s, lambda b, t: (b, 0, 0))
    return pl.pallas_call(
        functools.partial(_ssd_kernel, q=q, tt=tt),
        grid=(B, L // tt),
        in_specs=[tile(CONV_CH), tile(LANES), tile(D_SSM), per_b((CONV_K - 1, CONV_CH)),
                  per_b((D_SSM, D_STATE)),
                  _const_spec((CONV_K, CONV_CH)), _const_spec((1, CONV_CH)), _const_spec((1, LANES)),
                  _const_spec((1, D_SSM)), _const_spec((1, D_SSM)),
                  _const_spec((q, q)), _const_spec((LANES, hq)), _const_spec((LANES, D_SSM)),
                  _const_spec((q, hq)), _const_spec((q, hq)),
                  _const_spec((MXU_DIM, hb * SSM_HEAD_DIM))],
        out_specs=[tile(D_SSM), per_b((CONV_K - 1, CONV_CH)), per_b((D_SSM, D_STATE))],
        out_shape=[jax.ShapeDtypeStruct((B, L, D_SSM), BF16),
                   jax.ShapeDtypeStruct((B, CONV_K - 1, CONV_CH), F32),
                   jax.ShapeDtypeStruct((B, D_SSM, D_STATE), F32)],
        scratch_shapes=[pltpu.VMEM((8 + tt, CONV_CH), F32), pltpu.VMEM((tt, CONV_CH), F32),
                        pltpu.VMEM((D_STATE, D_SSM), F32)],
        compiler_params=pltpu.CompilerParams(dimension_semantics=("arbitrary", "arbitrary"),
                                             vmem_limit_bytes=VMEM_LIMIT),
        name="ssd",
    )(xbc, dt, z, conv_prev, h0, pw["conv_w"], pw["conv_b"], pw["a_neg"], pw["d_skip"], pw["ssm_norm"],
      cs["ltri"], cs["sel_c"], cs["sel_p"], cs["diag"], cs["causal"], cs["bdmask"])


R_OFF = N_EXPERT_GROUPS


def _outproj_kernel(xp_ref, xs_ref, attnp_ref, attns_ref, yp_ref, ys_ref, g0_ref, b0_ref, wo_a_ref, wo_s_ref,
                    g1_ref, b1_ref, wr_ref, br_ref, ltri_ref, h_ref, hpk_ref, route_ref, cnt_ref,
                    *, n_prompt_tiles, tm, n_groups):
    i = pl.program_id(0)
    is_p = i < n_prompt_tiles

    @pl.when(i == 0)
    def _():
        cnt_ref[...] = jnp.zeros(cnt_ref.shape, F32)

    cnt = cnt_ref[0:1, :]
    for g in range(n_groups):
        rows = slice(g * (tm // n_groups), (g + 1) * (tm // n_groups))
        refs = (xp_ref, xs_ref, attnp_ref, attns_ref, yp_ref, ys_ref)
        cnt = _outproj_rows(rows, is_p, cnt, refs, g0_ref, b0_ref, wo_a_ref, wo_s_ref, g1_ref, b1_ref,
                            wr_ref, br_ref, ltri_ref, h_ref, hpk_ref, route_ref)
    cnt_ref[...] = jnp.broadcast_to(cnt, cnt_ref.shape)


def _outproj_rows(rows, is_p, cnt, refs, g0_ref, b0_ref, wo_a_ref, wo_s_ref, g1_ref, b1_ref, wr_ref, br_ref,
                  ltri_ref, h_ref, hpk_ref, route_ref):
    xp_ref, xs_ref, attnp_ref, attns_ref, yp_ref, ys_ref = refs
    x = jnp.where(is_p, xp_ref[rows, :], xs_ref[rows, :])
    attn = jnp.where(is_p, attnp_ref[rows, :], attns_ref[rows, :])
    yssm = jnp.where(is_p, yp_ref[rows, :], ys_ref[rows, :])

    xn = _layernorm(x, g0_ref[...], b0_ref[...])
    mixed = _dot(attn, wo_a_ref[...]) + _dot(yssm, wo_s_ref[...])
    h = _layernorm(ALPHA * xn + mixed, g1_ref[...], b1_ref[...])
    h_ref[rows, :] = h

    h_hi = h.astype(BF16)
    h_hi32 = h_hi.astype(F32)
    hpk_ref[rows, :] = _pack_bf16_pairs(h_hi32)
    h_lo = (h - h_hi32).astype(BF16)
    tm = h.shape[0]
    prod = _dot(jnp.concatenate([h_hi, h_lo], axis=0), wr_ref[...])
    lg = (prod[:tm, :LANES] + prod[:tm, LANES:]) + (prod[tm:, :LANES] + prod[tm:, LANES:]) + br_ref[...]
    lane = lax.broadcasted_iota(jnp.int32, lg.shape, 1)
    big = jnp.int32(1 << 20)
    gl = jnp.where(lane < N_EXPERT_GROUPS, lg, -jnp.inf)
    gmax = jnp.max(gl, axis=1, keepdims=True)
    grp = jnp.min(jnp.where(gl == gmax, lane, big), axis=1, keepdims=True)
    g_w = 1.0 / jnp.sum(jnp.exp(gl - gmax), axis=1, keepdims=True)
    in_grp = (lane >= R_OFF) & (lane < R_OFF + N_EXPERTS) & ((lane - R_OFF) // EXPERTS_PER_GROUP == grp)
    el = jnp.where(in_grp, lg, -jnp.inf)
    emax = jnp.max(el, axis=1, keepdims=True)
    ee = jnp.exp(el - emax)
    prob = jnp.where(in_grp, ee / jnp.sum(ee, axis=1, keepdims=True), -1.0)
    p1 = jnp.max(prob, axis=1, keepdims=True)
    i1 = jnp.min(jnp.where(prob == p1, lane, big), axis=1, keepdims=True)
    prob2 = jnp.where(lane == i1, -1.0, prob)
    p2 = jnp.max(prob2, axis=1, keepdims=True)
    i2 = jnp.min(jnp.where(prob2 == p2, lane, big), axis=1, keepdims=True)
    denom = p1 + p2
    oh1 = (lane == i1).astype(F32)
    oh2 = (lane == i2).astype(F32)
    oh = oh1 + oh2
    before = _dot(ltri_ref[...], oh.astype(BF16)) + cnt
    rank1 = jnp.sum(before * oh1, axis=1, keepdims=True)
    rank2 = jnp.sum(before * oh2, axis=1, keepdims=True)
    route = jnp.where(lane == 0, (i1 - R_OFF).astype(F32),
                      jnp.where(lane == 1, (i2 - R_OFF).astype(F32),
                                jnp.where(lane == 2, g_w * p1 / denom,
                                          jnp.where(lane == 3, g_w * p2 / denom,
                                                    jnp.where(lane == 4, rank1,
                                                              jnp.where(lane == 5, rank2, 0.0))))))
    route_ref[rows, :] = route
    return cnt + jnp.sum(oh, axis=0, keepdims=True)


def _outproj(xp, xs, attn_p, attn_s, y_p, y_s, pw, *, tm):
    Tp, Ts = xp.shape[0], xs.shape[0]
    npt, nst = Tp // tm, Ts // tm
    T = Tp + Ts
    row = lambda i: (i, 0)
    prow = lambda i: (jnp.minimum(i, npt - 1), 0)
    srow = lambda i: (jnp.maximum(i - npt, 0), 0)
    n_groups = 1
    gm = tm // n_groups
    ltri = jnp.asarray(np.tril(np.ones((gm, gm), np.float32), -1), BF16)
    return pl.pallas_call(
        functools.partial(_outproj_kernel, n_prompt_tiles=npt, tm=tm, n_groups=n_groups),
        grid=(npt + nst,),
        in_specs=[pl.BlockSpec((tm, D_MODEL), prow), pl.BlockSpec((tm, D_MODEL), srow),
                  pl.BlockSpec((tm, D_ATTN), prow), pl.BlockSpec((tm, D_ATTN), srow),
                  pl.BlockSpec((tm, D_SSM), prow), pl.BlockSpec((tm, D_SSM), srow),
                  _const_spec((1, D_MODEL)), _const_spec((1, D_MODEL)),
                  _const_spec((D_ATTN, D_MODEL)), _const_spec((D_SSM, D_MODEL)),
                  _const_spec((1, D_MODEL)), _const_spec((1, D_MODEL)),
                  _const_spec((D_MODEL, 2 * LANES)), _const_spec((1, LANES)),
                  _const_spec((gm, gm))],
        out_specs=[pl.BlockSpec((tm, D_MODEL), row), pl.BlockSpec((tm, D_MODEL // 2), row),
                   pl.BlockSpec((tm, LANES), row), pl.BlockSpec((8, LANES), lambda i: (0, 0))],
        out_shape=[jax.ShapeDtypeStruct((T, D_MODEL), F32), jax.ShapeDtypeStruct((T, D_MODEL // 2), jnp.uint32),
                   jax.ShapeDtypeStruct((T, LANES), F32), jax.ShapeDtypeStruct((8, LANES), F32)],
        compiler_params=pltpu.CompilerParams(dimension_semantics=("arbitrary",),
                                             vmem_limit_bytes=VMEM_LIMIT),
        name="outproj",
    )(xp, xs, attn_p, attn_s, y_p, y_s, pw["ln0_g"], pw["ln0_b"], pw["w_o_a"], pw["w_o_s"],
      pw["ln1_g"], pw["ln1_b"], pw["w_r"], pw["b_r"], ltri)


def _experts_kernel(blk_e_ref, n_used_ref, tok_ref, tok_next_ref, h_hbm, wg_ref, wu_ref, wd_ref,
                    y_ref, xbuf, sem, wg_b, wu_b, wd_b, prev_e, *, rb):
    i = pl.program_id(0)
    n_used = n_used_ref[0]
    slot = i % 2

    def row_copy(tok, dst_slot, r):
        return pltpu.make_async_copy(h_hbm.at[pl.ds(tok, 1)], xbuf.at[dst_slot, pl.ds(r, 1)],
                                     sem.at[dst_slot])

    def wait_block(s):
        pltpu.make_async_copy(h_hbm.at[pl.ds(0, rb)], xbuf.at[s], sem.at[s]).wait()

    @pl.when(i == 0)
    def _():
        prev_e[0] = -1

        def body(r, c):
            row_copy(tok_ref[r], 0, r).start()
            return c
        lax.fori_loop(0, rb, body, 0)

    @pl.when(i < n_used)
    def _():
        wait_block(slot)
        e = blk_e_ref[i]

        @pl.when(e != prev_e[0])
        def _():
            wg_b[...] = wg_ref[...].astype(BF16)
            wu_b[...] = wu_ref[...].astype(BF16)
            wd_b[...] = wd_ref[...].astype(BF16)
            prev_e[0] = e

        xb = _unpack_bf16_pairs(xbuf[slot]).astype(BF16)
        for r in range(rb):
            row_copy(tok_next_ref[r], 1 - slot, r).start(priority=r % 2)
        hid = (_silu(_dot(xb, wg_b[...])) * _dot(xb, wu_b[...])).astype(BF16)
        y = _dot(hid, wd_b[...])
        y_ref[...] = _pack_bf16_pairs(y.astype(BF16).astype(F32))

        @pl.when(i == n_used - 1)
        def _():
            wait_block(1 - slot)

    @pl.when(i >= n_used)
    def _():
        y_ref[...] = jnp.zeros(y_ref.shape, jnp.uint32)


def _experts(hpk_all, row_tok, blk_e, n_used, w_gate, w_up, w_down, *, rb):
    n_blk = blk_e.shape[0]
    n_rows = n_blk * rb
    grid_spec = pltpu.PrefetchScalarGridSpec(
        num_scalar_prefetch=2,
        grid=(n_blk,),
        in_specs=[
            pl.BlockSpec((rb,), lambda i, be, nu: (jnp.minimum(i, nu[0] - 1),), memory_space=pltpu.SMEM),
            pl.BlockSpec((rb,), lambda i, be, nu: (jnp.minimum(i + 1, nu[0] - 1),), memory_space=pltpu.SMEM),
            pl.BlockSpec(memory_space=pl.ANY),
            pl.BlockSpec((None, D_MODEL, D_EXPERT), lambda i, be, nu: (be[jnp.minimum(i, nu[0] - 1)], 0, 0)),
            pl.BlockSpec((None, D_MODEL, D_EXPERT), lambda i, be, nu: (be[jnp.minimum(i, nu[0] - 1)], 0, 0)),
            pl.BlockSpec((None, D_EXPERT, D_MODEL), lambda i, be, nu: (be[jnp.minimum(i, nu[0] - 1)], 0, 0)),
        ],
        out_specs=pl.BlockSpec((rb, D_MODEL // 2), lambda i, be, nu: (i, 0)),
        scratch_shapes=[pltpu.VMEM((2, rb, D_MODEL // 2), jnp.uint32), pltpu.SemaphoreType.DMA((2,)),
                        pltpu.VMEM((D_MODEL, D_EXPERT), BF16), pltpu.VMEM((D_MODEL, D_EXPERT), BF16),
                        pltpu.VMEM((D_EXPERT, D_MODEL), BF16), pltpu.SMEM((1,), jnp.int32)],
    )
    return pl.pallas_call(
        functools.partial(_experts_kernel, rb=rb),
        grid_spec=grid_spec,
        out_shape=jax.ShapeDtypeStruct((n_rows, D_MODEL // 2), jnp.uint32),
        compiler_params=pltpu.CompilerParams(dimension_semantics=("arbitrary",),
                                             vmem_limit_bytes=VMEM_LIMIT),
        name="experts",
    )(blk_e, n_used, row_tok, row_tok, hpk_all, w_gate, w_up, w_down)


def _combine_kernel(pos_ref, pos_next_ref, y_hbm, h_ref, route_ref, g2_ref, b2_ref, op_ref, os_ref, ybuf, sem,
                    *, tm, n_prompt_tiles):
    i = pl.program_id(0)
    n = pl.num_programs(0)
    slot = i % 2
    rows = TOP_K * tm

    def row_copy(idx, dst_slot, r):
        return pltpu.make_async_copy(y_hbm.at[pl.ds(idx, 1)], ybuf.at[dst_slot, pl.ds(r, 1)],
                                     sem.at[dst_slot])

    def wait_tile(s):
        pltpu.make_async_copy(y_hbm.at[pl.ds(0, rows)], ybuf.at[s], sem.at[s]).wait()

    @pl.when(i == 0)
    def _():
        def body(r, c):
            row_copy(pos_ref[r], 0, r).start()
            return c
        lax.fori_loop(0, rows, body, 0)

    wait_tile(slot)
    gate1 = route_ref[:, 2:3]
    gate2 = route_ref[:, 3:4]
    f = (_unpack_bf16_pairs(ybuf[slot, 0:tm, :]) * gate1
         + _unpack_bf16_pairs(ybuf[slot, tm:rows, :]) * gate2)
    for r in range(rows):
        row_copy(pos_next_ref[r], 1 - slot, r).start(priority=r % 2)
    out = _layernorm(ALPHA * h_ref[...] + f, g2_ref[...], b2_ref[...])

    @pl.when(i < n_prompt_tiles)
    def _():
        op_ref[...] = out

    @pl.when(i >= n_prompt_tiles)
    def _():
        os_ref[...] = out

    @pl.when(i == n - 1)
    def _():
        wait_tile(1 - slot)


def _combine(y_rows, pos_tiles, h_all, route, pw, *, tm, t_prompt):
    T = h_all.shape[0]
    n = T // tm
    npt = t_prompt // tm
    rows = TOP_K * tm
    return pl.pallas_call(
        functools.partial(_combine_kernel, tm=tm, n_prompt_tiles=npt),
        grid=(n,),
        in_specs=[
            pl.BlockSpec((rows,), lambda i: (i,), memory_space=pltpu.SMEM),
            pl.BlockSpec((rows,), lambda i: (jnp.minimum(i + 1, n - 1),), memory_space=pltpu.SMEM),
            pl.BlockSpec(memory_space=pl.ANY),
            pl.BlockSpec((tm, D_MODEL), lambda i: (i, 0)),
            pl.BlockSpec((tm, LANES), lambda i: (i, 0)),
            _const_spec((1, D_MODEL)), _const_spec((1, D_MODEL)),
        ],
        out_specs=[pl.BlockSpec((tm, D_MODEL), lambda i: (jnp.minimum(i, npt - 1), 0)),
                   pl.BlockSpec((tm, D_MODEL), lambda i: (jnp.maximum(i - npt, 0), 0))],
        out_shape=[jax.ShapeDtypeStruct((t_prompt, D_MODEL), F32),
                   jax.ShapeDtypeStruct((T - t_prompt, D_MODEL), F32)],
        scratch_shapes=[pltpu.VMEM((2, rows, D_MODEL // 2), jnp.uint32), pltpu.SemaphoreType.DMA((2,))],
        compiler_params=pltpu.CompilerParams(dimension_semantics=("arbitrary",),
                                             vmem_limit_bytes=VMEM_LIMIT),
        name="combine",
    )(pos_tiles, pos_tiles, y_rows, h_all, route, pw["ln2_g"], pw["ln2_b"])


def _prep_weights(ln0_g, ln0_b, w_in, q_norm, w_uq, kv_norm, w_uk, w_uv, conv_w, conv_b, dt_bias, a_log,
                  d_skip, ssm_norm, w_o, ln1_g, ln1_b, w_rg, b_rg, w_re, b_re, ln2_g, ln2_b):
    s_q, s_kv, s_kr, s_z, s_xbc = Q_LORA, Q_LORA + KV_LORA, Q_LORA + KV_LORA + QK_ROPE, \
        Q_LORA + KV_LORA + QK_ROPE + D_SSM, Q_LORA + KV_LORA + QK_ROPE + D_SSM + CONV_CH
    zc = lambda n: jnp.zeros((D_MODEL, n), F32)
    w_in_p = jnp.concatenate([
        w_in[:, :s_q], w_in[:, s_q:s_kv], w_in[:, s_kr:s_z], w_in[:, s_z:s_xbc],
        w_in[:, s_kv:s_kr], zc(LANES - QK_ROPE), w_in[:, s_xbc:], zc(LANES - SSM_HEADS)], axis=1)
    wq = w_uq.reshape(Q_LORA, MLA_HEADS, QK_DIM) * (ATTN_SCALE * math.log2(math.e))
    wq = jnp.concatenate([wq, jnp.zeros((Q_LORA, MLA_HEADS, QK_PAD - QK_DIM), F32)], axis=2)
    w_r = jnp.concatenate([w_rg, w_re, jnp.zeros((D_MODEL, LANES - R_OFF - N_EXPERTS), F32)], axis=1)
    w_r_hi = w_r.astype(BF16)
    row = lambda v: v.reshape(1, -1)
    pad_row = lambda v: jnp.pad(v, (0, LANES - v.shape[0])).reshape(1, LANES)
    return dict(
        ln0_g=row(ln0_g), ln0_b=row(ln0_b), w_in=w_in_p.astype(BF16),
        q_norm=row(q_norm), kv_norm=row(kv_norm),
        w_uq=wq.reshape(Q_LORA, MLA_HEADS * QK_PAD).astype(BF16),
        w_uk2=w_uk.reshape(KV_LORA, D_ATTN).astype(BF16),
        w_uvt=w_uv.reshape(KV_LORA, D_ATTN).T.astype(BF16),
        w_ukt=jnp.transpose(w_uk, (1, 2, 0)).astype(BF16),
        w_uvh=jnp.transpose(w_uv, (1, 0, 2)).astype(BF16),
        conv_w=conv_w, conv_b=row(conv_b), dt_bias=pad_row(dt_bias),
        a_neg=pad_row(-jnp.exp(a_log)), d_skip=row(jnp.repeat(d_skip, SSM_HEAD_DIM)),
        ssm_norm=row(ssm_norm),
        w_o_a=w_o[:D_ATTN].astype(BF16), w_o_s=w_o[D_ATTN:].astype(BF16),
        ln1_g=row(ln1_g), ln1_b=row(ln1_b),
        w_r=jnp.concatenate([w_r_hi, (w_r - w_r_hi.astype(F32)).astype(BF16)], axis=1),
        b_r=pad_row(jnp.concatenate([b_rg, b_re])),
        ln2_g=row(ln2_g), ln2_b=row(ln2_b),
    )


def _rope_tables(pos):
    half = QK_ROPE // 2
    inv_freq = ROPE_THETA ** (-jnp.arange(half, dtype=F32) / half)
    ang = pos.astype(F32)[:, None] * inv_freq[None, :]
    cos, sin = jnp.cos(ang), jnp.sin(ang)
    zeros = jnp.zeros((pos.shape[0], LANES - QK_ROPE), F32)
    return (jnp.concatenate([cos, cos, zeros], axis=1), jnp.concatenate([-sin, sin, zeros], axis=1))


def _dispatch_tables(route, counts, *, rb, n_blk):
    T = route.shape[0]
    experts = jnp.arange(N_EXPERTS, dtype=jnp.int32)
    e = route[:, 0:TOP_K].astype(jnp.int32)
    rank = route[:, 4:4 + TOP_K].astype(jnp.int32)
    counts = counts.astype(jnp.int32)
    pcounts = (counts + rb - 1) // rb * rb
    pend = jnp.cumsum(pcounts)
    pstart = pend - pcounts
    pos = jnp.sum(jnp.where(e[:, :, None] == experts, pstart, 0), axis=2) + rank
    blk_start = jnp.arange(n_blk, dtype=jnp.int32) * rb
    blk_e = jnp.minimum(jnp.sum((pend[None, :] <= blk_start[:, None]).astype(jnp.int32), axis=1),
                        N_EXPERTS - 1)
    n_used = (pend[-1] // rb).reshape(1)
    tok = jnp.broadcast_to(jnp.arange(T, dtype=jnp.int32)[:, None], (T, TOP_K))
    row_tok = jnp.zeros((n_blk * rb,), jnp.int32).at[pos.reshape(-1)].set(
        tok.reshape(-1), unique_indices=True, mode="drop")
    return row_tok, blk_e, n_used, pos


def _largest_tile(n, cap):
    t = cap
    while n % t:
        t //= 2
    return t


def kernel(x_prompt, x_sample, cache_kv_latent, cache_k_rope, state_conv, state_ssm, ln0_g, ln0_b, w_in, q_norm, w_uq, kv_norm, w_uk, w_uv, conv_w, conv_b, dt_bias, a_log, d_skip, ssm_norm, w_o, ln1_g, ln1_b, w_rg, b_rg, w_re, b_re, w_gate, w_up, w_down, ln2_g, ln2_b):
    B, L, _ = x_prompt.shape
    Bs, Ls, _ = x_sample.shape
    past = cache_kv_latent.shape[2]
    Tp, Ts = B * L, Bs * Ls
    pw = _prep_weights(ln0_g, ln0_b, w_in[0], q_norm[0], w_uq[0], kv_norm[0], w_uk[0], w_uv[0], conv_w[0],
                       conv_b[0], dt_bias[0], a_log[0], d_skip[0], ssm_norm[0], w_o[0], ln1_g[0], ln1_b[0],
                       w_rg[0], b_rg[0], w_re[0], b_re[0], ln2_g[0], ln2_b[0])

    tm_p = _largest_tile(L, 512)
    xp = x_prompt.reshape(Tp, D_MODEL)
    cos_p, sin_p = _rope_tables(jnp.arange(L, dtype=jnp.int32))
    q_p, k_p, vt_p, lat_p, kr_p, krp_p, z_p, xbc_p, dt_p = _inproj(
        xp, pw, cos_p, sin_p, tm=tm_p, pos_blocks=L // tm_p, with_kv=True)
    attn_p = _attn_prompt(q_p.reshape(B, L, -1), k_p.reshape(B, L, -1), krp_p.reshape(B, L, -1), vt_p,
                          tq=_largest_tile(L, 512), tk=_largest_tile(L, 1024))
    q_ssd = min(CHUNK, L)
    y_p, conv_p, ssm_p = _ssd(
        xbc_p.reshape(B, L, -1), dt_p.reshape(B, L, -1), z_p.reshape(B, L, -1),
        jnp.zeros((B, CONV_K - 1, CONV_CH), F32), jnp.zeros((B, D_SSM, D_STATE), F32), pw,
        q=q_ssd, tt=_largest_tile(L, 512))

    xs = x_sample.reshape(Ts, D_MODEL)
    cos_s, sin_s = _rope_tables(past + jnp.arange(Ls, dtype=jnp.int32))
    tm_s = _largest_tile(Ts, 256)
    assert tm_s % Ls == 0
    q_s, _, _, lat_s, kr_s, krp_s, z_s, xbc_s, dt_s = _inproj(
        xs, pw, jnp.tile(cos_s, (tm_s // Ls, 1)), jnp.tile(sin_s, (tm_s // Ls, 1)),
        tm=tm_s, pos_blocks=1, with_kv=False)
    attn_s = _attn_sample(q_s.reshape(Bs, Ls, -1), pw, cache_kv_latent[0], cache_k_rope[0],
                          lat_s.reshape(Bs, Ls, -1), krp_s.reshape(Bs, Ls, -1),
                          tk=_largest_tile(past, 1024))
    y_s, conv_s, ssm_s = _ssd(
        xbc_s.reshape(Bs, Ls, -1), dt_s.reshape(Bs, Ls, -1), z_s.reshape(Bs, Ls, -1),
        state_conv[0], state_ssm[0].reshape(Bs, D_SSM, D_STATE), pw, q=min(CHUNK, Ls), tt=Ls)

    T = Tp + Ts
    tm_c = _largest_tile(math.gcd(Tp, Ts), 256)
    h_all, hpk_all, route, counts = _outproj(xp, xs, attn_p.reshape(Tp, -1), attn_s.reshape(Ts, -1),
                                             y_p.reshape(Tp, -1), y_s.reshape(Ts, -1), pw, tm=tm_c)
    rb = 256
    n_blk = (T * TOP_K + N_EXPERTS * (rb - 1) + rb - 1) // rb
    row_tok, blk_e, n_used, pos = _dispatch_tables(route, counts[0, R_OFF:R_OFF + N_EXPERTS], rb=rb, n_blk=n_blk)
    y_rows = _experts(hpk_all, row_tok, blk_e, n_used, w_gate[0], w_up[0], w_down[0], rb=rb)
    pos_tiles = pos.reshape(T // tm_c, tm_c, TOP_K).transpose(0, 2, 1).reshape(-1)
    out_p, out_s = _combine(y_rows, pos_tiles, h_all, route, pw, tm=tm_c, t_prompt=Tp)

    return (out_p.reshape(B, L, D_MODEL), out_s.reshape(Bs, Ls, D_MODEL),
            lat_p.reshape(1, B, L, KV_LORA), lat_s.reshape(1, Bs, Ls, KV_LORA),
            kr_p.reshape(1, B, L, QK_ROPE), kr_s.reshape(1, Bs, Ls, QK_ROPE),
            conv_p[None], conv_s[None],
            ssm_p.reshape(1, B, SSM_HEADS, SSM_HEAD_DIM, D_STATE),
            ssm_s.reshape(1, Bs, SSM_HEADS, SSM_HEAD_DIM, D_STATE))
```

```python
import functools
import math

import jax
import jax.numpy as jnp
import numpy as np
from jax import lax
from jax.experimental import pallas as pl
from jax.experimental.pallas import tpu as pltpu

F32 = jnp.float32
BF16 = jnp.bfloat16

D_MODEL = 2048
CHUNK = 64
MLA_HEADS = 8
QK_NOPE = 128
QK_ROPE = 64
QK_DIM = QK_NOPE + QK_ROPE
V_DIM = 128
Q_LORA = 512
KV_LORA = 512
ROPE_THETA = 10000.0
ATTN_SCALE = QK_DIM ** -0.5
D_ATTN = MLA_HEADS * V_DIM
D_SSM = 1024
SSM_HEAD_DIM = 64
SSM_HEADS = D_SSM // SSM_HEAD_DIM
SSM_GROUPS = 2
HEADS_PER_GROUP = SSM_HEADS // SSM_GROUPS
D_STATE = 128
CONV_K = 4
CONV_CH = D_SSM + 2 * SSM_GROUPS * D_STATE
N_EXPERT_GROUPS = 8
EXPERTS_PER_GROUP = 8
N_EXPERTS = N_EXPERT_GROUPS * EXPERTS_PER_GROUP
TOP_K = 2
D_EXPERT = 512
DEPTH = 1
ALPHA = (2 * DEPTH) ** 0.25
RMS_EPS = 1e-6
LN_EPS = 1e-5

LANES = 128
MXU_DIM = 256
VMEM_LIMIT = 56 * 1024 * 1024

QK_PAD = 2 * LANES
GATHER_AHEAD = 2
GATHER_SLOTS = GATHER_AHEAD + 1
C_Q, C_KV, C_Z, C_XBC, C_KR, C_DT = 0, 512, 1024, 2048, 3584, 3712
D_IN_PAD = 3840


def _const_spec(shape):
    nd = len(shape)
    return pl.BlockSpec(shape, lambda *_: (0,) * nd, pipeline_mode=pl.Buffered(1))


def _split3(a):
    hi = a.astype(BF16)
    r1 = a - hi.astype(F32)
    mid = r1.astype(BF16)
    lo = (r1 - mid.astype(F32)).astype(BF16)
    return hi, mid, lo


def _dot(a, b):
    return jnp.dot(a, b, preferred_element_type=F32)


def _dot_nt(a, b):
    return lax.dot_general(a, b, (((1,), (1,)), ((), ())), preferred_element_type=F32)


def _dot_tn(a, b):
    return lax.dot_general(a, b, (((0,), (0,)), ((), ())), preferred_element_type=F32)


def _exact_dot(a_f32, sel_bf16):
    hi, mid, lo = _split3(a_f32)
    return _dot(hi, sel_bf16) + _dot(mid, sel_bf16) + _dot(lo, sel_bf16)


def _exact_dot_l(sel_bf16, a_f32):
    hi, mid, lo = _split3(a_f32)
    return _dot(sel_bf16, hi) + _dot(sel_bf16, mid) + _dot(sel_bf16, lo)


def _layernorm(x, g, b):
    mu = jnp.mean(x, axis=-1, keepdims=True)
    xc = x - mu
    var = jnp.mean(xc * xc, axis=-1, keepdims=True)
    return xc * lax.rsqrt(var + LN_EPS) * g + b


def _rmsnorm(x, g):
    return x * lax.rsqrt(jnp.mean(x * x, axis=-1, keepdims=True) + RMS_EPS) * g


def _silu(x):
    return x * (1.0 / (1.0 + jnp.exp(-x)))


def _pack_bf16_pairs(x):
    n = x.shape[1] // 2
    bits = lax.bitcast_convert_type(x, jnp.uint32)
    return (bits[:, n:] & jnp.uint32(0xFFFF0000)) | (bits[:, :n] >> 16)


def _unpack_bf16_pairs(u):
    lo = lax.bitcast_convert_type(u << 16, F32)
    hi = lax.bitcast_convert_type(u & jnp.uint32(0xFFFF0000), F32)
    return jnp.concatenate([lo, hi], axis=1)


def _rope128(x, cos_t, sin_t):
    lane = lax.broadcasted_iota(jnp.int32, x.shape, 1)
    partner = jnp.where(lane < QK_ROPE // 2, pltpu.roll(x, LANES - QK_ROPE // 2, 1),
                        pltpu.roll(x, QK_ROPE // 2, 1))
    return x * cos_t + partner * sin_t


def _inproj_kernel(x_ref, g0_ref, b0_ref, w_in_ref, qn_ref, kvn_ref, w_uq_ref, w_ukv_ref,
                   cos_ref, sin_ref, dtb_ref,
                   q_ref, kv_ref, lat_ref, kr_ref, krp_ref, z_ref, xbc_ref, dt_ref, *, with_kv):
    xn = _layernorm(x_ref[...], g0_ref[...], b0_ref[...]).astype(BF16)
    cos_t = cos_ref[...]
    sin_t = sin_ref[...]

    c_q = _dot(xn, w_in_ref[:, C_Q:C_Q + Q_LORA])
    qb = _rmsnorm(c_q, qn_ref[...]).astype(BF16)
    for h in range(MLA_HEADS):
        qh = _dot(qb, w_uq_ref[:, h * QK_PAD:(h + 1) * QK_PAD])
        q_ref[:, h * QK_PAD:h * QK_PAD + LANES] = qh[:, :LANES].astype(BF16)
        q_ref[:, h * QK_PAD + LANES:(h + 1) * QK_PAD] = _rope128(qh[:, LANES:], cos_t, sin_t).astype(BF16)

    c_kv = _dot(xn, w_in_ref[:, C_KV:C_KV + KV_LORA])
    lat = _rmsnorm(c_kv, kvn_ref[...])
    lat_ref[...] = lat
    if with_kv:
        kv_ref[...] = _dot(lat.astype(BF16), w_ukv_ref[...]).astype(BF16)
    else:
        kv_ref[...] = jnp.zeros(kv_ref.shape, BF16)

    k_r = _rope128(_dot(xn, w_in_ref[:, C_KR:C_KR + LANES]), cos_t, sin_t)
    kr_ref[...] = k_r[:, :QK_ROPE]
    krp_ref[...] = k_r.astype(BF16)

    z_ref[...] = _dot(xn, w_in_ref[:, C_Z:C_Z + D_SSM]).astype(BF16)
    xbc_ref[...] = _dot(xn, w_in_ref[:, C_XBC:C_XBC + CONV_CH])

    dt_raw = _dot(xn, w_in_ref[:, C_DT:C_DT + LANES]) + dtb_ref[...]
    dt = jnp.maximum(dt_raw, 0.0) + jnp.log1p(jnp.exp(-jnp.abs(dt_raw)))
    lane = lax.broadcasted_iota(jnp.int32, dt.shape, 1)
    dt_ref[...] = jnp.where(lane < SSM_HEADS, dt, 0.0)


def _inproj(x, pw, cos_t, sin_t, *, tm, pos_blocks, with_kv):
    T = x.shape[0]
    n = T // tm
    row = lambda i: (i, 0)
    pos = lambda i: (i % pos_blocks, 0)
    kv_cols = 2 * D_ATTN if with_kv else LANES
    out_shape = [
        jax.ShapeDtypeStruct((T, MLA_HEADS * QK_PAD), BF16),
        jax.ShapeDtypeStruct((T, kv_cols), BF16),
        jax.ShapeDtypeStruct((T, KV_LORA), F32),
        jax.ShapeDtypeStruct((T, QK_ROPE), F32),
        jax.ShapeDtypeStruct((T, LANES), BF16),
        jax.ShapeDtypeStruct((T, D_SSM), BF16),
        jax.ShapeDtypeStruct((T, CONV_CH), F32),
        jax.ShapeDtypeStruct((T, LANES), F32),
    ]
    out_specs = [pl.BlockSpec((tm, s.shape[1]), row) for s in out_shape]
    in_specs = [
        pl.BlockSpec((tm, D_MODEL), row),
        _const_spec((1, D_MODEL)), _const_spec((1, D_MODEL)),
        _const_spec((D_MODEL, D_IN_PAD)),
        _const_spec((1, Q_LORA)), _const_spec((1, KV_LORA)),
        _const_spec((Q_LORA, MLA_HEADS * QK_PAD)),
        _const_spec((KV_LORA, 2 * D_ATTN)),
        pl.BlockSpec((tm, LANES), pos), pl.BlockSpec((tm, LANES), pos),
        _const_spec((1, LANES)),
    ]
    return pl.pallas_call(
        functools.partial(_inproj_kernel, with_kv=with_kv),
        grid=(n,), in_specs=in_specs, out_specs=out_specs, out_shape=out_shape,
        compiler_params=pltpu.CompilerParams(dimension_semantics=("arbitrary",),
                                             vmem_limit_bytes=VMEM_LIMIT),
        name="inproj",
    )(x, pw["ln0_g"], pw["ln0_b"], pw["w_in"], pw["q_norm"], pw["kv_norm"], pw["w_uq"],
      pw["w_ukv"], cos_t, sin_t, pw["dt_bias"])


def _attn_prompt_kernel(qi_ref, ki_ref, q_ref, kv_ref, krp_ref, o_ref, *scratch, tq, tk):
    m_refs = scratch[0:MLA_HEADS]
    acc_refs = scratch[MLA_HEADS:2 * MLA_HEADS]
    p_id = pl.program_id(1)
    qi = qi_ref[p_id]
    ki = ki_ref[p_id]
    k_last = ((qi + 1) * tq - 1) // tk

    @pl.when(ki == 0)
    def _():
        for h in range(MLA_HEADS):
            m_refs[h][...] = jnp.full(m_refs[h].shape, -jnp.inf, F32)
            acc_refs[h][...] = jnp.zeros(acc_refs[h].shape, F32)

    def step(masked):
        krp = krp_ref[...]
        ones_col = (lax.broadcasted_iota(jnp.int32, (tk, LANES), 1) == 0).astype(BF16)
        if masked:
            r = (qi * tq + lax.broadcasted_iota(jnp.int32, (tq, tk), 0)) // CHUNK
            c = (ki * tk + lax.broadcasted_iota(jnp.int32, (tq, tk), 1)) // CHUNK
            allowed = c <= r
        for h in range(MLA_HEADS):
            qh = q_ref[:, h * QK_PAD:(h + 1) * QK_PAD]
            kh = jnp.concatenate([kv_ref[:, h * QK_NOPE:(h + 1) * QK_NOPE], krp], axis=1)
            s = _dot_nt(qh, kh)
            if masked:
                s = jnp.where(allowed, s, -jnp.inf)
            m_old = m_refs[h][...]
            m_new = jnp.maximum(m_old, jnp.max(s, axis=1, keepdims=True))
            alpha = jnp.exp2(m_old - m_new)
            p = jnp.exp2(s - m_new).astype(BF16)
            v_aug = jnp.concatenate([kv_ref[:, D_ATTN + h * V_DIM:D_ATTN + (h + 1) * V_DIM], ones_col], axis=1)
            acc_refs[h][...] = alpha * acc_refs[h][...] + _dot(p, v_aug)
            m_refs[h][...] = m_new

    @pl.when(ki < k_last)
    def _():
        step(False)

    @pl.when(ki == k_last)
    def _():
        step(True)
        for h in range(MLA_HEADS):
            acc = acc_refs[h][...]
            o_ref[:, h * V_DIM:(h + 1) * V_DIM] = (acc[:, :V_DIM] / acc[:, V_DIM:V_DIM + 1]).astype(BF16)


def _attn_prompt(q, kv, krp, *, tq, tk):
    B, L, _ = q.shape
    nq = L // tq
    pairs = [(i, j) for i in range(nq) for j in range(((i + 1) * tq - 1) // tk + 1)]
    qi_tab = jnp.asarray(np.array([p[0] for p in pairs], np.int32))
    ki_tab = jnp.asarray(np.array([p[1] for p in pairs], np.int32))
    grid_spec = pltpu.PrefetchScalarGridSpec(
        num_scalar_prefetch=2,
        grid=(B, len(pairs)),
        in_specs=[
            pl.BlockSpec((None, tq, MLA_HEADS * QK_PAD), lambda b, p, qi, ki: (b, qi[p], 0)),
            pl.BlockSpec((None, tk, 2 * D_ATTN), lambda b, p, qi, ki: (b, ki[p], 0)),
            pl.BlockSpec((None, tk, LANES), lambda b, p, qi, ki: (b, ki[p], 0)),
        ],
        out_specs=pl.BlockSpec((None, tq, D_ATTN), lambda b, p, qi, ki: (b, qi[p], 0)),
        scratch_shapes=([pltpu.VMEM((tq, 1), F32)] * MLA_HEADS
                        + [pltpu.VMEM((tq, 2 * V_DIM), F32)] * MLA_HEADS),
    )
    return pl.pallas_call(
        functools.partial(_attn_prompt_kernel, tq=tq, tk=tk),
        grid_spec=grid_spec,
        out_shape=jax.ShapeDtypeStruct((B, L, D_ATTN), BF16),
        compiler_params=pltpu.CompilerParams(dimension_semantics=("arbitrary", "arbitrary"),
                                             vmem_limit_bytes=VMEM_LIMIT),
        name="attn_prompt",
    )(qi_tab, ki_tab, q, kv, krp)


def _attn_sample_kernel(q_ref, wukt_ref, wuv_ref, latp_ref, krpast_ref, latn_ref, krn_ref, o_ref,
                        qlat_ref, qrp_ref, m_ref, l_ref, acc_ref, *, ls, past, tk, n_new_pad):
    k = pl.program_id(1)
    nk = pl.num_programs(1)
    rows = MLA_HEADS * ls

    @pl.when(k == 0)
    def _():
        for h in range(MLA_HEADS):
            qn = q_ref[:, h * QK_PAD:h * QK_PAD + LANES]
            qlat_ref[h * ls:(h + 1) * ls, :] = _dot(qn, wukt_ref[h]).astype(BF16)
            qrp_ref[h * ls:(h + 1) * ls, :] = q_ref[:, h * QK_PAD + LANES:(h + 1) * QK_PAD]
        m_ref[...] = jnp.full(m_ref.shape, -jnp.inf, F32)
        l_ref[...] = jnp.zeros(l_ref.shape, F32)
        acc_ref[...] = jnp.zeros(acc_ref.shape, F32)

    q_chunk = (past + lax.broadcasted_iota(jnp.int32, (rows, 1), 0) % ls) // CHUNK

    def update(s, lat_b):
        m_old = m_ref[...]
        m_new = jnp.maximum(m_old, jnp.max(s, axis=1, keepdims=True))
        alpha = jnp.exp2(m_old - m_new)
        p = jnp.exp2(s - m_new)
        l_ref[...] = alpha * l_ref[...] + jnp.sum(p, axis=1, keepdims=True)
        acc_ref[...] = alpha * acc_ref[...] + _dot(p.astype(BF16), lat_b)
        m_ref[...] = m_new

    lat_b = latp_ref[...].astype(BF16)
    kr_b = krpast_ref[...].astype(BF16)
    s = _dot_nt(qlat_ref[...], lat_b) + _dot_nt(qrp_ref[:, :QK_ROPE], kr_b)
    k_pos = k * tk + lax.broadcasted_iota(jnp.int32, (rows, tk), 1)
    s = jnp.where(k_pos // CHUNK <= q_chunk, s, -jnp.inf)
    update(s, lat_b)

    @pl.when(k == nk - 1)
    def _():
        latn_b = latn_ref[...].astype(BF16)
        s2 = _dot_nt(qlat_ref[...], latn_b) + _dot_nt(qrp_ref[...], krn_ref[...])
        j = lax.broadcasted_iota(jnp.int32, (rows, n_new_pad), 1)
        ok = ((past + j) // CHUNK <= q_chunk) & (j < ls)
        update(jnp.where(ok, s2, -jnp.inf), latn_b)
        o_lat = (acc_ref[...] / l_ref[...]).astype(BF16)
        for h in range(MLA_HEADS):
            o_ref[:, h * V_DIM:(h + 1) * V_DIM] = _dot(o_lat[h * ls:(h + 1) * ls, :], wuv_ref[h]).astype(BF16)


def _attn_sample(q, pw, lat_past, kr_past, lat_new, krp_new, *, tk):
    Bs, Ls, _ = q.shape
    past = lat_past.shape[1]
    n_new_pad = LANES
    lat_new = jnp.pad(lat_new, ((0, 0), (0, n_new_pad - Ls), (0, 0)))
    krp_new = jnp.pad(krp_new, ((0, 0), (0, n_new_pad - Ls), (0, 0)))
    rows = MLA_HEADS * Ls
    return pl.pallas_call(
        functools.partial(_attn_sample_kernel, ls=Ls, past=past, tk=tk, n_new_pad=n_new_pad),
        grid=(Bs, past // tk),
        in_specs=[
            pl.BlockSpec((None, Ls, MLA_HEADS * QK_PAD), lambda b, k: (b, 0, 0)),
            _const_spec((MLA_HEADS, QK_NOPE, KV_LORA)),
            _const_spec((MLA_HEADS, KV_LORA, V_DIM)),
            pl.BlockSpec((None, tk, KV_LORA), lambda b, k: (b, k, 0)),
            pl.BlockSpec((None, tk, QK_ROPE), lambda b, k: (b, k, 0)),
            pl.BlockSpec((None, n_new_pad, KV_LORA), lambda b, k: (b, 0, 0)),
            pl.BlockSpec((None, n_new_pad, LANES), lambda b, k: (b, 0, 0)),
        ],
        out_specs=pl.BlockSpec((None, Ls, D_ATTN), lambda b, k: (b, 0, 0)),
        out_shape=jax.ShapeDtypeStruct((Bs, Ls, D_ATTN), BF16),
        scratch_shapes=[pltpu.VMEM((rows, KV_LORA), BF16), pltpu.VMEM((rows, LANES), BF16),
                        pltpu.VMEM((rows, 1), F32), pltpu.VMEM((rows, 1), F32),
                        pltpu.VMEM((rows, KV_LORA), F32)],
        compiler_params=pltpu.CompilerParams(dimension_semantics=("arbitrary", "arbitrary"),
                                             vmem_limit_bytes=VMEM_LIMIT),
        name="attn_sample",
    )(q, pw["w_ukt"], pw["w_uvh"], lat_past, kr_past, lat_new, krp_new)


def _ssd_consts(q):
    hb = MXU_DIM // q
    hq = SSM_HEADS * q
    lane = np.arange(hq)
    ltri = (np.arange(q)[None, :] <= np.arange(q)[:, None]).astype(np.float32)
    sel_c = (np.arange(LANES)[:, None] == (lane // q)[None, :]).astype(np.float32)
    sel_p = (np.arange(LANES)[:, None] == (np.arange(D_SSM) // SSM_HEAD_DIM)[None, :]).astype(np.float32)
    diag = (np.arange(q)[:, None] == (lane % q)[None, :]).astype(np.float32)
    causal = (np.arange(q)[:, None] >= (lane % q)[None, :]).astype(np.float32)
    bd_rows = np.arange(hb * q) // q
    bd_cols = np.arange(hb * SSM_HEAD_DIM) // SSM_HEAD_DIM
    bdmask = (bd_rows[:, None] == bd_cols[None, :]).astype(np.float32)
    return dict(ltri=jnp.asarray(ltri, BF16), sel_c=jnp.asarray(sel_c, BF16),
                sel_p=jnp.asarray(sel_p, BF16), diag=jnp.asarray(diag, F32),
                causal=jnp.asarray(causal, F32), bdmask=jnp.asarray(bdmask, BF16))


def _ssd_kernel(xbc_ref, dt_ref, z_ref, cprev_ref, h0_ref, cw_ref, cb_ref, a_ref, dskip_ref, norm_ref,
                ltri_ref, selc_ref, selp_ref, diag_ref, causal_ref, bdmask_ref,
                y_ref, cout_ref, hout_ref,
                ext_ref, u_ref, ht_ref, *, q, tt):
    t = pl.program_id(1)
    nt = pl.num_programs(1)
    hb = MXU_DIM // q
    nblk = SSM_HEADS // hb
    gw = HEADS_PER_GROUP * SSM_HEAD_DIM
    pad = 8

    @pl.when(t == 0)
    def _():
        ext_ref[0:pad, :] = jnp.zeros((pad, CONV_CH), F32)
        ext_ref[pad - (CONV_K - 1):pad, :] = cprev_ref[...]
        ht_ref[...] = h0_ref[...].T

    xbc = xbc_ref[...]
    ext_ref[pad:pad + tt, :] = xbc
    cout_ref[...] = xbc[tt - (CONV_K - 1):, :]
    conv = cb_ref[...] + cw_ref[0:1, :] * ext_ref[pad - 3:pad - 3 + tt, :]
    for k in range(1, CONV_K):
        conv = conv + cw_ref[k:k + 1, :] * ext_ref[pad - 3 + k:pad - 3 + k + tt, :]
    u_ref[...] = _silu(conv)
    ext_ref[pad - (CONV_K - 1):pad, :] = xbc[tt - (CONV_K - 1):, :]

    a_row = a_ref[...]
    d_row = dskip_ref[...]
    norm_row = norm_ref[...]

    def chunk(c, carry):
        r0 = pl.multiple_of(c * q, q)
        u = u_ref[pl.ds(r0, q), :]
        xs = u[:, :D_SSM]
        dt = dt_ref[pl.ds(r0, q), :]
        a_cs = _exact_dot_l(ltri_ref[...], dt * a_row)
        ap = _exact_dot(a_cs, selp_ref[...])
        dtp = _exact_dot(dt, selp_ref[...])
        a_i = ap if q == SSM_HEAD_DIM else _exact_dot(a_cs, selc_ref[...])
        a_j = jnp.sum(a_i * diag_ref[...], axis=0, keepdims=True)
        decay = jnp.exp(jnp.where(causal_ref[...] > 0.5, a_i - a_j, -jnp.inf))
        ap_last = ap[q - 1:q, :]
        xdt = (xs * dtp).astype(BF16)
        xw = (xs * (jnp.exp(ap_last - ap) * dtp)).astype(BF16)
        e_ap = jnp.exp(ap)

        cb_parts, y_off_parts = [], []
        for g in range(SSM_GROUPS):
            bm = u[:, D_SSM + g * D_STATE:D_SSM + (g + 1) * D_STATE].astype(BF16)
            cm = u[:, D_SSM + (SSM_GROUPS + g) * D_STATE:D_SSM + (SSM_GROUPS + g + 1) * D_STATE].astype(BF16)
            cb_parts.append(_dot_nt(cm, jnp.concatenate([bm] * HEADS_PER_GROUP, axis=0)))
            gl = slice(g * gw, (g + 1) * gw)
            ht_g = ht_ref[:, gl]
            y_off_parts.append(_dot(cm, ht_g.astype(BF16)) * e_ap[:, gl])
            st = _dot_tn(bm, xw[:, gl])
            ht_ref[:, gl] = ht_g * jnp.exp(ap_last[:, gl]) + st
        w_all = (jnp.concatenate(cb_parts, axis=1) * decay).astype(BF16)
        y_diag_parts = []
        for b in range(nblk):
            x_b = xdt[:, b * hb * SSM_HEAD_DIM:(b + 1) * hb * SSM_HEAD_DIM]
            bd = jnp.concatenate([x_b] * hb, axis=0) * bdmask_ref[...]
            y_diag_parts.append(_dot(w_all[:, b * MXU_DIM:(b + 1) * MXU_DIM], bd))
        y = jnp.concatenate(y_diag_parts, axis=1) + jnp.concatenate(y_off_parts, axis=1)

        y = y + d_row * xs
        y = y * _silu(z_ref[pl.ds(r0, q), :].astype(F32))
        outs = []
        for g in range(SSM_GROUPS):
            yg = y[:, g * gw:(g + 1) * gw]
            outs.append(yg * lax.rsqrt(jnp.mean(yg * yg, axis=1, keepdims=True) + RMS_EPS))
        y_ref[pl.ds(r0, q), :] = (jnp.concatenate(outs, axis=1) * norm_row).astype(BF16)
        return carry

    lax.fori_loop(0, tt // q, chunk, 0, unroll=True)

    @pl.when(t == nt - 1)
    def _():
        hout_ref[...] = ht_ref[...].T


def _ssd(xbc, dt, z, conv_prev, h0, pw, *, q, tt):
    B, L, _ = xbc.shape
    cs = _ssd_consts(q)
    hq = SSM_HEADS * q
    hb = MXU_DIM // q
    tile = lambda w: pl.BlockSpec((None, tt, w), lambda b, t: (b, t, 0))
    per_b = lambda s: pl.BlockSpec((None,) + s, lambda b, t: (b, 0, 0))
    return pl.pallas_call(
        functools.partial(_ssd_kernel, q=q, tt=tt),
        grid=(B, L // tt),
        in_specs=[tile(CONV_CH), tile(LANES), tile(D_SSM), per_b((CONV_K - 1, CONV_CH)),
                  per_b((D_SSM, D_STATE)),
                  _const_spec((CONV_K, CONV_CH)), _const_spec((1, CONV_CH)), _const_spec((1, LANES)),
                  _const_spec((1, D_SSM)), _const_spec((1, D_SSM)),
                  _const_spec((q, q)), _const_spec((LANES, hq)), _const_spec((LANES, D_SSM)),
                  _const_spec((q, hq)), _const_spec((q, hq)),
                  _const_spec((MXU_DIM, hb * SSM_HEAD_DIM))],
        out_specs=[tile(D_SSM), per_b((CONV_K - 1, CONV_CH)), per_b((D_SSM, D_STATE))],
        out_shape=[jax.ShapeDtypeStruct((B, L, D_SSM), BF16),
                   jax.ShapeDtypeStruct((B, CONV_K - 1, CONV_CH), F32),
                   jax.ShapeDtypeStruct((B, D_SSM, D_STATE), F32)],
        scratch_shapes=[pltpu.VMEM((8 + tt, CONV_CH), F32), pltpu.VMEM((tt, CONV_CH), F32),
                        pltpu.VMEM((D_STATE, D_SSM), F32)],
        compiler_params=pltpu.CompilerParams(dimension_semantics=("arbitrary", "arbitrary"),
                                             vmem_limit_bytes=VMEM_LIMIT),
        name="ssd",
    )(xbc, dt, z, conv_prev, h0, pw["conv_w"], pw["conv_b"], pw["a_neg"], pw["d_skip"], pw["ssm_norm"],
      cs["ltri"], cs["sel_c"], cs["sel_p"], cs["diag"], cs["causal"], cs["bdmask"])


R_OFF = N_EXPERT_GROUPS


def _outproj_kernel(xp_ref, xs_ref, attnp_ref, attns_ref, yp_ref, ys_ref, g0_ref, b0_ref, wo_a_ref, wo_s_ref,
                    g1_ref, b1_ref, wr_ref, br_ref, ltri_ref, h_ref, hpk_ref, route_ref, cnt_ref,
                    *, n_prompt_tiles, tm, n_groups):
    i = pl.program_id(0)
    is_p = i < n_prompt_tiles

    @pl.when(i == 0)
    def _():
        cnt_ref[...] = jnp.zeros(cnt_ref.shape, F32)

    cnt = cnt_ref[0:1, :]
    for g in range(n_groups):
        rows = slice(g * (tm // n_groups), (g + 1) * (tm // n_groups))
        refs = (xp_ref, xs_ref, attnp_ref, attns_ref, yp_ref, ys_ref)
        cnt = _outproj_rows(rows, is_p, cnt, refs, g0_ref, b0_ref, wo_a_ref, wo_s_ref, g1_ref, b1_ref,
                            wr_ref, br_ref, ltri_ref, h_ref, hpk_ref, route_ref)
    cnt_ref[...] = jnp.broadcast_to(cnt, cnt_ref.shape)


def _outproj_rows(rows, is_p, cnt, refs, g0_ref, b0_ref, wo_a_ref, wo_s_ref, g1_ref, b1_ref, wr_ref, br_ref,
                  ltri_ref, h_ref, hpk_ref, route_ref):
    xp_ref, xs_ref, attnp_ref, attns_ref, yp_ref, ys_ref = refs
    x = jnp.where(is_p, xp_ref[rows, :], xs_ref[rows, :])
    attn = jnp.where(is_p, attnp_ref[rows, :], attns_ref[rows, :])
    yssm = jnp.where(is_p, yp_ref[rows, :], ys_ref[rows, :])

    xn = _layernorm(x, g0_ref[...], b0_ref[...])
    mixed = _dot(attn, wo_a_ref[...]) + _dot(yssm, wo_s_ref[...])
    h = _layernorm(ALPHA * xn + mixed, g1_ref[...], b1_ref[...])
    h_ref[rows, :] = h

    h_hi = h.astype(BF16)
    h_hi32 = h_hi.astype(F32)
    hpk_ref[rows, :] = _pack_bf16_pairs(h_hi32)
    h_lo = (h - h_hi32).astype(BF16)
    tm = h.shape[0]
    prod = _dot(jnp.concatenate([h_hi, h_lo], axis=0), wr_ref[...])
    lg = (prod[:tm, :LANES] + prod[:tm, LANES:]) + (prod[tm:, :LANES] + prod[tm:, LANES:]) + br_ref[...]
    lane = lax.broadcasted_iota(jnp.int32, lg.shape, 1)
    big = jnp.int32(1 << 20)
    gl = jnp.where(lane < N_EXPERT_GROUPS, lg, -jnp.inf)
    gmax = jnp.max(gl, axis=1, keepdims=True)
    grp = jnp.min(jnp.where(gl == gmax, lane, big), axis=1, keepdims=True)
    g_w = 1.0 / jnp.sum(jnp.exp(gl - gmax), axis=1, keepdims=True)
    in_grp = (lane >= R_OFF) & (lane < R_OFF + N_EXPERTS) & ((lane - R_OFF) // EXPERTS_PER_GROUP == grp)
    el = jnp.where(in_grp, lg, -jnp.inf)
    emax = jnp.max(el, axis=1, keepdims=True)
    ee = jnp.exp(el - emax)
    prob = jnp.where(in_grp, ee / jnp.sum(ee, axis=1, keepdims=True), -1.0)
    p1 = jnp.max(prob, axis=1, keepdims=True)
    i1 = jnp.min(jnp.where(prob == p1, lane, big), axis=1, keepdims=True)
    prob2 = jnp.where(lane == i1, -1.0, prob)
    p2 = jnp.max(prob2, axis=1, keepdims=True)
    i2 = jnp.min(jnp.where(prob2 == p2, lane, big), axis=1, keepdims=True)
    denom = p1 + p2
    oh1 = (lane == i1).astype(F32)
    oh2 = (lane == i2).astype(F32)
    oh = oh1 + oh2
    before = _dot(ltri_ref[...], oh.astype(BF16)) + cnt
    rank1 = jnp.sum(before * oh1, axis=1, keepdims=True)
    rank2 = jnp.sum(before * oh2, axis=1, keepdims=True)
    route = jnp.where(lane == 0, (i1 - R_OFF).astype(F32),
                      jnp.where(lane == 1, (i2 - R_OFF).astype(F32),
                                jnp.where(lane == 2, g_w * p1 / denom,
                                          jnp.where(lane == 3, g_w * p2 / denom,
                                                    jnp.where(lane == 4, rank1,
                                                              jnp.where(lane == 5, rank2, 0.0))))))
    route_ref[rows, :] = route
    return cnt + jnp.sum(oh, axis=0, keepdims=True)


def _outproj(xp, xs, attn_p, attn_s, y_p, y_s, pw, *, tm):
    Tp, Ts = xp.shape[0], xs.shape[0]
    npt, nst = Tp // tm, Ts // tm
    T = Tp + Ts
    row = lambda i: (i, 0)
    prow = lambda i: (jnp.minimum(i, npt - 1), 0)
    srow = lambda i: (jnp.maximum(i - npt, 0), 0)
    n_groups = 1
    gm = tm // n_groups
    ltri = jnp.asarray(np.tril(np.ones((gm, gm), np.float32), -1), BF16)
    return pl.pallas_call(
        functools.partial(_outproj_kernel, n_prompt_tiles=npt, tm=tm, n_groups=n_groups),
        grid=(npt + nst,),
        in_specs=[pl.BlockSpec((tm, D_MODEL), prow), pl.BlockSpec((tm, D_MODEL), srow),
                  pl.BlockSpec((tm, D_ATTN), prow), pl.BlockSpec((tm, D_ATTN), srow),
                  pl.BlockSpec((tm, D_SSM), prow), pl.BlockSpec((tm, D_SSM), srow),
                  _const_spec((1, D_MODEL)), _const_spec((1, D_MODEL)),
                  _const_spec((D_ATTN, D_MODEL)), _const_spec((D_SSM, D_MODEL)),
                  _const_spec((1, D_MODEL)), _const_spec((1, D_MODEL)),
                  _const_spec((D_MODEL, 2 * LANES)), _const_spec((1, LANES)),
                  _const_spec((gm, gm))],
        out_specs=[pl.BlockSpec((tm, D_MODEL), row), pl.BlockSpec((tm, D_MODEL // 2), row),
                   pl.BlockSpec((tm, LANES), row), pl.BlockSpec((8, LANES), lambda i: (0, 0))],
        out_shape=[jax.ShapeDtypeStruct((T, D_MODEL), F32), jax.ShapeDtypeStruct((T, D_MODEL // 2), jnp.uint32),
                   jax.ShapeDtypeStruct((T, LANES), F32), jax.ShapeDtypeStruct((8, LANES), F32)],
        compiler_params=pltpu.CompilerParams(dimension_semantics=("arbitrary",),
                                             vmem_limit_bytes=VMEM_LIMIT),
        name="outproj",
    )(xp, xs, attn_p, attn_s, y_p, y_s, pw["ln0_g"], pw["ln0_b"], pw["w_o_a"], pw["w_o_s"],
      pw["ln1_g"], pw["ln1_b"], pw["w_r"], pw["b_r"], ltri)


def _experts_kernel(blk_e_ref, n_used_ref, tok_ref, tok_n1_ref, tok_n2_ref, h_hbm, wg_ref, wu_ref, wd_ref,
                    y_ref, xbuf, sem, wg_b, wu_b, wd_b, prev_e, *, rb):
    i = pl.program_id(0)
    n_used = n_used_ref[0]
    slot = i % GATHER_SLOTS

    def row_copy(tok, dst_slot, r):
        return pltpu.make_async_copy(h_hbm.at[pl.ds(tok, 1)], xbuf.at[dst_slot, pl.ds(r, 1)],
                                     sem.at[dst_slot])

    def wait_block(s):
        pltpu.make_async_copy(h_hbm.at[pl.ds(0, rb)], xbuf.at[s], sem.at[s]).wait()

    @pl.when(i == 0)
    def _():
        prev_e[0] = -1

        def body(r, c):
            row_copy(tok_ref[r], 0, r).start()
            row_copy(tok_n1_ref[r], 1, r).start()
            return c
        lax.fori_loop(0, rb, body, 0)

    @pl.when(i < n_used)
    def _():
        wait_block(slot)
        e = blk_e_ref[i]

        @pl.when(e != prev_e[0])
        def _():
            wg_b[...] = wg_ref[...].astype(BF16)
            wu_b[...] = wu_ref[...].astype(BF16)
            wd_b[...] = wd_ref[...].astype(BF16)
            prev_e[0] = e

        xb = _unpack_bf16_pairs(xbuf[slot]).astype(BF16)
        ahead = (i + GATHER_AHEAD) % GATHER_SLOTS
        for r in range(rb):
            row_copy(tok_n2_ref[r], ahead, r).start(priority=r % 2)
        hid = (_silu(_dot(xb, wg_b[...])) * _dot(xb, wu_b[...])).astype(BF16)
        y = _dot(hid, wd_b[...])
        y_ref[...] = _pack_bf16_pairs(y.astype(BF16).astype(F32))

        @pl.when(i == n_used - 1)
        def _():
            wait_block((i + 1) % GATHER_SLOTS)
            wait_block(ahead)

    @pl.when(i >= n_used)
    def _():
        y_ref[...] = jnp.zeros(y_ref.shape, jnp.uint32)


def _experts(hpk_all, row_tok, blk_e, n_used, w_gate, w_up, w_down, *, rb):
    n_blk = blk_e.shape[0]
    n_rows = n_blk * rb
    grid_spec = pltpu.PrefetchScalarGridSpec(
        num_scalar_prefetch=2,
        grid=(n_blk,),
        in_specs=[
            pl.BlockSpec((rb,), lambda i, be, nu: (jnp.minimum(i, nu[0] - 1),), memory_space=pltpu.SMEM),
            pl.BlockSpec((rb,), lambda i, be, nu: (jnp.minimum(i + 1, nu[0] - 1),), memory_space=pltpu.SMEM),
            pl.BlockSpec((rb,), lambda i, be, nu: (jnp.minimum(i + GATHER_AHEAD, nu[0] - 1),),
                         memory_space=pltpu.SMEM),
            pl.BlockSpec(memory_space=pl.ANY),
            pl.BlockSpec((None, D_MODEL, D_EXPERT), lambda i, be, nu: (be[jnp.minimum(i, nu[0] - 1)], 0, 0)),
            pl.BlockSpec((None, D_MODEL, D_EXPERT), lambda i, be, nu: (be[jnp.minimum(i, nu[0] - 1)], 0, 0)),
            pl.BlockSpec((None, D_EXPERT, D_MODEL), lambda i, be, nu: (be[jnp.minimum(i, nu[0] - 1)], 0, 0)),
        ],
        out_specs=pl.BlockSpec((rb, D_MODEL // 2), lambda i, be, nu: (i, 0)),
        scratch_shapes=[pltpu.VMEM((GATHER_SLOTS, rb, D_MODEL // 2), jnp.uint32),
                        pltpu.SemaphoreType.DMA((GATHER_SLOTS,)),
                        pltpu.VMEM((D_MODEL, D_EXPERT), BF16), pltpu.VMEM((D_MODEL, D_EXPERT), BF16),
                        pltpu.VMEM((D_EXPERT, D_MODEL), BF16), pltpu.SMEM((1,), jnp.int32)],
    )
    return pl.pallas_call(
        functools.partial(_experts_kernel, rb=rb),
        grid_spec=grid_spec,
        out_shape=jax.ShapeDtypeStruct((n_rows, D_MODEL // 2), jnp.uint32),
        compiler_params=pltpu.CompilerParams(dimension_semantics=("arbitrary",),
                                             vmem_limit_bytes=VMEM_LIMIT),
        name="experts",
    )(blk_e, n_used, row_tok, row_tok, row_tok, hpk_all, w_gate, w_up, w_down)


def _combine_kernel(pos_ref, pos_n1_ref, pos_n2_ref, y_hbm, h_ref, route_ref, g2_ref, b2_ref, op_ref, os_ref,
                    ybuf, sem, *, tm, n_prompt_tiles):
    i = pl.program_id(0)
    n = pl.num_programs(0)
    slot = i % GATHER_SLOTS
    rows = TOP_K * tm

    def row_copy(idx, dst_slot, r):
        return pltpu.make_async_copy(y_hbm.at[pl.ds(idx, 1)], ybuf.at[dst_slot, pl.ds(r, 1)],
                                     sem.at[dst_slot])

    def wait_tile(s):
        pltpu.make_async_copy(y_hbm.at[pl.ds(0, rows)], ybuf.at[s], sem.at[s]).wait()

    @pl.when(i == 0)
    def _():
        def body(r, c):
            row_copy(pos_ref[r], 0, r).start()
            row_copy(pos_n1_ref[r], 1, r).start()
            return c
        lax.fori_loop(0, rows, body, 0)

    wait_tile(slot)
    gate1 = route_ref[:, 2:3]
    gate2 = route_ref[:, 3:4]
    f = (_unpack_bf16_pairs(ybuf[slot, 0:tm, :]) * gate1
         + _unpack_bf16_pairs(ybuf[slot, tm:rows, :]) * gate2)
    ahead = (i + GATHER_AHEAD) % GATHER_SLOTS
    for r in range(rows):
        row_copy(pos_n2_ref[r], ahead, r).start(priority=r % 2)
    out = _layernorm(ALPHA * h_ref[...] + f, g2_ref[...], b2_ref[...])

    @pl.when(i < n_prompt_tiles)
    def _():
        op_ref[...] = out

    @pl.when(i >= n_prompt_tiles)
    def _():
        os_ref[...] = out

    @pl.when(i == n - 1)
    def _():
        wait_tile((i + 1) % GATHER_SLOTS)
        wait_tile(ahead)


def _combine(y_rows, pos_tiles, h_all, route, pw, *, tm, t_prompt):
    T = h_all.shape[0]
    n = T // tm
    npt = t_prompt // tm
    rows = TOP_K * tm
    return pl.pallas_call(
        functools.partial(_combine_kernel, tm=tm, n_prompt_tiles=npt),
        grid=(n,),
        in_specs=[
            pl.BlockSpec((rows,), lambda i: (i,), memory_space=pltpu.SMEM),
            pl.BlockSpec((rows,), lambda i: (jnp.minimum(i + 1, n - 1),), memory_space=pltpu.SMEM),
            pl.BlockSpec((rows,), lambda i: (jnp.minimum(i + GATHER_AHEAD, n - 1),), memory_space=pltpu.SMEM),
            pl.BlockSpec(memory_space=pl.ANY),
            pl.BlockSpec((tm, D_MODEL), lambda i: (i, 0)),
            pl.BlockSpec((tm, LANES), lambda i: (i, 0)),
            _const_spec((1, D_MODEL)), _const_spec((1, D_MODEL)),
        ],
        out_specs=[pl.BlockSpec((tm, D_MODEL), lambda i: (jnp.minimum(i, npt - 1), 0)),
                   pl.BlockSpec((tm, D_MODEL), lambda i: (jnp.maximum(i - npt, 0), 0))],
        out_shape=[jax.ShapeDtypeStruct((t_prompt, D_MODEL), F32),
                   jax.ShapeDtypeStruct((T - t_prompt, D_MODEL), F32)],
        scratch_shapes=[pltpu.VMEM((GATHER_SLOTS, rows, D_MODEL // 2), jnp.uint32),
                        pltpu.SemaphoreType.DMA((GATHER_SLOTS,))],
        compiler_params=pltpu.CompilerParams(dimension_semantics=("arbitrary",),
                                             vmem_limit_bytes=VMEM_LIMIT),
        name="combine",
    )(pos_tiles, pos_tiles, pos_tiles, y_rows, h_all, route, pw["ln2_g"], pw["ln2_b"])


def _prep_weights(ln0_g, ln0_b, w_in, q_norm, w_uq, kv_norm, w_uk, w_uv, conv_w, conv_b, dt_bias, a_log,
                  d_skip, ssm_norm, w_o, ln1_g, ln1_b, w_rg, b_rg, w_re, b_re, ln2_g, ln2_b):
    s_q, s_kv, s_kr, s_z, s_xbc = Q_LORA, Q_LORA + KV_LORA, Q_LORA + KV_LORA + QK_ROPE, \
        Q_LORA + KV_LORA + QK_ROPE + D_SSM, Q_LORA + KV_LORA + QK_ROPE + D_SSM + CONV_CH
    zc = lambda n: jnp.zeros((D_MODEL, n), F32)
    w_in_p = jnp.concatenate([
        w_in[:, :s_q], w_in[:, s_q:s_kv], w_in[:, s_kr:s_z], w_in[:, s_z:s_xbc],
        w_in[:, s_kv:s_kr], zc(LANES - QK_ROPE), w_in[:, s_xbc:], zc(LANES - SSM_HEADS)], axis=1)
    wq = w_uq.reshape(Q_LORA, MLA_HEADS, QK_DIM) * (ATTN_SCALE * math.log2(math.e))
    wq = jnp.concatenate([wq, jnp.zeros((Q_LORA, MLA_HEADS, QK_PAD - QK_DIM), F32)], axis=2)
    w_r = jnp.concatenate([w_rg, w_re, jnp.zeros((D_MODEL, LANES - R_OFF - N_EXPERTS), F32)], axis=1)
    w_r_hi = w_r.astype(BF16)
    row = lambda v: v.reshape(1, -1)
    pad_row = lambda v: jnp.pad(v, (0, LANES - v.shape[0])).reshape(1, LANES)
    return dict(
        ln0_g=row(ln0_g), ln0_b=row(ln0_b), w_in=w_in_p.astype(BF16),
        q_norm=row(q_norm), kv_norm=row(kv_norm),
        w_uq=wq.reshape(Q_LORA, MLA_HEADS * QK_PAD).astype(BF16),
        w_ukv=jnp.concatenate([w_uk.reshape(KV_LORA, D_ATTN), w_uv.reshape(KV_LORA, D_ATTN)],
                              axis=1).astype(BF16),
        w_ukt=jnp.transpose(w_uk, (1, 2, 0)).astype(BF16),
        w_uvh=jnp.transpose(w_uv, (1, 0, 2)).astype(BF16),
        conv_w=conv_w, conv_b=row(conv_b), dt_bias=pad_row(dt_bias),
        a_neg=pad_row(-jnp.exp(a_log)), d_skip=row(jnp.repeat(d_skip, SSM_HEAD_DIM)),
        ssm_norm=row(ssm_norm),
        w_o_a=w_o[:D_ATTN].astype(BF16), w_o_s=w_o[D_ATTN:].astype(BF16),
        ln1_g=row(ln1_g), ln1_b=row(ln1_b),
        w_r=jnp.concatenate([w_r_hi, (w_r - w_r_hi.astype(F32)).astype(BF16)], axis=1),
        b_r=pad_row(jnp.concatenate([b_rg, b_re])),
        ln2_g=row(ln2_g), ln2_b=row(ln2_b),
    )


def _rope_tables(pos):
    half = QK_ROPE // 2
    inv_freq = ROPE_THETA ** (-jnp.arange(half, dtype=F32) / half)
    ang = pos.astype(F32)[:, None] * inv_freq[None, :]
    cos, sin = jnp.cos(ang), jnp.sin(ang)
    zeros = jnp.zeros((pos.shape[0], LANES - QK_ROPE), F32)
    return (jnp.concatenate([cos, cos, zeros], axis=1), jnp.concatenate([-sin, sin, zeros], axis=1))


def _dispatch_tables(route, counts, *, rb, n_blk):
    T = route.shape[0]
    experts = jnp.arange(N_EXPERTS, dtype=jnp.int32)
    e = route[:, 0:TOP_K].astype(jnp.int32)
    rank = route[:, 4:4 + TOP_K].astype(jnp.int32)
    counts = counts.astype(jnp.int32)
    pcounts = (counts + rb - 1) // rb * rb
    pend = jnp.cumsum(pcounts)
    pstart = pend - pcounts
    pos = jnp.sum(jnp.where(e[:, :, None] == experts, pstart, 0), axis=2) + rank
    blk_start = jnp.arange(n_blk, dtype=jnp.int32) * rb
    blk_e = jnp.minimum(jnp.sum((pend[None, :] <= blk_start[:, None]).astype(jnp.int32), axis=1),
                        N_EXPERTS - 1)
    n_used = (pend[-1] // rb).reshape(1)
    tok = jnp.broadcast_to(jnp.arange(T, dtype=jnp.int32)[:, None], (T, TOP_K))
    row_tok = jnp.zeros((n_blk * rb,), jnp.int32).at[pos.reshape(-1)].set(
        tok.reshape(-1), unique_indices=True, mode="drop")
    return row_tok, blk_e, n_used, pos


def _largest_tile(n, cap):
    t = cap
    while n % t:
        t //= 2
    return t


def kernel(x_prompt, x_sample, cache_kv_latent, cache_k_rope, state_conv, state_ssm, ln0_g, ln0_b, w_in, q_norm, w_uq, kv_norm, w_uk, w_uv, conv_w, conv_b, dt_bias, a_log, d_skip, ssm_norm, w_o, ln1_g, ln1_b, w_rg, b_rg, w_re, b_re, w_gate, w_up, w_down, ln2_g, ln2_b):
    B, L, _ = x_prompt.shape
    Bs, Ls, _ = x_sample.shape
    past = cache_kv_latent.shape[2]
    Tp, Ts = B * L, Bs * Ls
    pw = _prep_weights(ln0_g, ln0_b, w_in[0], q_norm[0], w_uq[0], kv_norm[0], w_uk[0], w_uv[0], conv_w[0],
                       conv_b[0], dt_bias[0], a_log[0], d_skip[0], ssm_norm[0], w_o[0], ln1_g[0], ln1_b[0],
                       w_rg[0], b_rg[0], w_re[0], b_re[0], ln2_g[0], ln2_b[0])

    tm_p = _largest_tile(L, 512)
    xp = x_prompt.reshape(Tp, D_MODEL)
    cos_p, sin_p = _rope_tables(jnp.arange(L, dtype=jnp.int32))
    q_p, kv_p, lat_p, kr_p, krp_p, z_p, xbc_p, dt_p = _inproj(
        xp, pw, cos_p, sin_p, tm=tm_p, pos_blocks=L // tm_p, with_kv=True)
    attn_p = _attn_prompt(q_p.reshape(B, L, -1), kv_p.reshape(B, L, -1), krp_p.reshape(B, L, -1),
                          tq=_largest_tile(L, 512), tk=_largest_tile(L, 1024))
    q_ssd = min(CHUNK, L)
    y_p, conv_p, ssm_p = _ssd(
        xbc_p.reshape(B, L, -1), dt_p.reshape(B, L, -1), z_p.reshape(B, L, -1),
        jnp.zeros((B, CONV_K - 1, CONV_CH), F32), jnp.zeros((B, D_SSM, D_STATE), F32), pw,
        q=q_ssd, tt=_largest_tile(L, 512))

    xs = x_sample.reshape(Ts, D_MODEL)
    cos_s, sin_s = _rope_tables(past + jnp.arange(Ls, dtype=jnp.int32))
    tm_s = _largest_tile(Ts, 256)
    assert tm_s % Ls == 0
    q_s, _, lat_s, kr_s, krp_s, z_s, xbc_s, dt_s = _inproj(
        xs, pw, jnp.tile(cos_s, (tm_s // Ls, 1)), jnp.tile(sin_s, (tm_s // Ls, 1)),
        tm=tm_s, pos_blocks=1, with_kv=False)
    attn_s = _attn_sample(q_s.reshape(Bs, Ls, -1), pw, cache_kv_latent[0], cache_k_rope[0],
                          lat_s.reshape(Bs, Ls, -1), krp_s.reshape(Bs, Ls, -1),
                          tk=_largest_tile(past, 1024))
    y_s, conv_s, ssm_s = _ssd(
        xbc_s.reshape(Bs, Ls, -1), dt_s.reshape(Bs, Ls, -1), z_s.reshape(Bs, Ls, -1),
        state_conv[0], state_ssm[0].reshape(Bs, D_SSM, D_STATE), pw, q=min(CHUNK, Ls), tt=Ls)

    T = Tp + Ts
    tm_c = _largest_tile(math.gcd(Tp, Ts), 256)
    h_all, hpk_all, route, counts = _outproj(xp, xs, attn_p.reshape(Tp, -1), attn_s.reshape(Ts, -1),
                                             y_p.reshape(Tp, -1), y_s.reshape(Ts, -1), pw, tm=tm_c)
    rb = 256
    n_blk = (T * TOP_K + N_EXPERTS * (rb - 1) + rb - 1) // rb
    row_tok, blk_e, n_used, pos = _dispatch_tables(route, counts[0, R_OFF:R_OFF + N_EXPERTS], rb=rb, n_blk=n_blk)
    y_rows = _experts(hpk_all, row_tok, blk_e, n_used, w_gate[0], w_up[0], w_down[0], rb=rb)
    pos_tiles = pos.reshape(T // tm_c, tm_c, TOP_K).transpose(0, 2, 1).reshape(-1)
    out_p, out_s = _combine(y_rows, pos_tiles, h_all, route, pw, tm=tm_c, t_prompt=Tp)

    return (out_p.reshape(B, L, D_MODEL), out_s.reshape(Bs, Ls, D_MODEL),
            lat_p.reshape(1, B, L, KV_LORA), lat_s.reshape(1, Bs, Ls, KV_LORA),
            kr_p.reshape(1, B, L, QK_ROPE), kr_s.reshape(1, Bs, Ls, QK_ROPE),
            conv_p[None], conv_s[None],
            ssm_p.reshape(1, B, SSM_HEADS, SSM_HEAD_DIM, D_STATE),
            ssm_s.reshape(1, Bs, SSM_HEADS, SSM_HEAD_DIM, D_STATE))
```

```python
import functools
import math

import jax
import jax.numpy as jnp
import numpy as np
from jax import lax
from jax.experimental import pallas as pl
from jax.experimental.pallas import tpu as pltpu

F32 = jnp.float32
BF16 = jnp.bfloat16

D_MODEL = 2048
CHUNK = 64
MLA_HEADS = 8
QK_NOPE = 128
QK_ROPE = 64
QK_DIM = QK_NOPE + QK_ROPE
V_DIM = 128
Q_LORA = 512
KV_LORA = 512
ROPE_THETA = 10000.0
ATTN_SCALE = QK_DIM ** -0.5
D_ATTN = MLA_HEADS * V_DIM
D_SSM = 1024
SSM_HEAD_DIM = 64
SSM_HEADS = D_SSM // SSM_HEAD_DIM
SSM_GROUPS = 2
HEADS_PER_GROUP = SSM_HEADS // SSM_GROUPS
D_STATE = 128
CONV_K = 4
CONV_CH = D_SSM + 2 * SSM_GROUPS * D_STATE
N_EXPERT_GROUPS = 8
EXPERTS_PER_GROUP = 8
N_EXPERTS = N_EXPERT_GROUPS * EXPERTS_PER_GROUP
TOP_K = 2
D_EXPERT = 512
DEPTH = 1
ALPHA = (2 * DEPTH) ** 0.25
RMS_EPS = 1e-6
LN_EPS = 1e-5

LANES = 128
MXU_DIM = 256
VMEM_LIMIT = 56 * 1024 * 1024

QK_PAD = 2 * LANES
GATHER_AHEAD = 2
GATHER_SLOTS = GATHER_AHEAD + 1
C_Q, C_KV, C_Z, C_XBC, C_KR, C_DT = 0, 512, 1024, 2048, 3584, 3712
D_IN_PAD = 3840


def _const_spec(shape):
    nd = len(shape)
    return pl.BlockSpec(shape, lambda *_: (0,) * nd, pipeline_mode=pl.Buffered(1))


def _split3(a):
    hi = a.astype(BF16)
    r1 = a - hi.astype(F32)
    mid = r1.astype(BF16)
    lo = (r1 - mid.astype(F32)).astype(BF16)
    return hi, mid, lo


def _dot(a, b):
    return jnp.dot(a, b, preferred_element_type=F32)


def _dot_nt(a, b):
    return lax.dot_general(a, b, (((1,), (1,)), ((), ())), preferred_element_type=F32)


def _dot_tn(a, b):
    return lax.dot_general(a, b, (((0,), (0,)), ((), ())), preferred_element_type=F32)


def _exact_dot(a_f32, sel_bf16):
    hi, mid, lo = _split3(a_f32)
    return _dot(hi, sel_bf16) + _dot(mid, sel_bf16) + _dot(lo, sel_bf16)


def _exact_dot_l(sel_bf16, a_f32):
    hi, mid, lo = _split3(a_f32)
    return _dot(sel_bf16, hi) + _dot(sel_bf16, mid) + _dot(sel_bf16, lo)


def _layernorm(x, g, b):
    mu = jnp.mean(x, axis=-1, keepdims=True)
    xc = x - mu
    var = jnp.mean(xc * xc, axis=-1, keepdims=True)
    return xc * lax.rsqrt(var + LN_EPS) * g + b


def _rmsnorm(x, g):
    return x * lax.rsqrt(jnp.mean(x * x, axis=-1, keepdims=True) + RMS_EPS) * g


def _silu(x):
    return x * (1.0 / (1.0 + jnp.exp(-x)))


def _pack_bf16_pairs(x):
    n = x.shape[1] // 2
    bits = lax.bitcast_convert_type(x, jnp.uint32)
    return (bits[:, n:] & jnp.uint32(0xFFFF0000)) | (bits[:, :n] >> 16)


def _unpack_bf16_pairs(u):
    lo = lax.bitcast_convert_type(u << 16, F32)
    hi = lax.bitcast_convert_type(u & jnp.uint32(0xFFFF0000), F32)
    return jnp.concatenate([lo, hi], axis=1)


def _rope128(x, cos_t, sin_t):
    lane = lax.broadcasted_iota(jnp.int32, x.shape, 1)
    partner = jnp.where(lane < QK_ROPE // 2, pltpu.roll(x, LANES - QK_ROPE // 2, 1),
                        pltpu.roll(x, QK_ROPE // 2, 1))
    return x * cos_t + partner * sin_t


def _inproj_kernel(x_ref, g0_ref, b0_ref, w_in_ref, qn_ref, kvn_ref, w_uq_ref, w_ukv_ref,
                   cos_ref, sin_ref, dtb_ref,
                   q_ref, kv_ref, lat_ref, kr_ref, krp_ref, z_ref, xbc_ref, dt_ref, *, with_kv):
    xn = _layernorm(x_ref[...], g0_ref[...], b0_ref[...]).astype(BF16)
    cos_t = cos_ref[...]
    sin_t = sin_ref[...]

    c_q = _dot(xn, w_in_ref[:, C_Q:C_Q + Q_LORA])
    qb = _rmsnorm(c_q, qn_ref[...]).astype(BF16)
    for h in range(MLA_HEADS):
        qh = _dot(qb, w_uq_ref[:, h * QK_PAD:(h + 1) * QK_PAD])
        q_ref[:, h * QK_PAD:h * QK_PAD + LANES] = qh[:, :LANES].astype(BF16)
        q_ref[:, h * QK_PAD + LANES:(h + 1) * QK_PAD] = _rope128(qh[:, LANES:], cos_t, sin_t).astype(BF16)

    c_kv = _dot(xn, w_in_ref[:, C_KV:C_KV + KV_LORA])
    lat = _rmsnorm(c_kv, kvn_ref[...])
    lat_ref[...] = lat
    if with_kv:
        kv_ref[...] = _dot(lat.astype(BF16), w_ukv_ref[...]).astype(BF16)
    else:
        kv_ref[...] = jnp.zeros(kv_ref.shape, BF16)

    k_r = _rope128(_dot(xn, w_in_ref[:, C_KR:C_KR + LANES]), cos_t, sin_t)
    kr_ref[...] = k_r[:, :QK_ROPE]
    krp_ref[...] = k_r.astype(BF16)

    z_ref[...] = _dot(xn, w_in_ref[:, C_Z:C_Z + D_SSM]).astype(BF16)
    xbc_ref[...] = _dot(xn, w_in_ref[:, C_XBC:C_XBC + CONV_CH])

    dt_raw = _dot(xn, w_in_ref[:, C_DT:C_DT + LANES]) + dtb_ref[...]
    dt = jnp.maximum(dt_raw, 0.0) + jnp.log1p(jnp.exp(-jnp.abs(dt_raw)))
    lane = lax.broadcasted_iota(jnp.int32, dt.shape, 1)
    dt_ref[...] = jnp.where(lane < SSM_HEADS, dt, 0.0)


def _inproj(x, pw, cos_t, sin_t, *, tm, pos_blocks, with_kv):
    T = x.shape[0]
    n = T // tm
    row = lambda i: (i, 0)
    pos = lambda i: (i % pos_blocks, 0)
    kv_cols = 2 * D_ATTN if with_kv else LANES
    out_shape = [
        jax.ShapeDtypeStruct((T, MLA_HEADS * QK_PAD), BF16),
        jax.ShapeDtypeStruct((T, kv_cols), BF16),
        jax.ShapeDtypeStruct((T, KV_LORA), F32),
        jax.ShapeDtypeStruct((T, QK_ROPE), F32),
        jax.ShapeDtypeStruct((T, LANES), BF16),
        jax.ShapeDtypeStruct((T, D_SSM), BF16),
        jax.ShapeDtypeStruct((T, CONV_CH), F32),
        jax.ShapeDtypeStruct((T, LANES), F32),
    ]
    out_specs = [pl.BlockSpec((tm, s.shape[1]), row) for s in out_shape]
    in_specs = [
        pl.BlockSpec((tm, D_MODEL), row),
        _const_spec((1, D_MODEL)), _const_spec((1, D_MODEL)),
        _const_spec((D_MODEL, D_IN_PAD)),
        _const_spec((1, Q_LORA)), _const_spec((1, KV_LORA)),
        _const_spec((Q_LORA, MLA_HEADS * QK_PAD)),
        _const_spec((KV_LORA, 2 * D_ATTN)),
        pl.BlockSpec((tm, LANES), pos), pl.BlockSpec((tm, LANES), pos),
        _const_spec((1, LANES)),
    ]
    return pl.pallas_call(
        functools.partial(_inproj_kernel, with_kv=with_kv),
        grid=(n,), in_specs=in_specs, out_specs=out_specs, out_shape=out_shape,
        compiler_params=pltpu.CompilerParams(dimension_semantics=("arbitrary",),
                                             vmem_limit_bytes=VMEM_LIMIT),
        name="inproj",
    )(x, pw["ln0_g"], pw["ln0_b"], pw["w_in"], pw["q_norm"], pw["kv_norm"], pw["w_uq"],
      pw["w_ukv"], cos_t, sin_t, pw["dt_bias"])


def _attn_prompt_kernel(qi_ref, ki_ref, q_ref, kv_ref, krp_ref, o_ref, *scratch, tq, tk):
    m_refs = scratch[0:MLA_HEADS]
    acc_refs = scratch[MLA_HEADS:2 * MLA_HEADS]
    p_id = pl.program_id(1)
    qi = qi_ref[p_id]
    ki = ki_ref[p_id]
    k_last = ((qi + 1) * tq - 1) // tk

    @pl.when(ki == 0)
    def _():
        for h in range(MLA_HEADS):
            m_refs[h][...] = jnp.full(m_refs[h].shape, -jnp.inf, F32)
            acc_refs[h][...] = jnp.zeros(acc_refs[h].shape, F32)

    def step(masked):
        krp = krp_ref[...]
        ones_col = (lax.broadcasted_iota(jnp.int32, (tk, LANES), 1) == 0).astype(BF16)
        if masked:
            r = (qi * tq + lax.broadcasted_iota(jnp.int32, (tq, tk), 0)) // CHUNK
            c = (ki * tk + lax.broadcasted_iota(jnp.int32, (tq, tk), 1)) // CHUNK
            allowed = c <= r
        for h in range(MLA_HEADS):
            qh = q_ref[:, h * QK_PAD:(h + 1) * QK_PAD]
            kh = jnp.concatenate([kv_ref[:, h * QK_NOPE:(h + 1) * QK_NOPE], krp], axis=1)
            s = _dot_nt(qh, kh)
            if masked:
                s = jnp.where(allowed, s, -jnp.inf)
            m_old = m_refs[h][...]
            m_new = jnp.maximum(m_old, jnp.max(s, axis=1, keepdims=True))
            alpha = jnp.exp2(m_old - m_new)
            p = jnp.exp2(s - m_new).astype(BF16)
            v_aug = jnp.concatenate([kv_ref[:, D_ATTN + h * V_DIM:D_ATTN + (h + 1) * V_DIM], ones_col], axis=1)
            acc_refs[h][...] = alpha * acc_refs[h][...] + _dot(p, v_aug)
            m_refs[h][...] = m_new

    @pl.when(ki < k_last)
    def _():
        step(False)

    @pl.when(ki == k_last)
    def _():
        step(True)
        for h in range(MLA_HEADS):
            acc = acc_refs[h][...]
            o_ref[:, h * V_DIM:(h + 1) * V_DIM] = (acc[:, :V_DIM] / acc[:, V_DIM:V_DIM + 1]).astype(BF16)


def _attn_prompt(q, kv, krp, *, tq, tk):
    B, L, _ = q.shape
    nq = L // tq
    pairs = [(i, j) for i in range(nq) for j in range(((i + 1) * tq - 1) // tk + 1)]
    qi_tab = jnp.asarray(np.array([p[0] for p in pairs], np.int32))
    ki_tab = jnp.asarray(np.array([p[1] for p in pairs], np.int32))
    grid_spec = pltpu.PrefetchScalarGridSpec(
        num_scalar_prefetch=2,
        grid=(B, len(pairs)),
        in_specs=[
            pl.BlockSpec((None, tq, MLA_HEADS * QK_PAD), lambda b, p, qi, ki: (b, qi[p], 0)),
            pl.BlockSpec((None, tk, 2 * D_ATTN), lambda b, p, qi, ki: (b, ki[p], 0)),
            pl.BlockSpec((None, tk, LANES), lambda b, p, qi, ki: (b, ki[p], 0)),
        ],
        out_specs=pl.BlockSpec((None, tq, D_ATTN), lambda b, p, qi, ki: (b, qi[p], 0)),
        scratch_shapes=([pltpu.VMEM((tq, 1), F32)] * MLA_HEADS
                        + [pltpu.VMEM((tq, 2 * V_DIM), F32)] * MLA_HEADS),
    )
    return pl.pallas_call(
        functools.partial(_attn_prompt_kernel, tq=tq, tk=tk),
        grid_spec=grid_spec,
        out_shape=jax.ShapeDtypeStruct((B, L, D_ATTN), BF16),
        compiler_params=pltpu.CompilerParams(dimension_semantics=("arbitrary", "arbitrary"),
                                             vmem_limit_bytes=VMEM_LIMIT),
        name="attn_prompt",
    )(qi_tab, ki_tab, q, kv, krp)


def _attn_sample_kernel(q_ref, wukt_ref, wuv_ref, latp_ref, krpast_ref, latn_ref, krn_ref, o_ref,
                        qlat_ref, qrp_ref, m_ref, l_ref, acc_ref, *, ls, past, tk, n_new_pad):
    k = pl.program_id(1)
    nk = pl.num_programs(1)
    rows = MLA_HEADS * ls

    @pl.when(k == 0)
    def _():
        for h in range(MLA_HEADS):
            qn = q_ref[:, h * QK_PAD:h * QK_PAD + LANES]
            qlat_ref[h * ls:(h + 1) * ls, :] = _dot(qn, wukt_ref[h]).astype(BF16)
            qrp_ref[h * ls:(h + 1) * ls, :] = q_ref[:, h * QK_PAD + LANES:(h + 1) * QK_PAD]
        m_ref[...] = jnp.full(m_ref.shape, -jnp.inf, F32)
        l_ref[...] = jnp.zeros(l_ref.shape, F32)
        acc_ref[...] = jnp.zeros(acc_ref.shape, F32)

    q_chunk = (past + lax.broadcasted_iota(jnp.int32, (rows, 1), 0) % ls) // CHUNK

    def update(s, lat_b):
        m_old = m_ref[...]
        m_new = jnp.maximum(m_old, jnp.max(s, axis=1, keepdims=True))
        alpha = jnp.exp2(m_old - m_new)
        p = jnp.exp2(s - m_new)
        l_ref[...] = alpha * l_ref[...] + jnp.sum(p, axis=1, keepdims=True)
        acc_ref[...] = alpha * acc_ref[...] + _dot(p.astype(BF16), lat_b)
        m_ref[...] = m_new

    lat_b = latp_ref[...].astype(BF16)
    kr_b = krpast_ref[...].astype(BF16)
    s = _dot_nt(qlat_ref[...], lat_b) + _dot_nt(qrp_ref[:, :QK_ROPE], kr_b)
    k_pos = k * tk + lax.broadcasted_iota(jnp.int32, (rows, tk), 1)
    s = jnp.where(k_pos // CHUNK <= q_chunk, s, -jnp.inf)
    update(s, lat_b)

    @pl.when(k == nk - 1)
    def _():
        latn_b = latn_ref[...].astype(BF16)
        s2 = _dot_nt(qlat_ref[...], latn_b) + _dot_nt(qrp_ref[...], krn_ref[...])
        j = lax.broadcasted_iota(jnp.int32, (rows, n_new_pad), 1)
        ok = ((past + j) // CHUNK <= q_chunk) & (j < ls)
        update(jnp.where(ok, s2, -jnp.inf), latn_b)
        o_lat = (acc_ref[...] / l_ref[...]).astype(BF16)
        for h in range(MLA_HEADS):
            o_ref[:, h * V_DIM:(h + 1) * V_DIM] = _dot(o_lat[h * ls:(h + 1) * ls, :], wuv_ref[h]).astype(BF16)


def _attn_sample(q, pw, lat_past, kr_past, lat_new, krp_new, *, tk):
    Bs, Ls, _ = q.shape
    past = lat_past.shape[1]
    n_new_pad = LANES
    lat_new = jnp.pad(lat_new, ((0, 0), (0, n_new_pad - Ls), (0, 0)))
    krp_new = jnp.pad(krp_new, ((0, 0), (0, n_new_pad - Ls), (0, 0)))
    rows = MLA_HEADS * Ls
    return pl.pallas_call(
        functools.partial(_attn_sample_kernel, ls=Ls, past=past, tk=tk, n_new_pad=n_new_pad),
        grid=(Bs, past // tk),
        in_specs=[
            pl.BlockSpec((None, Ls, MLA_HEADS * QK_PAD), lambda b, k: (b, 0, 0)),
            _const_spec((MLA_HEADS, QK_NOPE, KV_LORA)),
            _const_spec((MLA_HEADS, KV_LORA, V_DIM)),
            pl.BlockSpec((None, tk, KV_LORA), lambda b, k: (b, k, 0)),
            pl.BlockSpec((None, tk, QK_ROPE), lambda b, k: (b, k, 0)),
            pl.BlockSpec((None, n_new_pad, KV_LORA), lambda b, k: (b, 0, 0)),
            pl.BlockSpec((None, n_new_pad, LANES), lambda b, k: (b, 0, 0)),
        ],
        out_specs=pl.BlockSpec((None, Ls, D_ATTN), lambda b, k: (b, 0, 0)),
        out_shape=jax.ShapeDtypeStruct((Bs, Ls, D_ATTN), BF16),
        scratch_shapes=[pltpu.VMEM((rows, KV_LORA), BF16), pltpu.VMEM((rows, LANES), BF16),
                        pltpu.VMEM((rows, 1), F32), pltpu.VMEM((rows, 1), F32),
                        pltpu.VMEM((rows, KV_LORA), F32)],
        compiler_params=pltpu.CompilerParams(dimension_semantics=("arbitrary", "arbitrary"),
                                             vmem_limit_bytes=VMEM_LIMIT),
        name="attn_sample",
    )(q, pw["w_ukt"], pw["w_uvh"], lat_past, kr_past, lat_new, krp_new)


def _ssd_consts(q):
    hb = MXU_DIM // q
    hq = SSM_HEADS * q
    lane = np.arange(hq)
    ltri = (np.arange(q)[None, :] <= np.arange(q)[:, None]).astype(np.float32)
    sel_c = (np.arange(LANES)[:, None] == (lane // q)[None, :]).astype(np.float32)
    sel_p = (np.arange(LANES)[:, None] == (np.arange(D_SSM) // SSM_HEAD_DIM)[None, :]).astype(np.float32)
    diag = (np.arange(q)[:, None] == (lane % q)[None, :]).astype(np.float32)
    causal = (np.arange(q)[:, None] >= (lane % q)[None, :]).astype(np.float32)
    bd_rows = np.arange(hb * q) // q
    bd_cols = np.arange(hb * SSM_HEAD_DIM) // SSM_HEAD_DIM
    bdmask = (bd_rows[:, None] == bd_cols[None, :]).astype(np.float32)
    return dict(ltri=jnp.asarray(ltri, BF16), sel_c=jnp.asarray(sel_c, BF16),
                sel_p=jnp.asarray(sel_p, BF16), diag=jnp.asarray(diag, F32),
                causal=jnp.asarray(causal, F32), bdmask=jnp.asarray(bdmask, BF16))


def _ssd_kernel(xbc_ref, dt_ref, z_ref, cprev_ref, h0_ref, cw_ref, cb_ref, a_ref, dskip_ref, norm_ref,
                ltri_ref, selc_ref, selp_ref, diag_ref, causal_ref, bdmask_ref,
                y_ref, cout_ref, hout_ref,
                ext_ref, u_ref, ht_ref, *, q, tt):
    t = pl.program_id(1)
    nt = pl.num_programs(1)
    hb = MXU_DIM // q
    nblk = SSM_HEADS // hb
    gw = HEADS_PER_GROUP * SSM_HEAD_DIM
    pad = 8

    @pl.when(t == 0)
    def _():
        ext_ref[0:pad, :] = jnp.zeros((pad, CONV_CH), F32)
        ext_ref[pad - (CONV_K - 1):pad, :] = cprev_ref[...]
        ht_ref[...] = h0_ref[...].T

    xbc = xbc_ref[...]
    ext_ref[pad:pad + tt, :] = xbc
    cout_ref[...] = xbc[tt - (CONV_K - 1):, :]
    conv = cb_ref[...] + cw_ref[0:1, :] * ext_ref[pad - 3:pad - 3 + tt, :]
    for k in range(1, CONV_K):
        conv = conv + cw_ref[k:k + 1, :] * ext_ref[pad - 3 + k:pad - 3 + k + tt, :]
    u_ref[...] = _silu(conv)
    ext_ref[pad - (CONV_K - 1):pad, :] = xbc[tt - (CONV_K - 1):, :]

    a_row = a_ref[...]
    d_row = dskip_ref[...]
    norm_row = norm_ref[...]

    def chunk(c, carry):
        r0 = pl.multiple_of(c * q, q)
        u = u_ref[pl.ds(r0, q), :]
        xs = u[:, :D_SSM]
        dt = dt_ref[pl.ds(r0, q), :]
        a_cs = _exact_dot_l(ltri_ref[...], dt * a_row)
        ap = _exact_dot(a_cs, selp_ref[...])
        dtp = _exact_dot(dt, selp_ref[...])
        a_i = ap if q == SSM_HEAD_DIM else _exact_dot(a_cs, selc_ref[...])
        a_j = jnp.sum(a_i * diag_ref[...], axis=0, keepdims=True)
        decay = jnp.exp(jnp.where(causal_ref[...] > 0.5, a_i - a_j, -jnp.inf))
        ap_last = ap[q - 1:q, :]
        xdt = (xs * dtp).astype(BF16)
        xw = (xs * (jnp.exp(ap_last - ap) * dtp)).astype(BF16)
        e_ap = jnp.exp(ap)

        cb_parts, y_off_parts = [], []
        for g in range(SSM_GROUPS):
            bm = u[:, D_SSM + g * D_STATE:D_SSM + (g + 1) * D_STATE].astype(BF16)
            cm = u[:, D_SSM + (SSM_GROUPS + g) * D_STATE:D_SSM + (SSM_GROUPS + g + 1) * D_STATE].astype(BF16)
            cb_parts.append(_dot_nt(cm, jnp.concatenate([bm] * HEADS_PER_GROUP, axis=0)))
            gl = slice(g * gw, (g + 1) * gw)
            ht_g = ht_ref[:, gl]
            y_off_parts.append(_dot(cm, ht_g.astype(BF16)) * e_ap[:, gl])
            st = _dot_tn(bm, xw[:, gl])
            ht_ref[:, gl] = ht_g * jnp.exp(ap_last[:, gl]) + st
        w_all = (jnp.concatenate(cb_parts, axis=1) * decay).astype(BF16)
        y_diag_parts = []
        for b in range(nblk):
            x_b = xdt[:, b * hb * SSM_HEAD_DIM:(b + 1) * hb * SSM_HEAD_DIM]
            bd = jnp.concatenate([x_b] * hb, axis=0) * bdmask_ref[...]
            y_diag_parts.append(_dot(w_all[:, b * MXU_DIM:(b + 1) * MXU_DIM], bd))
        y = jnp.concatenate(y_diag_parts, axis=1) + jnp.concatenate(y_off_parts, axis=1)

        y = y + d_row * xs
        y = y * _silu(z_ref[pl.ds(r0, q), :].astype(F32))
        outs = []
        for g in range(SSM_GROUPS):
            yg = y[:, g * gw:(g + 1) * gw]
            outs.append(yg * lax.rsqrt(jnp.mean(yg * yg, axis=1, keepdims=True) + RMS_EPS))
        y_ref[pl.ds(r0, q), :] = (jnp.concatenate(outs, axis=1) * norm_row).astype(BF16)
        return carry

    lax.fori_loop(0, tt // q, chunk, 0, unroll=True)

    @pl.when(t == nt - 1)
    def _():
        hout_ref[...] = ht_ref[...].T


def _ssd(xbc, dt, z, conv_prev, h0, pw, *, q, tt):
    B, L, _ = xbc.shape
    cs = _ssd_consts(q)
    hq = SSM_HEADS * q
    hb = MXU_DIM // q
    tile = lambda w: pl.BlockSpec((None, tt, w), lambda b, t: (b, t, 0))
    per_b = lambda s: pl.BlockSpec((None,) + s, lambda b, t: (b, 0, 0))
    return pl.pallas_call(
        functools.partial(_ssd_kernel, q=q, tt=tt),
        grid=(B, L // tt),
        in_specs=[tile(CONV_CH), tile(LANES), tile(D_SSM), per_b((CONV_K - 1, CONV_CH)),
                  per_b((D_SSM, D_STATE)),
                  _const_spec((CONV_K, CONV_CH)), _const_spec((1, CONV_CH)), _const_spec((1, LANES)),
                  _const_spec((1, D_SSM)), _const_spec((1, D_SSM)),
                  _const_spec((q, q)), _const_spec((LANES, hq)), _const_spec((LANES, D_SSM)),
                  _const_spec((q, hq)), _const_spec((q, hq)),
                  _const_spec((MXU_DIM, hb * SSM_HEAD_DIM))],
        out_specs=[tile(D_SSM), per_b((CONV_K - 1, CONV_CH)), per_b((D_SSM, D_STATE))],
        out_shape=[jax.ShapeDtypeStruct((B, L, D_SSM), BF16),
                   jax.ShapeDtypeStruct((B, CONV_K - 1, CONV_CH), F32),
                   jax.ShapeDtypeStruct((B, D_SSM, D_STATE), F32)],
        scratch_shapes=[pltpu.VMEM((8 + tt, CONV_CH), F32), pltpu.VMEM((tt, CONV_CH), F32),
                        pltpu.VMEM((D_STATE, D_SSM), F32)],
        compiler_params=pltpu.CompilerParams(dimension_semantics=("arbitrary", "arbitrary"),
                                             vmem_limit_bytes=VMEM_LIMIT),
        name="ssd",
    )(xbc, dt, z, conv_prev, h0, pw["conv_w"], pw["conv_b"], pw["a_neg"], pw["d_skip"], pw["ssm_norm"],
      cs["ltri"], cs["sel_c"], cs["sel_p"], cs["diag"], cs["causal"], cs["bdmask"])


R_OFF = N_EXPERT_GROUPS


def _outproj_kernel(xp_ref, xs_ref, attnp_ref, attns_ref, yp_ref, ys_ref, g0_ref, b0_ref, wo_a_ref, wo_s_ref,
                    g1_ref, b1_ref, wr_ref, br_ref, ltri_ref, h_ref, hpk_ref, route_ref, cnt_ref,
                    *, n_prompt_tiles, tm, n_groups):
    i = pl.program_id(0)
    is_p = i < n_prompt_tiles

    @pl.when(i == 0)
    def _():
        cnt_ref[...] = jnp.zeros(cnt_ref.shape, F32)

    cnt = cnt_ref[0:1, :]
    for g in range(n_groups):
        rows = slice(g * (tm // n_groups), (g + 1) * (tm // n_groups))
        refs = (xp_ref, xs_ref, attnp_ref, attns_ref, yp_ref, ys_ref)
        cnt = _outproj_rows(rows, is_p, cnt, refs, g0_ref, b0_ref, wo_a_ref, wo_s_ref, g1_ref, b1_ref,
                            wr_ref, br_ref, ltri_ref, h_ref, hpk_ref, route_ref)
    cnt_ref[...] = jnp.broadcast_to(cnt, cnt_ref.shape)


def _outproj_rows(rows, is_p, cnt, refs, g0_ref, b0_ref, wo_a_ref, wo_s_ref, g1_ref, b1_ref, wr_ref, br_ref,
                  ltri_ref, h_ref, hpk_ref, route_ref):
    xp_ref, xs_ref, attnp_ref, attns_ref, yp_ref, ys_ref = refs
    x = jnp.where(is_p, xp_ref[rows, :], xs_ref[rows, :])
    attn = jnp.where(is_p, attnp_ref[rows, :], attns_ref[rows, :])
    yssm = jnp.where(is_p, yp_ref[rows, :], ys_ref[rows, :])

    xn = _layernorm(x, g0_ref[...], b0_ref[...])
    mixed = _dot(attn, wo_a_ref[...]) + _dot(yssm, wo_s_ref[...])
    h = _layernorm(ALPHA * xn + mixed, g1_ref[...], b1_ref[...])
    h_ref[rows, :] = h

    h_hi = h.astype(BF16)
    h_hi32 = h_hi.astype(F32)
    hpk_ref[rows, :] = _pack_bf16_pairs(h_hi32)
    h_lo = (h - h_hi32).astype(BF16)
    tm = h.shape[0]
    prod = _dot(jnp.concatenate([h_hi, h_lo], axis=0), wr_ref[...])
    lg = (prod[:tm, :LANES] + prod[:tm, LANES:]) + (prod[tm:, :LANES] + prod[tm:, LANES:]) + br_ref[...]
    lane = lax.broadcasted_iota(jnp.int32, lg.shape, 1)
    big = jnp.int32(1 << 20)
    gl = jnp.where(lane < N_EXPERT_GROUPS, lg, -jnp.inf)
    gmax = jnp.max(gl, axis=1, keepdims=True)
    grp = jnp.min(jnp.where(gl == gmax, lane, big), axis=1, keepdims=True)
    g_w = 1.0 / jnp.sum(jnp.exp(gl - gmax), axis=1, keepdims=True)
    in_grp = (lane >= R_OFF) & (lane < R_OFF + N_EXPERTS) & ((lane - R_OFF) // EXPERTS_PER_GROUP == grp)
    el = jnp.where(in_grp, lg, -jnp.inf)
    emax = jnp.max(el, axis=1, keepdims=True)
    ee = jnp.exp(el - emax)
    prob = jnp.where(in_grp, ee / jnp.sum(ee, axis=1, keepdims=True), -1.0)
    p1 = jnp.max(prob, axis=1, keepdims=True)
    i1 = jnp.min(jnp.where(prob == p1, lane, big), axis=1, keepdims=True)
    prob2 = jnp.where(lane == i1, -1.0, prob)
    p2 = jnp.max(prob2, axis=1, keepdims=True)
    i2 = jnp.min(jnp.where(prob2 == p2, lane, big), axis=1, keepdims=True)
    denom = p1 + p2
    oh1 = (lane == i1).astype(F32)
    oh2 = (lane == i2).astype(F32)
    oh = oh1 + oh2
    before = _dot(ltri_ref[...], oh.astype(BF16)) + cnt
    rank1 = jnp.sum(before * oh1, axis=1, keepdims=True)
    rank2 = jnp.sum(before * oh2, axis=1, keepdims=True)
    route = jnp.where(lane == 0, (i1 - R_OFF).astype(F32),
                      jnp.where(lane == 1, (i2 - R_OFF).astype(F32),
                                jnp.where(lane == 2, g_w * p1 / denom,
                                          jnp.where(lane == 3, g_w * p2 / denom,
                                                    jnp.where(lane == 4, rank1,
                                                              jnp.where(lane == 5, rank2, 0.0))))))
    route_ref[rows, :] = route
    return cnt + jnp.sum(oh, axis=0, keepdims=True)


def _outproj(xp, xs, attn_p, attn_s, y_p, y_s, pw, *, tm):
    Tp, Ts = xp.shape[0], xs.shape[0]
    npt, nst = Tp // tm, Ts // tm
    T = Tp + Ts
    row = lambda i: (i, 0)
    prow = lambda i: (jnp.minimum(i, npt - 1), 0)
    srow = lambda i: (jnp.maximum(i - npt, 0), 0)
    n_groups = 1
    gm = tm // n_groups
    ltri = jnp.asarray(np.tril(np.ones((gm, gm), np.float32), -1), BF16)
    return pl.pallas_call(
        functools.partial(_outproj_kernel, n_prompt_tiles=npt, tm=tm, n_groups=n_groups),
        grid=(npt + nst,),
        in_specs=[pl.BlockSpec((tm, D_MODEL), prow), pl.BlockSpec((tm, D_MODEL), srow),
                  pl.BlockSpec((tm, D_ATTN), prow), pl.BlockSpec((tm, D_ATTN), srow),
                  pl.BlockSpec((tm, D_SSM), prow), pl.BlockSpec((tm, D_SSM), srow),
                  _const_spec((1, D_MODEL)), _const_spec((1, D_MODEL)),
                  _const_spec((D_ATTN, D_MODEL)), _const_spec((D_SSM, D_MODEL)),
                  _const_spec((1, D_MODEL)), _const_spec((1, D_MODEL)),
                  _const_spec((D_MODEL, 2 * LANES)), _const_spec((1, LANES)),
                  _const_spec((gm, gm))],
        out_specs=[pl.BlockSpec((tm, D_MODEL), row), pl.BlockSpec((tm, D_MODEL // 2), row),
                   pl.BlockSpec((tm, LANES), row), pl.BlockSpec((8, LANES), lambda i: (0, 0))],
        out_shape=[jax.ShapeDtypeStruct((T, D_MODEL), F32), jax.ShapeDtypeStruct((T, D_MODEL // 2), jnp.uint32),
                   jax.ShapeDtypeStruct((T, LANES), F32), jax.ShapeDtypeStruct((8, LANES), F32)],
        compiler_params=pltpu.CompilerParams(dimension_semantics=("arbitrary",),
                                             vmem_limit_bytes=VMEM_LIMIT),
        name="outproj",
    )(xp, xs, attn_p, attn_s, y_p, y_s, pw["ln0_g"], pw["ln0_b"], pw["w_o_a"], pw["w_o_s"],
      pw["ln1_g"], pw["ln1_b"], pw["w_r"], pw["b_r"], ltri)


def _experts_kernel(blk_e_ref, n_used_ref, next_e_ref, tok_ref, tok_n1_ref, tok_n2_ref, h_hbm,
                    wg_hbm, wu_hbm, wd_hbm, y_ref, xbuf, sem, wg_s, wu_s, wd_s, wsem, wg_b, wu_b, wd_b, prev_e,
                    *, rb):
    i = pl.program_id(0)
    n_used = n_used_ref[0]
    slot = i % GATHER_SLOTS

    def weight_copies(e):
        return (pltpu.make_async_copy(wg_hbm.at[e], wg_s, wsem.at[0]),
                pltpu.make_async_copy(wu_hbm.at[e], wu_s, wsem.at[1]),
                pltpu.make_async_copy(wd_hbm.at[e], wd_s, wsem.at[2]))

    def row_copy(tok, dst_slot, r):
        return pltpu.make_async_copy(h_hbm.at[pl.ds(tok, 1)], xbuf.at[dst_slot, pl.ds(r, 1)],
                                     sem.at[dst_slot])

    def wait_block(s):
        pltpu.make_async_copy(h_hbm.at[pl.ds(0, rb)], xbuf.at[s], sem.at[s]).wait()

    @pl.when(i == 0)
    def _():
        prev_e[0] = -1
        for c in weight_copies(blk_e_ref[0]):
            c.start()

        def body(r, c):
            row_copy(tok_ref[r], 0, r).start()
            row_copy(tok_n1_ref[r], 1, r).start()
            return c
        lax.fori_loop(0, rb, body, 0)

    @pl.when(i < n_used)
    def _():
        wait_block(slot)
        e = blk_e_ref[i]

        @pl.when(e != prev_e[0])
        def _():
            for c in weight_copies(e):
                c.wait()
            wg_b[...] = wg_s[...].astype(BF16)
            wu_b[...] = wu_s[...].astype(BF16)
            wd_b[...] = wd_s[...].astype(BF16)
            prev_e[0] = e
            nxt = next_e_ref[i]

            @pl.when(nxt >= 0)
            def _():
                for c in weight_copies(nxt):
                    c.start()

        xb = _unpack_bf16_pairs(xbuf[slot]).astype(BF16)
        ahead = (i + GATHER_AHEAD) % GATHER_SLOTS
        for r in range(rb):
            row_copy(tok_n2_ref[r], ahead, r).start(priority=r % 2)
        hid = (_silu(_dot(xb, wg_b[...])) * _dot(xb, wu_b[...])).astype(BF16)
        y = _dot(hid, wd_b[...])
        y_ref[...] = _pack_bf16_pairs(y.astype(BF16).astype(F32))

        @pl.when(i == n_used - 1)
        def _():
            wait_block((i + 1) % GATHER_SLOTS)
            wait_block(ahead)

    @pl.when(i >= n_used)
    def _():
        y_ref[...] = jnp.zeros(y_ref.shape, jnp.uint32)


def _experts(hpk_all, row_tok, blk_e, n_used, next_e, w_gate, w_up, w_down, *, rb):
    n_blk = blk_e.shape[0]
    n_rows = n_blk * rb
    blk = lambda ahead: pl.BlockSpec(
        (rb,), lambda i, be, nu, ne: (jnp.minimum(i + ahead, nu[0] - 1),), memory_space=pltpu.SMEM)
    hbm = pl.BlockSpec(memory_space=pl.ANY)
    grid_spec = pltpu.PrefetchScalarGridSpec(
        num_scalar_prefetch=3,
        grid=(n_blk,),
        in_specs=[blk(0), blk(1), blk(GATHER_AHEAD), hbm, hbm, hbm, hbm],
        out_specs=pl.BlockSpec((rb, D_MODEL // 2), lambda i, be, nu, ne: (i, 0)),
        scratch_shapes=[pltpu.VMEM((GATHER_SLOTS, rb, D_MODEL // 2), jnp.uint32),
                        pltpu.SemaphoreType.DMA((GATHER_SLOTS,)),
                        pltpu.VMEM((D_MODEL, D_EXPERT), F32), pltpu.VMEM((D_MODEL, D_EXPERT), F32),
                        pltpu.VMEM((D_EXPERT, D_MODEL), F32), pltpu.SemaphoreType.DMA((3,)),
                        pltpu.VMEM((D_MODEL, D_EXPERT), BF16), pltpu.VMEM((D_MODEL, D_EXPERT), BF16),
                        pltpu.VMEM((D_EXPERT, D_MODEL), BF16), pltpu.SMEM((1,), jnp.int32)],
    )
    return pl.pallas_call(
        functools.partial(_experts_kernel, rb=rb),
        grid_spec=grid_spec,
        out_shape=jax.ShapeDtypeStruct((n_rows, D_MODEL // 2), jnp.uint32),
        compiler_params=pltpu.CompilerParams(dimension_semantics=("arbitrary",),
                                             vmem_limit_bytes=VMEM_LIMIT),
        name="experts",
    )(blk_e, n_used, next_e, row_tok, row_tok, row_tok, hpk_all, w_gate, w_up, w_down)


def _combine_kernel(pos_ref, pos_n1_ref, pos_n2_ref, y_hbm, h_ref, route_ref, g2_ref, b2_ref, op_ref, os_ref,
                    ybuf, sem, *, tm, n_prompt_tiles):
    i = pl.program_id(0)
    n = pl.num_programs(0)
    slot = i % GATHER_SLOTS
    rows = TOP_K * tm

    def row_copy(idx, dst_slot, r):
        return pltpu.make_async_copy(y_hbm.at[pl.ds(idx, 1)], ybuf.at[dst_slot, pl.ds(r, 1)],
                                     sem.at[dst_slot])

    def wait_tile(s):
        pltpu.make_async_copy(y_hbm.at[pl.ds(0, rows)], ybuf.at[s], sem.at[s]).wait()

    @pl.when(i == 0)
    def _():
        def body(r, c):
            row_copy(pos_ref[r], 0, r).start()
            row_copy(pos_n1_ref[r], 1, r).start()
            return c
        lax.fori_loop(0, rows, body, 0)

    wait_tile(slot)
    gate1 = route_ref[:, 2:3]
    gate2 = route_ref[:, 3:4]
    f = (_unpack_bf16_pairs(ybuf[slot, 0:tm, :]) * gate1
         + _unpack_bf16_pairs(ybuf[slot, tm:rows, :]) * gate2)
    ahead = (i + GATHER_AHEAD) % GATHER_SLOTS
    for r in range(rows):
        row_copy(pos_n2_ref[r], ahead, r).start(priority=r % 2)
    out = _layernorm(ALPHA * h_ref[...] + f, g2_ref[...], b2_ref[...])

    @pl.when(i < n_prompt_tiles)
    def _():
        op_ref[...] = out

    @pl.when(i >= n_prompt_tiles)
    def _():
        os_ref[...] = out

    @pl.when(i == n - 1)
    def _():
        wait_tile((i + 1) % GATHER_SLOTS)
        wait_tile(ahead)


def _combine(y_rows, pos_tiles, h_all, route, pw, *, tm, t_prompt):
    T = h_all.shape[0]
    n = T // tm
    npt = t_prompt // tm
    rows = TOP_K * tm
    return pl.pallas_call(
        functools.partial(_combine_kernel, tm=tm, n_prompt_tiles=npt),
        grid=(n,),
        in_specs=[
            pl.BlockSpec((rows,), lambda i: (i,), memory_space=pltpu.SMEM),
            pl.BlockSpec((rows,), lambda i: (jnp.minimum(i + 1, n - 1),), memory_space=pltpu.SMEM),
            pl.BlockSpec((rows,), lambda i: (jnp.minimum(i + GATHER_AHEAD, n - 1),), memory_space=pltpu.SMEM),
            pl.BlockSpec(memory_space=pl.ANY),
            pl.BlockSpec((tm, D_MODEL), lambda i: (i, 0)),
            pl.BlockSpec((tm, LANES), lambda i: (i, 0)),
            _const_spec((1, D_MODEL)), _const_spec((1, D_MODEL)),
        ],
        out_specs=[pl.BlockSpec((tm, D_MODEL), lambda i: (jnp.minimum(i, npt - 1), 0)),
                   pl.BlockSpec((tm, D_MODEL), lambda i: (jnp.maximum(i - npt, 0), 0))],
        out_shape=[jax.ShapeDtypeStruct((t_prompt, D_MODEL), F32),
                   jax.ShapeDtypeStruct((T - t_prompt, D_MODEL), F32)],
        scratch_shapes=[pltpu.VMEM((GATHER_SLOTS, rows, D_MODEL // 2), jnp.uint32),
                        pltpu.SemaphoreType.DMA((GATHER_SLOTS,))],
        compiler_params=pltpu.CompilerParams(dimension_semantics=("arbitrary",),
                                             vmem_limit_bytes=VMEM_LIMIT),
        name="combine",
    )(pos_tiles, pos_tiles, pos_tiles, y_rows, h_all, route, pw["ln2_g"], pw["ln2_b"])


def _prep_weights(ln0_g, ln0_b, w_in, q_norm, w_uq, kv_norm, w_uk, w_uv, conv_w, conv_b, dt_bias, a_log,
                  d_skip, ssm_norm, w_o, ln1_g, ln1_b, w_rg, b_rg, w_re, b_re, ln2_g, ln2_b):
    s_q, s_kv, s_kr, s_z, s_xbc = Q_LORA, Q_LORA + KV_LORA, Q_LORA + KV_LORA + QK_ROPE, \
        Q_LORA + KV_LORA + QK_ROPE + D_SSM, Q_LORA + KV_LORA + QK_ROPE + D_SSM + CONV_CH
    zc = lambda n: jnp.zeros((D_MODEL, n), F32)
    w_in_p = jnp.concatenate([
        w_in[:, :s_q], w_in[:, s_q:s_kv], w_in[:, s_kr:s_z], w_in[:, s_z:s_xbc],
        w_in[:, s_kv:s_kr], zc(LANES - QK_ROPE), w_in[:, s_xbc:], zc(LANES - SSM_HEADS)], axis=1)
    wq = w_uq.reshape(Q_LORA, MLA_HEADS, QK_DIM) * (ATTN_SCALE * math.log2(math.e))
    wq = jnp.concatenate([wq, jnp.zeros((Q_LORA, MLA_HEADS, QK_PAD - QK_DIM), F32)], axis=2)
    w_r = jnp.concatenate([w_rg, w_re, jnp.zeros((D_MODEL, LANES - R_OFF - N_EXPERTS), F32)], axis=1)
    w_r_hi = w_r.astype(BF16)
    row = lambda v: v.reshape(1, -1)
    pad_row = lambda v: jnp.pad(v, (0, LANES - v.shape[0])).reshape(1, LANES)
    return dict(
        ln0_g=row(ln0_g), ln0_b=row(ln0_b), w_in=w_in_p.astype(BF16),
        q_norm=row(q_norm), kv_norm=row(kv_norm),
        w_uq=wq.reshape(Q_LORA, MLA_HEADS * QK_PAD).astype(BF16),
        w_ukv=jnp.concatenate([w_uk.reshape(KV_LORA, D_ATTN), w_uv.reshape(KV_LORA, D_ATTN)],
                              axis=1).astype(BF16),
        w_ukt=jnp.transpose(w_uk, (1, 2, 0)).astype(BF16),
        w_uvh=jnp.transpose(w_uv, (1, 0, 2)).astype(BF16),
        conv_w=conv_w, conv_b=row(conv_b), dt_bias=pad_row(dt_bias),
        a_neg=pad_row(-jnp.exp(a_log)), d_skip=row(jnp.repeat(d_skip, SSM_HEAD_DIM)),
        ssm_norm=row(ssm_norm),
        w_o_a=w_o[:D_ATTN].astype(BF16), w_o_s=w_o[D_ATTN:].astype(BF16),
        ln1_g=row(ln1_g), ln1_b=row(ln1_b),
        w_r=jnp.concatenate([w_r_hi, (w_r - w_r_hi.astype(F32)).astype(BF16)], axis=1),
        b_r=pad_row(jnp.concatenate([b_rg, b_re])),
        ln2_g=row(ln2_g), ln2_b=row(ln2_b),
    )


def _rope_tables(pos):
    half = QK_ROPE // 2
    inv_freq = ROPE_THETA ** (-jnp.arange(half, dtype=F32) / half)
    ang = pos.astype(F32)[:, None] * inv_freq[None, :]
    cos, sin = jnp.cos(ang), jnp.sin(ang)
    zeros = jnp.zeros((pos.shape[0], LANES - QK_ROPE), F32)
    return (jnp.concatenate([cos, cos, zeros], axis=1), jnp.concatenate([-sin, sin, zeros], axis=1))


def _dispatch_tables(route, counts, *, rb, n_blk):
    T = route.shape[0]
    experts = jnp.arange(N_EXPERTS, dtype=jnp.int32)
    e = route[:, 0:TOP_K].astype(jnp.int32)
    rank = route[:, 4:4 + TOP_K].astype(jnp.int32)
    counts = counts.astype(jnp.int32)
    pcounts = (counts + rb - 1) // rb * rb
    pend = jnp.cumsum(pcounts)
    pstart = pend - pcounts
    pos = jnp.sum(jnp.where(e[:, :, None] == experts, pstart, 0), axis=2) + rank
    blk_start = jnp.arange(n_blk, dtype=jnp.int32) * rb
    blk_e = jnp.minimum(jnp.sum((pend[None, :] <= blk_start[:, None]).astype(jnp.int32), axis=1),
                        N_EXPERTS - 1)
    n_used = (pend[-1] // rb).reshape(1)
    blk_idx = jnp.arange(n_blk, dtype=jnp.int32)
    later = (blk_idx[None, :] > blk_idx[:, None]) & (blk_idx[None, :] < n_used) & (blk_e[None, :] > blk_e[:, None])
    next_e = jnp.min(jnp.where(later, blk_e[None, :], N_EXPERTS), axis=1)
    next_e = jnp.where(next_e < N_EXPERTS, next_e, -1).astype(jnp.int32)
    tok = jnp.broadcast_to(jnp.arange(T, dtype=jnp.int32)[:, None], (T, TOP_K))
    row_tok = jnp.zeros((n_blk * rb,), jnp.int32).at[pos.reshape(-1)].set(
        tok.reshape(-1), unique_indices=True, mode="drop")
    return row_tok, blk_e, n_used, next_e, pos


def _largest_tile(n, cap):
    t = cap
    while n % t:
        t //= 2
    return t


def kernel(x_prompt, x_sample, cache_kv_latent, cache_k_rope, state_conv, state_ssm, ln0_g, ln0_b, w_in, q_norm, w_uq, kv_norm, w_uk, w_uv, conv_w, conv_b, dt_bias, a_log, d_skip, ssm_norm, w_o, ln1_g, ln1_b, w_rg, b_rg, w_re, b_re, w_gate, w_up, w_down, ln2_g, ln2_b):
    B, L, _ = x_prompt.shape
    Bs, Ls, _ = x_sample.shape
    past = cache_kv_latent.shape[2]
    Tp, Ts = B * L, Bs * Ls
    pw = _prep_weights(ln0_g, ln0_b, w_in[0], q_norm[0], w_uq[0], kv_norm[0], w_uk[0], w_uv[0], conv_w[0],
                       conv_b[0], dt_bias[0], a_log[0], d_skip[0], ssm_norm[0], w_o[0], ln1_g[0], ln1_b[0],
                       w_rg[0], b_rg[0], w_re[0], b_re[0], ln2_g[0], ln2_b[0])

    tm_p = _largest_tile(L, 512)
    xp = x_prompt.reshape(Tp, D_MODEL)
    cos_p, sin_p = _rope_tables(jnp.arange(L, dtype=jnp.int32))
    q_p, kv_p, lat_p, kr_p, krp_p, z_p, xbc_p, dt_p = _inproj(
        xp, pw, cos_p, sin_p, tm=tm_p, pos_blocks=L // tm_p, with_kv=True)
    attn_p = _attn_prompt(q_p.reshape(B, L, -1), kv_p.reshape(B, L, -1), krp_p.reshape(B, L, -1),
                          tq=_largest_tile(L, 512), tk=_largest_tile(L, 1024))
    q_ssd = min(CHUNK, L)
    y_p, conv_p, ssm_p = _ssd(
        xbc_p.reshape(B, L, -1), dt_p.reshape(B, L, -1), z_p.reshape(B, L, -1),
        jnp.zeros((B, CONV_K - 1, CONV_CH), F32), jnp.zeros((B, D_SSM, D_STATE), F32), pw,
        q=q_ssd, tt=_largest_tile(L, 512))

    xs = x_sample.reshape(Ts, D_MODEL)
    cos_s, sin_s = _rope_tables(past + jnp.arange(Ls, dtype=jnp.int32))
    tm_s = _largest_tile(Ts, 256)
    assert tm_s % Ls == 0
    q_s, _, lat_s, kr_s, krp_s, z_s, xbc_s, dt_s = _inproj(
        xs, pw, jnp.tile(cos_s, (tm_s // Ls, 1)), jnp.tile(sin_s, (tm_s // Ls, 1)),
        tm=tm_s, pos_blocks=1, with_kv=False)
    attn_s = _attn_sample(q_s.reshape(Bs, Ls, -1), pw, cache_kv_latent[0], cache_k_rope[0],
                          lat_s.reshape(Bs, Ls, -1), krp_s.reshape(Bs, Ls, -1),
                          tk=_largest_tile(past, 1024))
    y_s, conv_s, ssm_s = _ssd(
        xbc_s.reshape(Bs, Ls, -1), dt_s.reshape(Bs, Ls, -1), z_s.reshape(Bs, Ls, -1),
        state_conv[0], state_ssm[0].reshape(Bs, D_SSM, D_STATE), pw, q=min(CHUNK, Ls), tt=Ls)

    T = Tp + Ts
    tm_c = _largest_tile(math.gcd(Tp, Ts), 256)
    h_all, hpk_all, route, counts = _outproj(xp, xs, attn_p.reshape(Tp, -1), attn_s.reshape(Ts, -1),
                                             y_p.reshape(Tp, -1), y_s.reshape(Ts, -1), pw, tm=tm_c)
    rb = 256
    n_blk = (T * TOP_K + N_EXPERTS * (rb - 1) + rb - 1) // rb
    row_tok, blk_e, n_used, next_e, pos = _dispatch_tables(route, counts[0, R_OFF:R_OFF + N_EXPERTS],
                                                           rb=rb, n_blk=n_blk)
    y_rows = _experts(hpk_all, row_tok, blk_e, n_used, next_e, w_gate[0], w_up[0], w_down[0], rb=rb)
    pos_tiles = pos.reshape(T // tm_c, tm_c, TOP_K).transpose(0, 2, 1).reshape(-1)
    out_p, out_s = _combine(y_rows, pos_tiles, h_all, route, pw, tm=tm_c, t_prompt=Tp)

    return (out_p.reshape(B, L, D_MODEL), out_s.reshape(Bs, Ls, D_MODEL),
            lat_p.reshape(1, B, L, KV_LORA), lat_s.reshape(1, Bs, Ls, KV_LORA),
            kr_p.reshape(1, B, L, QK_ROPE), kr_s.reshape(1, Bs, Ls, QK_ROPE),
            conv_p[None], conv_s[None],
            ssm_p.reshape(1, B, SSM_HEADS, SSM_HEAD_DIM, D_STATE),
            ssm_s.reshape(1, Bs, SSM_HEADS, SSM_HEAD_DIM, D_STATE))
```

```python
import functools
import math

import jax
import jax.numpy as jnp
import numpy as np
from jax import lax
from jax.experimental import pallas as pl
from jax.experimental.pallas import tpu as pltpu

F32 = jnp.float32
BF16 = jnp.bfloat16

D_MODEL = 2048
CHUNK = 64
MLA_HEADS = 8
QK_NOPE = 128
QK_ROPE = 64
QK_DIM = QK_NOPE + QK_ROPE
V_DIM = 128
Q_LORA = 512
KV_LORA = 512
ROPE_THETA = 10000.0
ATTN_SCALE = QK_DIM ** -0.5
D_ATTN = MLA_HEADS * V_DIM
D_SSM = 1024
SSM_HEAD_DIM = 64
SSM_HEADS = D_SSM // SSM_HEAD_DIM
SSM_GROUPS = 2
HEADS_PER_GROUP = SSM_HEADS // SSM_GROUPS
D_STATE = 128
CONV_K = 4
CONV_CH = D_SSM + 2 * SSM_GROUPS * D_STATE
N_EXPERT_GROUPS = 8
EXPERTS_PER_GROUP = 8
N_EXPERTS = N_EXPERT_GROUPS * EXPERTS_PER_GROUP
TOP_K = 2
D_EXPERT = 512
DEPTH = 1
ALPHA = (2 * DEPTH) ** 0.25
RMS_EPS = 1e-6
LN_EPS = 1e-5

LANES = 128
MXU_DIM = 256
VMEM_LIMIT = 56 * 1024 * 1024

QK_PAD = 2 * LANES
GATHER_AHEAD = 2
GATHER_SLOTS = GATHER_AHEAD + 1
C_Q, C_KV, C_Z, C_XBC, C_KR, C_DT = 0, 512, 1024, 2048, 3584, 3712
D_IN_PAD = 3840


def _const_spec(shape):
    nd = len(shape)
    return pl.BlockSpec(shape, lambda *_: (0,) * nd, pipeline_mode=pl.Buffered(1))


def _split3(a):
    hi = a.astype(BF16)
    r1 = a - hi.astype(F32)
    mid = r1.astype(BF16)
    lo = (r1 - mid.astype(F32)).astype(BF16)
    return hi, mid, lo


def _dot(a, b):
    return jnp.dot(a, b, preferred_element_type=F32)


def _dot_nt(a, b):
    return lax.dot_general(a, b, (((1,), (1,)), ((), ())), preferred_element_type=F32)


def _dot_tn(a, b):
    return lax.dot_general(a, b, (((0,), (0,)), ((), ())), preferred_element_type=F32)


def _exact_dot(a_f32, sel_bf16):
    hi, mid, lo = _split3(a_f32)
    return _dot(hi, sel_bf16) + _dot(mid, sel_bf16) + _dot(lo, sel_bf16)


def _exact_dot_l(sel_bf16, a_f32):
    hi, mid, lo = _split3(a_f32)
    return _dot(sel_bf16, hi) + _dot(sel_bf16, mid) + _dot(sel_bf16, lo)


def _layernorm(x, g, b):
    mu = jnp.mean(x, axis=-1, keepdims=True)
    xc = x - mu
    var = jnp.mean(xc * xc, axis=-1, keepdims=True)
    return xc * lax.rsqrt(var + LN_EPS) * g + b


def _rmsnorm(x, g):
    return x * lax.rsqrt(jnp.mean(x * x, axis=-1, keepdims=True) + RMS_EPS) * g


def _silu(x):
    return x * (1.0 / (1.0 + jnp.exp(-x)))


def _pack_bf16_pairs(x):
    n = x.shape[1] // 2
    bits = lax.bitcast_convert_type(x, jnp.uint32)
    return (bits[:, n:] & jnp.uint32(0xFFFF0000)) | (bits[:, :n] >> 16)


def _unpack_bf16_pairs(u):
    lo = lax.bitcast_convert_type(u << 16, F32)
    hi = lax.bitcast_convert_type(u & jnp.uint32(0xFFFF0000), F32)
    return jnp.concatenate([lo, hi], axis=1)


def _rope128(x, cos_t, sin_t):
    lane = lax.broadcasted_iota(jnp.int32, x.shape, 1)
    partner = jnp.where(lane < QK_ROPE // 2, pltpu.roll(x, LANES - QK_ROPE // 2, 1),
                        pltpu.roll(x, QK_ROPE // 2, 1))
    return x * cos_t + partner * sin_t


def _inproj_kernel(x_ref, g0_ref, b0_ref, w_in_ref, qn_ref, kvn_ref, w_uq_ref, w_ukv_ref,
                   cos_ref, sin_ref, dtb_ref,
                   q_ref, kv_ref, lat_ref, kr_ref, krp_ref, z_ref, xbc_ref, dt_ref, *, with_kv):
    xn = _layernorm(x_ref[...], g0_ref[...], b0_ref[...]).astype(BF16)
    cos_t = cos_ref[...]
    sin_t = sin_ref[...]

    c_q = _dot(xn, w_in_ref[:, C_Q:C_Q + Q_LORA])
    qb = _rmsnorm(c_q, qn_ref[...]).astype(BF16)
    for h in range(MLA_HEADS):
        qh = _dot(qb, w_uq_ref[:, h * QK_PAD:(h + 1) * QK_PAD])
        q_ref[:, h * QK_PAD:h * QK_PAD + LANES] = qh[:, :LANES].astype(BF16)
        q_ref[:, h * QK_PAD + LANES:(h + 1) * QK_PAD] = _rope128(qh[:, LANES:], cos_t, sin_t).astype(BF16)

    c_kv = _dot(xn, w_in_ref[:, C_KV:C_KV + KV_LORA])
    lat = _rmsnorm(c_kv, kvn_ref[...])
    lat_ref[...] = lat
    if with_kv:
        kv_ref[...] = _dot(lat.astype(BF16), w_ukv_ref[...]).astype(BF16)
    else:
        kv_ref[...] = jnp.zeros(kv_ref.shape, BF16)

    k_r = _rope128(_dot(xn, w_in_ref[:, C_KR:C_KR + LANES]), cos_t, sin_t)
    kr_ref[...] = k_r[:, :QK_ROPE]
    krp_ref[...] = k_r.astype(BF16)

    z_ref[...] = _dot(xn, w_in_ref[:, C_Z:C_Z + D_SSM]).astype(BF16)
    xbc_ref[...] = _dot(xn, w_in_ref[:, C_XBC:C_XBC + CONV_CH])

    dt_raw = _dot(xn, w_in_ref[:, C_DT:C_DT + LANES]) + dtb_ref[...]
    dt = jnp.maximum(dt_raw, 0.0) + jnp.log1p(jnp.exp(-jnp.abs(dt_raw)))
    lane = lax.broadcasted_iota(jnp.int32, dt.shape, 1)
    dt_ref[...] = jnp.where(lane < SSM_HEADS, dt, 0.0)


def _inproj(x, pw, cos_t, sin_t, *, tm, pos_blocks, with_kv):
    T = x.shape[0]
    n = T // tm
    row = lambda i: (i, 0)
    pos = lambda i: (i % pos_blocks, 0)
    kv_cols = 2 * D_ATTN if with_kv else LANES
    out_shape = [
        jax.ShapeDtypeStruct((T, MLA_HEADS * QK_PAD), BF16),
        jax.ShapeDtypeStruct((T, kv_cols), BF16),
        jax.ShapeDtypeStruct((T, KV_LORA), F32),
        jax.ShapeDtypeStruct((T, QK_ROPE), F32),
        jax.ShapeDtypeStruct((T, LANES), BF16),
        jax.ShapeDtypeStruct((T, D_SSM), BF16),
        jax.ShapeDtypeStruct((T, CONV_CH), F32),
        jax.ShapeDtypeStruct((T, LANES), F32),
    ]
    out_specs = [pl.BlockSpec((tm, s.shape[1]), row) for s in out_shape]
    in_specs = [
        pl.BlockSpec((tm, D_MODEL), row),
        _const_spec((1, D_MODEL)), _const_spec((1, D_MODEL)),
        _const_spec((D_MODEL, D_IN_PAD)),
        _const_spec((1, Q_LORA)), _const_spec((1, KV_LORA)),
        _const_spec((Q_LORA, MLA_HEADS * QK_PAD)),
        _const_spec((KV_LORA, 2 * D_ATTN)),
        pl.BlockSpec((tm, LANES), pos), pl.BlockSpec((tm, LANES), pos),
        _const_spec((1, LANES)),
    ]
    return pl.pallas_call(
        functools.partial(_inproj_kernel, with_kv=with_kv),
        grid=(n,), in_specs=in_specs, out_specs=out_specs, out_shape=out_shape,
        compiler_params=pltpu.CompilerParams(dimension_semantics=("arbitrary",),
                                             vmem_limit_bytes=VMEM_LIMIT),
        name="inproj",
    )(x, pw["ln0_g"], pw["ln0_b"], pw["w_in"], pw["q_norm"], pw["kv_norm"], pw["w_uq"],
      pw["w_ukv"], cos_t, sin_t, pw["dt_bias"])


def _attn_prompt_kernel(qi_ref, ki_ref, q_ref, kv_ref, krp_ref, o_ref, *scratch, tq, tk):
    m_refs = scratch[0:MLA_HEADS]
    acc_refs = scratch[MLA_HEADS:2 * MLA_HEADS]
    p_id = pl.program_id(1)
    qi = qi_ref[p_id]
    ki = ki_ref[p_id]
    k_last = ((qi + 1) * tq - 1) // tk

    @pl.when(ki == 0)
    def _():
        for h in range(MLA_HEADS):
            m_refs[h][...] = jnp.full(m_refs[h].shape, -jnp.inf, F32)
            acc_refs[h][...] = jnp.zeros(acc_refs[h].shape, F32)

    def step(masked):
        krp = krp_ref[...]
        ones_col = (lax.broadcasted_iota(jnp.int32, (tk, LANES), 1) == 0).astype(BF16)
        if masked:
            r = (qi * tq + lax.broadcasted_iota(jnp.int32, (tq, tk), 0)) // CHUNK
            c = (ki * tk + lax.broadcasted_iota(jnp.int32, (tq, tk), 1)) // CHUNK
            allowed = c <= r
        for h in range(MLA_HEADS):
            qh = q_ref[:, h * QK_PAD:(h + 1) * QK_PAD]
            kh = jnp.concatenate([kv_ref[:, h * QK_NOPE:(h + 1) * QK_NOPE], krp], axis=1)
            s = _dot_nt(qh, kh)
            if masked:
                s = jnp.where(allowed, s, -jnp.inf)
            m_old = m_refs[h][...]
            m_new = jnp.maximum(m_old, jnp.max(s, axis=1, keepdims=True))
            alpha = jnp.exp2(m_old - m_new)
            p = jnp.exp2(s - m_new).astype(BF16)
            v_aug = jnp.concatenate([kv_ref[:, D_ATTN + h * V_DIM:D_ATTN + (h + 1) * V_DIM], ones_col], axis=1)
            acc_refs[h][...] = alpha * acc_refs[h][...] + _dot(p, v_aug)
            m_refs[h][...] = m_new

    @pl.when(ki < k_last)
    def _():
        step(False)

    @pl.when(ki == k_last)
    def _():
        step(True)
        for h in range(MLA_HEADS):
            acc = acc_refs[h][...]
            o_ref[:, h * V_DIM:(h + 1) * V_DIM] = (acc[:, :V_DIM] / acc[:, V_DIM:V_DIM + 1]).astype(BF16)


def _attn_prompt(q, kv, krp, *, tq, tk):
    B, L, _ = q.shape
    nq = L // tq
    pairs = [(i, j) for i in range(nq) for j in range(((i + 1) * tq - 1) // tk + 1)]
    qi_tab = jnp.asarray(np.array([p[0] for p in pairs], np.int32))
    ki_tab = jnp.asarray(np.array([p[1] for p in pairs], np.int32))
    grid_spec = pltpu.PrefetchScalarGridSpec(
        num_scalar_prefetch=2,
        grid=(B, len(pairs)),
        in_specs=[
            pl.BlockSpec((None, tq, MLA_HEADS * QK_PAD), lambda b, p, qi, ki: (b, qi[p], 0)),
            pl.BlockSpec((None, tk, 2 * D_ATTN), lambda b, p, qi, ki: (b, ki[p], 0)),
            pl.BlockSpec((None, tk, LANES), lambda b, p, qi, ki: (b, ki[p], 0)),
        ],
        out_specs=pl.BlockSpec((None, tq, D_ATTN), lambda b, p, qi, ki: (b, qi[p], 0)),
        scratch_shapes=([pltpu.VMEM((tq, 1), F32)] * MLA_HEADS
                        + [pltpu.VMEM((tq, 2 * V_DIM), F32)] * MLA_HEADS),
    )
    return pl.pallas_call(
        functools.partial(_attn_prompt_kernel, tq=tq, tk=tk),
        grid_spec=grid_spec,
        out_shape=jax.ShapeDtypeStruct((B, L, D_ATTN), BF16),
        compiler_params=pltpu.CompilerParams(dimension_semantics=("arbitrary", "arbitrary"),
                                             vmem_limit_bytes=VMEM_LIMIT),
        name="attn_prompt",
    )(qi_tab, ki_tab, q, kv, krp)


def _attn_sample_kernel(q_ref, wukt_ref, wuv_ref, latp_ref, krpast_ref, latn_ref, krn_ref, o_ref,
                        qlat_ref, qrp_ref, m_ref, l_ref, acc_ref, *, ls, past, tk, n_new_pad):
    k = pl.program_id(1)
    nk = pl.num_programs(1)
    rows = MLA_HEADS * ls

    @pl.when(k == 0)
    def _():
        for h in range(MLA_HEADS):
            qn = q_ref[:, h * QK_PAD:h * QK_PAD + LANES]
            qlat_ref[h * ls:(h + 1) * ls, :] = _dot(qn, wukt_ref[h]).astype(BF16)
            qrp_ref[h * ls:(h + 1) * ls, :] = q_ref[:, h * QK_PAD + LANES:(h + 1) * QK_PAD]
        m_ref[...] = jnp.full(m_ref.shape, -jnp.inf, F32)
        l_ref[...] = jnp.zeros(l_ref.shape, F32)
        acc_ref[...] = jnp.zeros(acc_ref.shape, F32)

    q_chunk = (past + lax.broadcasted_iota(jnp.int32, (rows, 1), 0) % ls) // CHUNK

    def update(s, lat_b):
        m_old = m_ref[...]
        m_new = jnp.maximum(m_old, jnp.max(s, axis=1, keepdims=True))
        alpha = jnp.exp2(m_old - m_new)
        p = jnp.exp2(s - m_new)
        l_ref[...] = alpha * l_ref[...] + jnp.sum(p, axis=1, keepdims=True)
        acc_ref[...] = alpha * acc_ref[...] + _dot(p.astype(BF16), lat_b)
        m_ref[...] = m_new

    lat_b = latp_ref[...].astype(BF16)
    kr_b = krpast_ref[...].astype(BF16)
    s = _dot_nt(qlat_ref[...], lat_b) + _dot_nt(qrp_ref[:, :QK_ROPE], kr_b)
    k_pos = k * tk + lax.broadcasted_iota(jnp.int32, (rows, tk), 1)
    s = jnp.where(k_pos // CHUNK <= q_chunk, s, -jnp.inf)
    update(s, lat_b)

    @pl.when(k == nk - 1)
    def _():
        latn_b = latn_ref[...].astype(BF16)
        s2 = _dot_nt(qlat_ref[...], latn_b) + _dot_nt(qrp_ref[...], krn_ref[...])
        j = lax.broadcasted_iota(jnp.int32, (rows, n_new_pad), 1)
        ok = ((past + j) // CHUNK <= q_chunk) & (j < ls)
        update(jnp.where(ok, s2, -jnp.inf), latn_b)
        o_lat = (acc_ref[...] / l_ref[...]).astype(BF16)
        for h in range(MLA_HEADS):
            o_ref[:, h * V_DIM:(h + 1) * V_DIM] = _dot(o_lat[h * ls:(h + 1) * ls, :], wuv_ref[h]).astype(BF16)


def _attn_sample(q, pw, lat_past, kr_past, lat_new, krp_new, *, tk):
    Bs, Ls, _ = q.shape
    past = lat_past.shape[1]
    n_new_pad = LANES
    lat_new = jnp.pad(lat_new, ((0, 0), (0, n_new_pad - Ls), (0, 0)))
    krp_new = jnp.pad(krp_new, ((0, 0), (0, n_new_pad - Ls), (0, 0)))
    rows = MLA_HEADS * Ls
    return pl.pallas_call(
        functools.partial(_attn_sample_kernel, ls=Ls, past=past, tk=tk, n_new_pad=n_new_pad),
        grid=(Bs, past // tk),
        in_specs=[
            pl.BlockSpec((None, Ls, MLA_HEADS * QK_PAD), lambda b, k: (b, 0, 0)),
            _const_spec((MLA_HEADS, QK_NOPE, KV_LORA)),
            _const_spec((MLA_HEADS, KV_LORA, V_DIM)),
            pl.BlockSpec((None, tk, KV_LORA), lambda b, k: (b, k, 0)),
            pl.BlockSpec((None, tk, QK_ROPE), lambda b, k: (b, k, 0)),
            pl.BlockSpec((None, n_new_pad, KV_LORA), lambda b, k: (b, 0, 0)),
            pl.BlockSpec((None, n_new_pad, LANES), lambda b, k: (b, 0, 0)),
        ],
        out_specs=pl.BlockSpec((None, Ls, D_ATTN), lambda b, k: (b, 0, 0)),
        out_shape=jax.ShapeDtypeStruct((Bs, Ls, D_ATTN), BF16),
        scratch_shapes=[pltpu.VMEM((rows, KV_LORA), BF16), pltpu.VMEM((rows, LANES), BF16),
                        pltpu.VMEM((rows, 1), F32), pltpu.VMEM((rows, 1), F32),
                        pltpu.VMEM((rows, KV_LORA), F32)],
        compiler_params=pltpu.CompilerParams(dimension_semantics=("arbitrary", "arbitrary"),
                                             vmem_limit_bytes=VMEM_LIMIT),
        name="attn_sample",
    )(q, pw["w_ukt"], pw["w_uvh"], lat_past, kr_past, lat_new, krp_new)


def _ssd_consts(q):
    hb = MXU_DIM // q
    hq = SSM_HEADS * q
    lane = np.arange(hq)
    ltri = (np.arange(q)[None, :] <= np.arange(q)[:, None]).astype(np.float32)
    sel_c = (np.arange(LANES)[:, None] == (lane // q)[None, :]).astype(np.float32)
    sel_p = (np.arange(LANES)[:, None] == (np.arange(D_SSM) // SSM_HEAD_DIM)[None, :]).astype(np.float32)
    diag = (np.arange(q)[:, None] == (lane % q)[None, :]).astype(np.float32)
    causal = (np.arange(q)[:, None] >= (lane % q)[None, :]).astype(np.float32)
    bd_rows = np.arange(hb * q) // q
    bd_cols = np.arange(hb * SSM_HEAD_DIM) // SSM_HEAD_DIM
    bdmask = (bd_rows[:, None] == bd_cols[None, :]).astype(np.float32)
    return dict(ltri=jnp.asarray(ltri, BF16), sel_c=jnp.asarray(sel_c, BF16),
                sel_p=jnp.asarray(sel_p, BF16), diag=jnp.asarray(diag, F32),
                causal=jnp.asarray(causal, F32), bdmask=jnp.asarray(bdmask, BF16))


def _ssd_kernel(xbc_ref, dt_ref, z_ref, cprev_ref, h0_ref, cw_ref, cb_ref, a_ref, dskip_ref, norm_ref,
                ltri_ref, selc_ref, selp_ref, diag_ref, causal_ref, bdmask_ref,
                y_ref, cout_ref, hout_ref,
                ext_ref, u_ref, ht_ref, *, q, tt):
    t = pl.program_id(1)
    nt = pl.num_programs(1)
    hb = MXU_DIM // q
    nblk = SSM_HEADS // hb
    gw = HEADS_PER_GROUP * SSM_HEAD_DIM
    pad = 8

    @pl.when(t == 0)
    def _():
        ext_ref[0:pad, :] = jnp.zeros((pad, CONV_CH), F32)
        ext_ref[pad - (CONV_K - 1):pad, :] = cprev_ref[...]
        ht_ref[...] = h0_ref[...].T

    xbc = xbc_ref[...]
    ext_ref[pad:pad + tt, :] = xbc
    cout_ref[...] = xbc[tt - (CONV_K - 1):, :]
    conv = cb_ref[...] + cw_ref[0:1, :] * ext_ref[pad - 3:pad - 3 + tt, :]
    for k in range(1, CONV_K):
        conv = conv + cw_ref[k:k + 1, :] * ext_ref[pad - 3 + k:pad - 3 + k + tt, :]
    u_ref[...] = _silu(conv)
    ext_ref[pad - (CONV_K - 1):pad, :] = xbc[tt - (CONV_K - 1):, :]

    a_row = a_ref[...]
    d_row = dskip_ref[...]
    norm_row = norm_ref[...]

    def chunk(c, carry):
        r0 = pl.multiple_of(c * q, q)
        u = u_ref[pl.ds(r0, q), :]
        xs = u[:, :D_SSM]
        dt = dt_ref[pl.ds(r0, q), :]
        a_cs = _exact_dot_l(ltri_ref[...], dt * a_row)
        ap = _exact_dot(a_cs, selp_ref[...])
        dtp = _exact_dot(dt, selp_ref[...])
        a_i = ap if q == SSM_HEAD_DIM else _exact_dot(a_cs, selc_ref[...])
        a_j = jnp.sum(a_i * diag_ref[...], axis=0, keepdims=True)
        decay = jnp.exp(jnp.where(causal_ref[...] > 0.5, a_i - a_j, -jnp.inf))
        ap_last = ap[q - 1:q, :]
        xdt = (xs * dtp).astype(BF16)
        xw = (xs * (jnp.exp(ap_last - ap) * dtp)).astype(BF16)
        e_ap = jnp.exp(ap)

        cb_parts, y_off_parts = [], []
        for g in range(SSM_GROUPS):
            bm = u[:, D_SSM + g * D_STATE:D_SSM + (g + 1) * D_STATE].astype(BF16)
            cm = u[:, D_SSM + (SSM_GROUPS + g) * D_STATE:D_SSM + (SSM_GROUPS + g + 1) * D_STATE].astype(BF16)
            cb_parts.append(_dot_nt(cm, jnp.concatenate([bm] * HEADS_PER_GROUP, axis=0)))
            gl = slice(g * gw, (g + 1) * gw)
            ht_g = ht_ref[:, gl]
            y_off_parts.append(_dot(cm, ht_g.astype(BF16)) * e_ap[:, gl])
            st = _dot_tn(bm, xw[:, gl])
            ht_ref[:, gl] = ht_g * jnp.exp(ap_last[:, gl]) + st
        w_all = (jnp.concatenate(cb_parts, axis=1) * decay).astype(BF16)
        y_diag_parts = []
        for b in range(nblk):
            x_b = xdt[:, b * hb * SSM_HEAD_DIM:(b + 1) * hb * SSM_HEAD_DIM]
            bd = jnp.concatenate([x_b] * hb, axis=0) * bdmask_ref[...]
            y_diag_parts.append(_dot(w_all[:, b * MXU_DIM:(b + 1) * MXU_DIM], bd))
        y = jnp.concatenate(y_diag_parts, axis=1) + jnp.concatenate(y_off_parts, axis=1)

        y = y + d_row * xs
        y = y * _silu(z_ref[pl.ds(r0, q), :].astype(F32))
        outs = []
        for g in range(SSM_GROUPS):
            yg = y[:, g * gw:(g + 1) * gw]
            outs.append(yg * lax.rsqrt(jnp.mean(yg * yg, axis=1, keepdims=True) + RMS_EPS))
        y_ref[pl.ds(r0, q), :] = (jnp.concatenate(outs, axis=1) * norm_row).astype(BF16)
        return carry

    lax.fori_loop(0, tt // q, chunk, 0, unroll=True)

    @pl.when(t == nt - 1)
    def _():
        hout_ref[...] = ht_ref[...].T


def _ssd(xbc, dt, z, conv_prev, h0, pw, *, q, tt):
    B, L, _ = xbc.shape
    cs = _ssd_consts(q)
    hq = SSM_HEADS * q
    hb = MXU_DIM // q
    tile = lambda w: pl.BlockSpec((None, tt, w), lambda b, t: (b, t, 0))
    per_b = lambda s: pl.BlockSpec((None,) + s, lambda b, t: (b, 0, 0))
    return pl.pallas_call(
        functools.partial(_ssd_kernel, q=q, tt=tt),
        grid=(B, L // tt),
        in_specs=[tile(CONV_CH), tile(LANES), tile(D_SSM), per_b((CONV_K - 1, CONV_CH)),
                  per_b((D_SSM, D_STATE)),
                  _const_spec((CONV_K, CONV_CH)), _const_spec((1, CONV_CH)), _const_spec((1, LANES)),
                  _const_spec((1, D_SSM)), _const_spec((1, D_SSM)),
                  _const_spec((q, q)), _const_spec((LANES, hq)), _const_spec((LANES, D_SSM)),
                  _const_spec((q, hq)), _const_spec((q, hq)),
                  _const_spec((MXU_DIM, hb * SSM_HEAD_DIM))],
        out_specs=[tile(D_SSM), per_b((CONV_K - 1, CONV_CH)), per_b((D_SSM, D_STATE))],
        out_shape=[jax.ShapeDtypeStruct((B, L, D_SSM), BF16),
                   jax.ShapeDtypeStruct((B, CONV_K - 1, CONV_CH), F32),
                   jax.ShapeDtypeStruct((B, D_SSM, D_STATE), F32)],
        scratch_shapes=[pltpu.VMEM((8 + tt, CONV_CH), F32), pltpu.VMEM((tt, CONV_CH), F32),
                        pltpu.VMEM((D_STATE, D_SSM), F32)],
        compiler_params=pltpu.CompilerParams(dimension_semantics=("arbitrary", "arbitrary"),
                                             vmem_limit_bytes=VMEM_LIMIT),
        name="ssd",
    )(xbc, dt, z, conv_prev, h0, pw["conv_w"], pw["conv_b"], pw["a_neg"], pw["d_skip"], pw["ssm_norm"],
      cs["ltri"], cs["sel_c"], cs["sel_p"], cs["diag"], cs["causal"], cs["bdmask"])


R_OFF = N_EXPERT_GROUPS


def _outproj_kernel(xp_ref, xs_ref, attnp_ref, attns_ref, yp_ref, ys_ref, g0_ref, b0_ref, wo_a_ref, wo_s_ref,
                    g1_ref, b1_ref, wr_ref, br_ref, ltri_ref, h_ref, hpk_ref, route_ref, cnt_ref,
                    *, n_prompt_tiles, tm, n_groups):
    i = pl.program_id(0)
    is_p = i < n_prompt_tiles

    @pl.when(i == 0)
    def _():
        cnt_ref[...] = jnp.zeros(cnt_ref.shape, F32)

    cnt = cnt_ref[0:1, :]
    for g in range(n_groups):
        rows = slice(g * (tm // n_groups), (g + 1) * (tm // n_groups))
        refs = (xp_ref, xs_ref, attnp_ref, attns_ref, yp_ref, ys_ref)
        cnt = _outproj_rows(rows, is_p, cnt, refs, g0_ref, b0_ref, wo_a_ref, wo_s_ref, g1_ref, b1_ref,
                            wr_ref, br_ref, ltri_ref, h_ref, hpk_ref, route_ref)
    cnt_ref[...] = jnp.broadcast_to(cnt, cnt_ref.shape)


def _outproj_rows(rows, is_p, cnt, refs, g0_ref, b0_ref, wo_a_ref, wo_s_ref, g1_ref, b1_ref, wr_ref, br_ref,
                  ltri_ref, h_ref, hpk_ref, route_ref):
    xp_ref, xs_ref, attnp_ref, attns_ref, yp_ref, ys_ref = refs
    x = jnp.where(is_p, xp_ref[rows, :], xs_ref[rows, :])
    attn = jnp.where(is_p, attnp_ref[rows, :], attns_ref[rows, :])
    yssm = jnp.where(is_p, yp_ref[rows, :], ys_ref[rows, :])

    xn = _layernorm(x, g0_ref[...], b0_ref[...])
    mixed = _dot(attn, wo_a_ref[...]) + _dot(yssm, wo_s_ref[...])
    h = _layernorm(ALPHA * xn + mixed, g1_ref[...], b1_ref[...])
    h_ref[rows, :] = h

    h_hi = h.astype(BF16)
    h_hi32 = h_hi.astype(F32)
    hpk_ref[rows, :] = _pack_bf16_pairs(h_hi32)
    h_lo = (h - h_hi32).astype(BF16)
    tm = h.shape[0]
    prod = _dot(jnp.concatenate([h_hi, h_lo], axis=0), wr_ref[...])
    lg = (prod[:tm, :LANES] + prod[:tm, LANES:]) + (prod[tm:, :LANES] + prod[tm:, LANES:]) + br_ref[...]
    lane = lax.broadcasted_iota(jnp.int32, lg.shape, 1)
    big = jnp.int32(1 << 20)
    gl = jnp.where(lane < N_EXPERT_GROUPS, lg, -jnp.inf)
    gmax = jnp.max(gl, axis=1, keepdims=True)
    grp = jnp.min(jnp.where(gl == gmax, lane, big), axis=1, keepdims=True)
    g_w = 1.0 / jnp.sum(jnp.exp(gl - gmax), axis=1, keepdims=True)
    in_grp = (lane >= R_OFF) & (lane < R_OFF + N_EXPERTS) & ((lane - R_OFF) // EXPERTS_PER_GROUP == grp)
    el = jnp.where(in_grp, lg, -jnp.inf)
    emax = jnp.max(el, axis=1, keepdims=True)
    ee = jnp.exp(el - emax)
    prob = jnp.where(in_grp, ee / jnp.sum(ee, axis=1, keepdims=True), -1.0)
    p1 = jnp.max(prob, axis=1, keepdims=True)
    i1 = jnp.min(jnp.where(prob == p1, lane, big), axis=1, keepdims=True)
    prob2 = jnp.where(lane == i1, -1.0, prob)
    p2 = jnp.max(prob2, axis=1, keepdims=True)
    i2 = jnp.min(jnp.where(prob2 == p2, lane, big), axis=1, keepdims=True)
    denom = p1 + p2
    oh1 = (lane == i1).astype(F32)
    oh2 = (lane == i2).astype(F32)
    oh = oh1 + oh2
    before = _dot(ltri_ref[...], oh.astype(BF16)) + cnt
    rank1 = jnp.sum(before * oh1, axis=1, keepdims=True)
    rank2 = jnp.sum(before * oh2, axis=1, keepdims=True)
    route = jnp.where(lane == 0, (i1 - R_OFF).astype(F32),
                      jnp.where(lane == 1, (i2 - R_OFF).astype(F32),
                                jnp.where(lane == 2, g_w * p1 / denom,
                                          jnp.where(lane == 3, g_w * p2 / denom,
                                                    jnp.where(lane == 4, rank1,
                                                              jnp.where(lane == 5, rank2, 0.0))))))
    route_ref[rows, :] = route
    return cnt + jnp.sum(oh, axis=0, keepdims=True)


def _outproj(xp, xs, attn_p, attn_s, y_p, y_s, pw, *, tm):
    Tp, Ts = xp.shape[0], xs.shape[0]
    npt, nst = Tp // tm, Ts // tm
    T = Tp + Ts
    row = lambda i: (i, 0)
    prow = lambda i: (jnp.minimum(i, npt - 1), 0)
    srow = lambda i: (jnp.maximum(i - npt, 0), 0)
    n_groups = 1
    gm = tm // n_groups
    ltri = jnp.asarray(np.tril(np.ones((gm, gm), np.float32), -1), BF16)
    return pl.pallas_call(
        functools.partial(_outproj_kernel, n_prompt_tiles=npt, tm=tm, n_groups=n_groups),
        grid=(npt + nst,),
        in_specs=[pl.BlockSpec((tm, D_MODEL), prow), pl.BlockSpec((tm, D_MODEL), srow),
                  pl.BlockSpec((tm, D_ATTN), prow), pl.BlockSpec((tm, D_ATTN), srow),
                  pl.BlockSpec((tm, D_SSM), prow), pl.BlockSpec((tm, D_SSM), srow),
                  _const_spec((1, D_MODEL)), _const_spec((1, D_MODEL)),
                  _const_spec((D_ATTN, D_MODEL)), _const_spec((D_SSM, D_MODEL)),
                  _const_spec((1, D_MODEL)), _const_spec((1, D_MODEL)),
                  _const_spec((D_MODEL, 2 * LANES)), _const_spec((1, LANES)),
                  _const_spec((gm, gm))],
        out_specs=[pl.BlockSpec((tm, D_MODEL), row), pl.BlockSpec((tm, D_MODEL // 2), row),
                   pl.BlockSpec((tm, LANES), row), pl.BlockSpec((8, LANES), lambda i: (0, 0))],
        out_shape=[jax.ShapeDtypeStruct((T, D_MODEL), F32), jax.ShapeDtypeStruct((T, D_MODEL // 2), jnp.uint32),
                   jax.ShapeDtypeStruct((T, LANES), F32), jax.ShapeDtypeStruct((8, LANES), F32)],
        compiler_params=pltpu.CompilerParams(dimension_semantics=("arbitrary",),
                                             vmem_limit_bytes=VMEM_LIMIT),
        name="outproj",
    )(xp, xs, attn_p, attn_s, y_p, y_s, pw["ln0_g"], pw["ln0_b"], pw["w_o_a"], pw["w_o_s"],
      pw["ln1_g"], pw["ln1_b"], pw["w_r"], pw["b_r"], ltri)


def _dispatch_kernel(pad_base_ref, pad_n_ref, n_used_ref, pos_ref, hpk_hbm, zrow_hbm, xs_hbm, sem,
                     *, tm, n_tiles, rb, n_blk):
    i = pl.program_id(0)
    rows = TOP_K * tm

    def wait_tile():
        pltpu.make_async_copy(hpk_hbm.at[pl.ds(0, rows)], xs_hbm.at[pl.ds(0, rows)], sem.at[0]).wait()

    @pl.when(i < n_tiles)
    def _():
        @pl.when(i > 0)
        def _():
            wait_tile()
        t0 = i * tm
        for a in range(rows):
            pltpu.make_async_copy(hpk_hbm.at[pl.ds(t0 + a % tm, 1)], xs_hbm.at[pl.ds(pos_ref[a], 1)],
                                  sem.at[0]).start(priority=a % 2)

    @pl.when(i == n_tiles)
    def _():
        wait_tile()

        def tail_copy(b):
            r0 = pl.multiple_of(b * rb, rb)
            return pltpu.make_async_copy(zrow_hbm, xs_hbm.at[pl.ds(r0, rb)], sem.at[1])

        def tail_start(b, c):
            tail_copy(b).start()
            return c

        def tail_wait(b, c):
            tail_copy(b).wait()
            return c
        lax.fori_loop(n_used_ref[0], n_blk, tail_start, 0)
        lax.fori_loop(n_used_ref[0], n_blk, tail_wait, 0)

    @pl.when(i >= n_tiles)
    def _():
        e = i - n_tiles
        base = pad_base_ref[e]

        def pad_copy(j):
            return pltpu.make_async_copy(zrow_hbm.at[pl.ds(0, 1)], xs_hbm.at[pl.ds(base + j, 1)], sem.at[1])

        def start(j, c):
            pad_copy(j).start()
            return c

        def wait(j, c):
            pad_copy(j).wait()
            return c
        lax.fori_loop(0, pad_n_ref[e], start, 0)
        lax.fori_loop(0, pad_n_ref[e], wait, 0)


def _dispatch(hpk_all, pos_tiles, pad_base, pad_n, n_used, *, tm, rb, n_blk):
    T = hpk_all.shape[0]
    n_tiles = T // tm
    rows = TOP_K * tm
    grid_spec = pltpu.PrefetchScalarGridSpec(
        num_scalar_prefetch=3,
        grid=(n_tiles + N_EXPERTS,),
        in_specs=[pl.BlockSpec((rows,), lambda i, pb, pn, nu: (jnp.minimum(i, n_tiles - 1),),
                               memory_space=pltpu.SMEM),
                  pl.BlockSpec(memory_space=pl.ANY), pl.BlockSpec(memory_space=pl.ANY)],
        out_specs=pl.BlockSpec(memory_space=pl.ANY),
        scratch_shapes=[pltpu.SemaphoreType.DMA((2,))],
    )
    return pl.pallas_call(
        functools.partial(_dispatch_kernel, tm=tm, n_tiles=n_tiles, rb=rb, n_blk=n_blk),
        grid_spec=grid_spec,
        out_shape=jax.ShapeDtypeStruct((n_blk * rb, D_MODEL // 2), jnp.uint32),
        compiler_params=pltpu.CompilerParams(dimension_semantics=("arbitrary",)),
        name="dispatch",
    )(pad_base, pad_n, n_used, pos_tiles, hpk_all, jnp.zeros((rb, D_MODEL // 2), jnp.uint32))


def _experts_kernel(blk_e_ref, n_used_ref, next_e_ref, x_ref, wg_hbm, wu_hbm, wd_hbm, y_ref,
                    wg_s, wu_s, wd_s, wsem, wg_b, wu_b, wd_b, prev_e):
    i = pl.program_id(0)
    n_used = n_used_ref[0]

    def weight_copies(e):
        return (pltpu.make_async_copy(wg_hbm.at[e], wg_s, wsem.at[0]),
                pltpu.make_async_copy(wu_hbm.at[e], wu_s, wsem.at[1]),
                pltpu.make_async_copy(wd_hbm.at[e], wd_s, wsem.at[2]))

    @pl.when(i == 0)
    def _():
        prev_e[0] = -1
        for c in weight_copies(blk_e_ref[0]):
            c.start()

    @pl.when(i < n_used)
    def _():
        e = blk_e_ref[i]

        @pl.when(e != prev_e[0])
        def _():
            for c in weight_copies(e):
                c.wait()
            wg_b[...] = wg_s[...].astype(BF16)
            wu_b[...] = wu_s[...].astype(BF16)
            wd_b[...] = wd_s[...].astype(BF16)
            prev_e[0] = e
            nxt = next_e_ref[i]

            @pl.when(nxt >= 0)
            def _():
                for c in weight_copies(nxt):
                    c.start()

        xb = _unpack_bf16_pairs(x_ref[...]).astype(BF16)
        hid = (_silu(_dot(xb, wg_b[...])) * _dot(xb, wu_b[...])).astype(BF16)
        y = _dot(hid, wd_b[...])
        y_ref[...] = _pack_bf16_pairs(y.astype(BF16).astype(F32))

    @pl.when(i >= n_used)
    def _():
        y_ref[...] = jnp.zeros(y_ref.shape, jnp.uint32)


def _experts(xs_rows, blk_e, n_used, next_e, w_gate, w_up, w_down, *, rb):
    n_blk = blk_e.shape[0]
    n_rows = n_blk * rb
    hbm = pl.BlockSpec(memory_space=pl.ANY)
    grid_spec = pltpu.PrefetchScalarGridSpec(
        num_scalar_prefetch=3,
        grid=(n_blk,),
        in_specs=[pl.BlockSpec((rb, D_MODEL // 2), lambda i, be, nu, ne: (jnp.minimum(i, nu[0] - 1), 0)),
                  hbm, hbm, hbm],
        out_specs=pl.BlockSpec((rb, D_MODEL // 2), lambda i, be, nu, ne: (i, 0)),
        scratch_shapes=[pltpu.VMEM((D_MODEL, D_EXPERT), F32), pltpu.VMEM((D_MODEL, D_EXPERT), F32),
                        pltpu.VMEM((D_EXPERT, D_MODEL), F32), pltpu.SemaphoreType.DMA((3,)),
                        pltpu.VMEM((D_MODEL, D_EXPERT), BF16), pltpu.VMEM((D_MODEL, D_EXPERT), BF16),
                        pltpu.VMEM((D_EXPERT, D_MODEL), BF16), pltpu.SMEM((1,), jnp.int32)],
    )
    return pl.pallas_call(
        _experts_kernel,
        grid_spec=grid_spec,
        out_shape=jax.ShapeDtypeStruct((n_rows, D_MODEL // 2), jnp.uint32),
        compiler_params=pltpu.CompilerParams(dimension_semantics=("arbitrary",),
                                             vmem_limit_bytes=VMEM_LIMIT),
        name="experts",
    )(blk_e, n_used, next_e, xs_rows, w_gate, w_up, w_down)


def _combine_kernel(pos_ref, pos_n1_ref, pos_n2_ref, y_hbm, h_ref, route_ref, g2_ref, b2_ref, op_ref, os_ref,
                    ybuf, sem, *, tm, n_prompt_tiles):
    i = pl.program_id(0)
    n = pl.num_programs(0)
    slot = i % GATHER_SLOTS
    rows = TOP_K * tm

    def row_copy(idx, dst_slot, r):
        return pltpu.make_async_copy(y_hbm.at[pl.ds(idx, 1)], ybuf.at[dst_slot, pl.ds(r, 1)],
                                     sem.at[dst_slot])

    def wait_tile(s):
        pltpu.make_async_copy(y_hbm.at[pl.ds(0, rows)], ybuf.at[s], sem.at[s]).wait()

    @pl.when(i == 0)
    def _():
        def body(r, c):
            row_copy(pos_ref[r], 0, r).start()
            row_copy(pos_n1_ref[r], 1, r).start()
            return c
        lax.fori_loop(0, rows, body, 0)

    wait_tile(slot)
    gate1 = route_ref[:, 2:3]
    gate2 = route_ref[:, 3:4]
    f = (_unpack_bf16_pairs(ybuf[slot, 0:tm, :]) * gate1
         + _unpack_bf16_pairs(ybuf[slot, tm:rows, :]) * gate2)
    ahead = (i + GATHER_AHEAD) % GATHER_SLOTS
    for r in range(rows):
        row_copy(pos_n2_ref[r], ahead, r).start(priority=r % 2)
    out = _layernorm(ALPHA * h_ref[...] + f, g2_ref[...], b2_ref[...])

    @pl.when(i < n_prompt_tiles)
    def _():
        op_ref[...] = out

    @pl.when(i >= n_prompt_tiles)
    def _():
        os_ref[...] = out

    @pl.when(i == n - 1)
    def _():
        wait_tile((i + 1) % GATHER_SLOTS)
        wait_tile(ahead)


def _combine(y_rows, pos_tiles, h_all, route, pw, *, tm, t_prompt):
    T = h_all.shape[0]
    n = T // tm
    npt = t_prompt // tm
    rows = TOP_K * tm
    return pl.pallas_call(
        functools.partial(_combine_kernel, tm=tm, n_prompt_tiles=npt),
        grid=(n,),
        in_specs=[
            pl.BlockSpec((rows,), lambda i: (i,), memory_space=pltpu.SMEM),
            pl.BlockSpec((rows,), lambda i: (jnp.minimum(i + 1, n - 1),), memory_space=pltpu.SMEM),
            pl.BlockSpec((rows,), lambda i: (jnp.minimum(i + GATHER_AHEAD, n - 1),), memory_space=pltpu.SMEM),
            pl.BlockSpec(memory_space=pl.ANY),
            pl.BlockSpec((tm, D_MODEL), lambda i: (i, 0)),
            pl.BlockSpec((tm, LANES), lambda i: (i, 0)),
            _const_spec((1, D_MODEL)), _const_spec((1, D_MODEL)),
        ],
        out_specs=[pl.BlockSpec((tm, D_MODEL), lambda i: (jnp.minimum(i, npt - 1), 0)),
                   pl.BlockSpec((tm, D_MODEL), lambda i: (jnp.maximum(i - npt, 0), 0))],
        out_shape=[jax.ShapeDtypeStruct((t_prompt, D_MODEL), F32),
                   jax.ShapeDtypeStruct((T - t_prompt, D_MODEL), F32)],
        scratch_shapes=[pltpu.VMEM((GATHER_SLOTS, rows, D_MODEL // 2), jnp.uint32),
                        pltpu.SemaphoreType.DMA((GATHER_SLOTS,))],
        compiler_params=pltpu.CompilerParams(dimension_semantics=("arbitrary",),
                                             vmem_limit_bytes=VMEM_LIMIT),
        name="combine",
    )(pos_tiles, pos_tiles, pos_tiles, y_rows, h_all, route, pw["ln2_g"], pw["ln2_b"])


def _prep_weights(ln0_g, ln0_b, w_in, q_norm, w_uq, kv_norm, w_uk, w_uv, conv_w, conv_b, dt_bias, a_log,
                  d_skip, ssm_norm, w_o, ln1_g, ln1_b, w_rg, b_rg, w_re, b_re, ln2_g, ln2_b):
    s_q, s_kv, s_kr, s_z, s_xbc = Q_LORA, Q_LORA + KV_LORA, Q_LORA + KV_LORA + QK_ROPE, \
        Q_LORA + KV_LORA + QK_ROPE + D_SSM, Q_LORA + KV_LORA + QK_ROPE + D_SSM + CONV_CH
    zc = lambda n: jnp.zeros((D_MODEL, n), F32)
    w_in_p = jnp.concatenate([
        w_in[:, :s_q], w_in[:, s_q:s_kv], w_in[:, s_kr:s_z], w_in[:, s_z:s_xbc],
        w_in[:, s_kv:s_kr], zc(LANES - QK_ROPE), w_in[:, s_xbc:], zc(LANES - SSM_HEADS)], axis=1)
    wq = w_uq.reshape(Q_LORA, MLA_HEADS, QK_DIM) * (ATTN_SCALE * math.log2(math.e))
    wq = jnp.concatenate([wq, jnp.zeros((Q_LORA, MLA_HEADS, QK_PAD - QK_DIM), F32)], axis=2)
    w_r = jnp.concatenate([w_rg, w_re, jnp.zeros((D_MODEL, LANES - R_OFF - N_EXPERTS), F32)], axis=1)
    w_r_hi = w_r.astype(BF16)
    row = lambda v: v.reshape(1, -1)
    pad_row = lambda v: jnp.pad(v, (0, LANES - v.shape[0])).reshape(1, LANES)
    return dict(
        ln0_g=row(ln0_g), ln0_b=row(ln0_b), w_in=w_in_p.astype(BF16),
        q_norm=row(q_norm), kv_norm=row(kv_norm),
        w_uq=wq.reshape(Q_LORA, MLA_HEADS * QK_PAD).astype(BF16),
        w_ukv=jnp.concatenate([w_uk.reshape(KV_LORA, D_ATTN), w_uv.reshape(KV_LORA, D_ATTN)],
                              axis=1).astype(BF16),
        w_ukt=jnp.transpose(w_uk, (1, 2, 0)).astype(BF16),
        w_uvh=jnp.transpose(w_uv, (1, 0, 2)).astype(BF16),
        conv_w=conv_w, conv_b=row(conv_b), dt_bias=pad_row(dt_bias),
        a_neg=pad_row(-jnp.exp(a_log)), d_skip=row(jnp.repeat(d_skip, SSM_HEAD_DIM)),
        ssm_norm=row(ssm_norm),
        w_o_a=w_o[:D_ATTN].astype(BF16), w_o_s=w_o[D_ATTN:].astype(BF16),
        ln1_g=row(ln1_g), ln1_b=row(ln1_b),
        w_r=jnp.concatenate([w_r_hi, (w_r - w_r_hi.astype(F32)).astype(BF16)], axis=1),
        b_r=pad_row(jnp.concatenate([b_rg, b_re])),
        ln2_g=row(ln2_g), ln2_b=row(ln2_b),
    )


def _rope_tables(pos):
    half = QK_ROPE // 2
    inv_freq = ROPE_THETA ** (-jnp.arange(half, dtype=F32) / half)
    ang = pos.astype(F32)[:, None] * inv_freq[None, :]
    cos, sin = jnp.cos(ang), jnp.sin(ang)
    zeros = jnp.zeros((pos.shape[0], LANES - QK_ROPE), F32)
    return (jnp.concatenate([cos, cos, zeros], axis=1), jnp.concatenate([-sin, sin, zeros], axis=1))


def _dispatch_tables(route, counts, *, rb, n_blk):
    experts = jnp.arange(N_EXPERTS, dtype=jnp.int32)
    e = route[:, 0:TOP_K].astype(jnp.int32)
    rank = route[:, 4:4 + TOP_K].astype(jnp.int32)
    counts = counts.astype(jnp.int32)
    pcounts = (counts + rb - 1) // rb * rb
    pend = jnp.cumsum(pcounts)
    pstart = pend - pcounts
    pos = jnp.sum(jnp.where(e[:, :, None] == experts, pstart, 0), axis=2) + rank
    blk_start = jnp.arange(n_blk, dtype=jnp.int32) * rb
    blk_e = jnp.minimum(jnp.sum((pend[None, :] <= blk_start[:, None]).astype(jnp.int32), axis=1),
                        N_EXPERTS - 1)
    n_used = (pend[-1] // rb).reshape(1)
    blk_idx = jnp.arange(n_blk, dtype=jnp.int32)
    later = (blk_idx[None, :] > blk_idx[:, None]) & (blk_idx[None, :] < n_used) & (blk_e[None, :] > blk_e[:, None])
    next_e = jnp.min(jnp.where(later, blk_e[None, :], N_EXPERTS), axis=1)
    next_e = jnp.where(next_e < N_EXPERTS, next_e, -1).astype(jnp.int32)
    pad_base = pstart + counts
    pad_n = pcounts - counts
    return blk_e, n_used, next_e, pos, pad_base, pad_n


def _largest_tile(n, cap):
    t = cap
    while n % t:
        t //= 2
    return t


def kernel(x_prompt, x_sample, cache_kv_latent, cache_k_rope, state_conv, state_ssm, ln0_g, ln0_b, w_in, q_norm, w_uq, kv_norm, w_uk, w_uv, conv_w, conv_b, dt_bias, a_log, d_skip, ssm_norm, w_o, ln1_g, ln1_b, w_rg, b_rg, w_re, b_re, w_gate, w_up, w_down, ln2_g, ln2_b):
    B, L, _ = x_prompt.shape
    Bs, Ls, _ = x_sample.shape
    past = cache_kv_latent.shape[2]
    Tp, Ts = B * L, Bs * Ls
    pw = _prep_weights(ln0_g, ln0_b, w_in[0], q_norm[0], w_uq[0], kv_norm[0], w_uk[0], w_uv[0], conv_w[0],
                       conv_b[0], dt_bias[0], a_log[0], d_skip[0], ssm_norm[0], w_o[0], ln1_g[0], ln1_b[0],
                       w_rg[0], b_rg[0], w_re[0], b_re[0], ln2_g[0], ln2_b[0])

    tm_p = _largest_tile(L, 512)
    xp = x_prompt.reshape(Tp, D_MODEL)
    cos_p, sin_p = _rope_tables(jnp.arange(L, dtype=jnp.int32))
    q_p, kv_p, lat_p, kr_p, krp_p, z_p, xbc_p, dt_p = _inproj(
        xp, pw, cos_p, sin_p, tm=tm_p, pos_blocks=L // tm_p, with_kv=True)
    attn_p = _attn_prompt(q_p.reshape(B, L, -1), kv_p.reshape(B, L, -1), krp_p.reshape(B, L, -1),
                          tq=_largest_tile(L, 512), tk=_largest_tile(L, 1024))
    q_ssd = min(CHUNK, L)
    y_p, conv_p, ssm_p = _ssd(
        xbc_p.reshape(B, L, -1), dt_p.reshape(B, L, -1), z_p.reshape(B, L, -1),
        jnp.zeros((B, CONV_K - 1, CONV_CH), F32), jnp.zeros((B, D_SSM, D_STATE), F32), pw,
        q=q_ssd, tt=_largest_tile(L, 512))

    xs = x_sample.reshape(Ts, D_MODEL)
    cos_s, sin_s = _rope_tables(past + jnp.arange(Ls, dtype=jnp.int32))
    tm_s = _largest_tile(Ts, 256)
    assert tm_s % Ls == 0
    q_s, _, lat_s, kr_s, krp_s, z_s, xbc_s, dt_s = _inproj(
        xs, pw, jnp.tile(cos_s, (tm_s // Ls, 1)), jnp.tile(sin_s, (tm_s // Ls, 1)),
        tm=tm_s, pos_blocks=1, with_kv=False)
    attn_s = _attn_sample(q_s.reshape(Bs, Ls, -1), pw, cache_kv_latent[0], cache_k_rope[0],
                          lat_s.reshape(Bs, Ls, -1), krp_s.reshape(Bs, Ls, -1),
                          tk=_largest_tile(past, 1024))
    y_s, conv_s, ssm_s = _ssd(
        xbc_s.reshape(Bs, Ls, -1), dt_s.reshape(Bs, Ls, -1), z_s.reshape(Bs, Ls, -1),
        state_conv[0], state_ssm[0].reshape(Bs, D_SSM, D_STATE), pw, q=min(CHUNK, Ls), tt=Ls)

    T = Tp + Ts
    tm_c = _largest_tile(math.gcd(Tp, Ts), 256)
    h_all, hpk_all, route, counts = _outproj(xp, xs, attn_p.reshape(Tp, -1), attn_s.reshape(Ts, -1),
                                             y_p.reshape(Tp, -1), y_s.reshape(Ts, -1), pw, tm=tm_c)
    rb = 256
    n_blk = (T * TOP_K + N_EXPERTS * (rb - 1) + rb - 1) // rb
    blk_e, n_used, next_e, pos, pad_base, pad_n = _dispatch_tables(
        route, counts[0, R_OFF:R_OFF + N_EXPERTS], rb=rb, n_blk=n_blk)
    pos_tiles = pos.reshape(T // tm_c, tm_c, TOP_K).transpose(0, 2, 1).reshape(-1)
    xs_rows = _dispatch(hpk_all, pos_tiles, pad_base, pad_n, n_used, tm=tm_c, rb=rb, n_blk=n_blk)
    y_rows = _experts(xs_rows, blk_e, n_used, next_e, w_gate[0], w_up[0], w_down[0], rb=rb)
    out_p, out_s = _combine(y_rows, pos_tiles, h_all, route, pw, tm=tm_c, t_prompt=Tp)

    return (out_p.reshape(B, L, D_MODEL), out_s.reshape(Bs, Ls, D_MODEL),
            lat_p.reshape(1, B, L, KV_LORA), lat_s.reshape(1, Bs, Ls, KV_LORA),
            kr_p.reshape(1, B, L, QK_ROPE), kr_s.reshape(1, Bs, Ls, QK_ROPE),
            conv_p[None], conv_s[None],
            ssm_p.reshape(1, B, SSM_HEADS, SSM_HEAD_DIM, D_STATE),
            ssm_s.reshape(1, Bs, SSM_HEADS, SSM_HEAD_DIM, D_STATE))
```

```python
import functools
import math

import jax
import jax.numpy as jnp
import numpy as np
from jax import lax
from jax.experimental import pallas as pl
from jax.experimental.pallas import tpu as pltpu

F32 = jnp.float32
BF16 = jnp.bfloat16

D_MODEL = 2048
CHUNK = 64
MLA_HEADS = 8
QK_NOPE = 128
QK_ROPE = 64
QK_DIM = QK_NOPE + QK_ROPE
V_DIM = 128
Q_LORA = 512
KV_LORA = 512
ROPE_THETA = 10000.0
ATTN_SCALE = QK_DIM ** -0.5
D_ATTN = MLA_HEADS * V_DIM
D_SSM = 1024
SSM_HEAD_DIM = 64
SSM_HEADS = D_SSM // SSM_HEAD_DIM
SSM_GROUPS = 2
HEADS_PER_GROUP = SSM_HEADS // SSM_GROUPS
D_STATE = 128
CONV_K = 4
CONV_CH = D_SSM + 2 * SSM_GROUPS * D_STATE
N_EXPERT_GROUPS = 8
EXPERTS_PER_GROUP = 8
N_EXPERTS = N_EXPERT_GROUPS * EXPERTS_PER_GROUP
TOP_K = 2
D_EXPERT = 512
DEPTH = 1
ALPHA = (2 * DEPTH) ** 0.25
RMS_EPS = 1e-6
LN_EPS = 1e-5

LANES = 128
MXU_DIM = 256
VMEM_LIMIT = 56 * 1024 * 1024

QK_PAD = 2 * LANES
GATHER_AHEAD = 2
GATHER_SLOTS = GATHER_AHEAD + 1
C_Q, C_KV, C_Z, C_XBC, C_KR, C_DT = 0, 512, 1024, 2048, 3584, 3712
D_IN_PAD = 3840


def _const_spec(shape):
    nd = len(shape)
    return pl.BlockSpec(shape, lambda *_: (0,) * nd, pipeline_mode=pl.Buffered(1))


def _split3(a):
    hi = a.astype(BF16)
    r1 = a - hi.astype(F32)
    mid = r1.astype(BF16)
    lo = (r1 - mid.astype(F32)).astype(BF16)
    return hi, mid, lo


def _dot(a, b):
    return jnp.dot(a, b, preferred_element_type=F32)


def _dot_nt(a, b):
    return lax.dot_general(a, b, (((1,), (1,)), ((), ())), preferred_element_type=F32)


def _dot_tn(a, b):
    return lax.dot_general(a, b, (((0,), (0,)), ((), ())), preferred_element_type=F32)


def _exact_dot(a_f32, sel_bf16):
    hi, mid, lo = _split3(a_f32)
    return _dot(hi, sel_bf16) + _dot(mid, sel_bf16) + _dot(lo, sel_bf16)


def _exact_dot_l(sel_bf16, a_f32):
    hi, mid, lo = _split3(a_f32)
    return _dot(sel_bf16, hi) + _dot(sel_bf16, mid) + _dot(sel_bf16, lo)


def _layernorm(x, g, b):
    mu = jnp.mean(x, axis=-1, keepdims=True)
    xc = x - mu
    var = jnp.mean(xc * xc, axis=-1, keepdims=True)
    return xc * lax.rsqrt(var + LN_EPS) * g + b


def _rmsnorm(x, g):
    return x * lax.rsqrt(jnp.mean(x * x, axis=-1, keepdims=True) + RMS_EPS) * g


def _silu(x):
    return x * (1.0 / (1.0 + jnp.exp(-x)))


def _pack_bf16_pairs(x):
    n = x.shape[1] // 2
    bits = lax.bitcast_convert_type(x, jnp.uint32)
    return (bits[:, n:] & jnp.uint32(0xFFFF0000)) | (bits[:, :n] >> 16)


def _unpack_bf16_pairs(u):
    lo = lax.bitcast_convert_type(u << 16, F32)
    hi = lax.bitcast_convert_type(u & jnp.uint32(0xFFFF0000), F32)
    return jnp.concatenate([lo, hi], axis=1)


def _rope128(x, cos_t, sin_t):
    lane = lax.broadcasted_iota(jnp.int32, x.shape, 1)
    partner = jnp.where(lane < QK_ROPE // 2, pltpu.roll(x, LANES - QK_ROPE // 2, 1),
                        pltpu.roll(x, QK_ROPE // 2, 1))
    return x * cos_t + partner * sin_t


def _inproj_kernel(x_ref, g0_ref, b0_ref, w_in_ref, qn_ref, kvn_ref, w_uq_ref, w_ukv_ref,
                   cos_ref, sin_ref, dtb_ref,
                   q_ref, kv_ref, lat_ref, kr_ref, krp_ref, z_ref, xbc_ref, dt_ref, *, with_kv):
    xn = _layernorm(x_ref[...], g0_ref[...], b0_ref[...]).astype(BF16)
    cos_t = cos_ref[...]
    sin_t = sin_ref[...]

    c_q = _dot(xn, w_in_ref[:, C_Q:C_Q + Q_LORA])
    qb = _rmsnorm(c_q, qn_ref[...]).astype(BF16)
    for h in range(MLA_HEADS):
        qh = _dot(qb, w_uq_ref[:, h * QK_PAD:(h + 1) * QK_PAD])
        q_ref[:, h * QK_PAD:h * QK_PAD + LANES] = qh[:, :LANES].astype(BF16)
        q_ref[:, h * QK_PAD + LANES:(h + 1) * QK_PAD] = _rope128(qh[:, LANES:], cos_t, sin_t).astype(BF16)

    c_kv = _dot(xn, w_in_ref[:, C_KV:C_KV + KV_LORA])
    lat = _rmsnorm(c_kv, kvn_ref[...])
    lat_ref[...] = lat
    if with_kv:
        kv_ref[...] = _dot(lat.astype(BF16), w_ukv_ref[...]).astype(BF16)
    else:
        kv_ref[...] = jnp.zeros(kv_ref.shape, BF16)

    k_r = _rope128(_dot(xn, w_in_ref[:, C_KR:C_KR + LANES]), cos_t, sin_t)
    kr_ref[...] = k_r[:, :QK_ROPE]
    krp_ref[...] = k_r.astype(BF16)

    z_ref[...] = _dot(xn, w_in_ref[:, C_Z:C_Z + D_SSM]).astype(BF16)
    xbc_ref[...] = _dot(xn, w_in_ref[:, C_XBC:C_XBC + CONV_CH])

    dt_raw = _dot(xn, w_in_ref[:, C_DT:C_DT + LANES]) + dtb_ref[...]
    dt = jnp.maximum(dt_raw, 0.0) + jnp.log1p(jnp.exp(-jnp.abs(dt_raw)))
    lane = lax.broadcasted_iota(jnp.int32, dt.shape, 1)
    dt_ref[...] = jnp.where(lane < SSM_HEADS, dt, 0.0)


def _inproj(x, pw, cos_t, sin_t, *, tm, pos_blocks, with_kv):
    T = x.shape[0]
    n = T // tm
    row = lambda i: (i, 0)
    pos = lambda i: (i % pos_blocks, 0)
    kv_cols = 2 * D_ATTN if with_kv else LANES
    out_shape = [
        jax.ShapeDtypeStruct((T, MLA_HEADS * QK_PAD), BF16),
        jax.ShapeDtypeStruct((T, kv_cols), BF16),
        jax.ShapeDtypeStruct((T, KV_LORA), F32),
        jax.ShapeDtypeStruct((T, QK_ROPE), F32),
        jax.ShapeDtypeStruct((T, LANES), BF16),
        jax.ShapeDtypeStruct((T, D_SSM), BF16),
        jax.ShapeDtypeStruct((T, CONV_CH), F32),
        jax.ShapeDtypeStruct((T, LANES), F32),
    ]
    out_specs = [pl.BlockSpec((tm, s.shape[1]), row) for s in out_shape]
    in_specs = [
        pl.BlockSpec((tm, D_MODEL), row),
        _const_spec((1, D_MODEL)), _const_spec((1, D_MODEL)),
        _const_spec((D_MODEL, D_IN_PAD)),
        _const_spec((1, Q_LORA)), _const_spec((1, KV_LORA)),
        _const_spec((Q_LORA, MLA_HEADS * QK_PAD)),
        _const_spec((KV_LORA, 2 * D_ATTN)),
        pl.BlockSpec((tm, LANES), pos), pl.BlockSpec((tm, LANES), pos),
        _const_spec((1, LANES)),
    ]
    return pl.pallas_call(
        functools.partial(_inproj_kernel, with_kv=with_kv),
        grid=(n,), in_specs=in_specs, out_specs=out_specs, out_shape=out_shape,
        compiler_params=pltpu.CompilerParams(dimension_semantics=("arbitrary",),
                                             vmem_limit_bytes=VMEM_LIMIT),
        name="inproj",
    )(x, pw["ln0_g"], pw["ln0_b"], pw["w_in"], pw["q_norm"], pw["kv_norm"], pw["w_uq"],
      pw["w_ukv"], cos_t, sin_t, pw["dt_bias"])


def _attn_prompt_kernel(qi_ref, ki_ref, q_ref, kv_ref, krp_ref, o_ref, *scratch, tq, tk):
    m_refs = scratch[0:MLA_HEADS]
    acc_refs = scratch[MLA_HEADS:2 * MLA_HEADS]
    p_id = pl.program_id(1)
    qi = qi_ref[p_id]
    ki = ki_ref[p_id]
    k_last = ((qi + 1) * tq - 1) // tk

    @pl.when(ki == 0)
    def _():
        for h in range(MLA_HEADS):
            m_refs[h][...] = jnp.full(m_refs[h].shape, -jnp.inf, F32)
            acc_refs[h][...] = jnp.zeros(acc_refs[h].shape, F32)

    def step(masked):
        krp = krp_ref[...]
        ones_col = (lax.broadcasted_iota(jnp.int32, (tk, LANES), 1) == 0).astype(BF16)
        if masked:
            r = (qi * tq + lax.broadcasted_iota(jnp.int32, (tq, tk), 0)) // CHUNK
            c = (ki * tk + lax.broadcasted_iota(jnp.int32, (tq, tk), 1)) // CHUNK
            allowed = c <= r
        for h in range(MLA_HEADS):
            qh = q_ref[:, h * QK_PAD:(h + 1) * QK_PAD]
            kh = jnp.concatenate([kv_ref[:, h * QK_NOPE:(h + 1) * QK_NOPE], krp], axis=1)
            s = _dot_nt(qh, kh)
            if masked:
                s = jnp.where(allowed, s, -jnp.inf)
            m_old = m_refs[h][...]
            m_new = jnp.maximum(m_old, jnp.max(s, axis=1, keepdims=True))
            alpha = jnp.exp2(m_old - m_new)
            p = jnp.exp2(s - m_new).astype(BF16)
            v_aug = jnp.concatenate([kv_ref[:, D_ATTN + h * V_DIM:D_ATTN + (h + 1) * V_DIM], ones_col], axis=1)
            acc_refs[h][...] = alpha * acc_refs[h][...] + _dot(p, v_aug)
            m_refs[h][...] = m_new

    @pl.when(ki < k_last)
    def _():
        step(False)

    @pl.when(ki == k_last)
    def _():
        step(True)
        for h in range(MLA_HEADS):
            acc = acc_refs[h][...]
            o_ref[:, h * V_DIM:(h + 1) * V_DIM] = (acc[:, :V_DIM] / acc[:, V_DIM:V_DIM + 1]).astype(BF16)


def _attn_prompt(q, kv, krp, *, tq, tk):
    B, L, _ = q.shape
    nq = L // tq
    pairs = [(i, j) for i in range(nq) for j in range(((i + 1) * tq - 1) // tk + 1)]
    qi_tab = jnp.asarray(np.array([p[0] for p in pairs], np.int32))
    ki_tab = jnp.asarray(np.array([p[1] for p in pairs], np.int32))
    grid_spec = pltpu.PrefetchScalarGridSpec(
        num_scalar_prefetch=2,
        grid=(B, len(pairs)),
        in_specs=[
            pl.BlockSpec((None, tq, MLA_HEADS * QK_PAD), lambda b, p, qi, ki: (b, qi[p], 0)),
            pl.BlockSpec((None, tk, 2 * D_ATTN), lambda b, p, qi, ki: (b, ki[p], 0)),
            pl.BlockSpec((None, tk, LANES), lambda b, p, qi, ki: (b, ki[p], 0)),
        ],
        out_specs=pl.BlockSpec((None, tq, D_ATTN), lambda b, p, qi, ki: (b, qi[p], 0)),
        scratch_shapes=([pltpu.VMEM((tq, 1), F32)] * MLA_HEADS
                        + [pltpu.VMEM((tq, 2 * V_DIM), F32)] * MLA_HEADS),
    )
    return pl.pallas_call(
        functools.partial(_attn_prompt_kernel, tq=tq, tk=tk),
        grid_spec=grid_spec,
        out_shape=jax.ShapeDtypeStruct((B, L, D_ATTN), BF16),
        compiler_params=pltpu.CompilerParams(dimension_semantics=("arbitrary", "arbitrary"),
                                             vmem_limit_bytes=VMEM_LIMIT),
        name="attn_prompt",
    )(qi_tab, ki_tab, q, kv, krp)


def _attn_sample_kernel(q_ref, wukt_ref, wuv_ref, latp_ref, krpast_ref, latn_ref, krn_ref, o_ref,
                        qlat_ref, qrp_ref, m_ref, l_ref, acc_ref, *, ls, past, tk, n_new_pad):
    k = pl.program_id(1)
    nk = pl.num_programs(1)
    rows = MLA_HEADS * ls

    @pl.when(k == 0)
    def _():
        for h in range(MLA_HEADS):
            qn = q_ref[:, h * QK_PAD:h * QK_PAD + LANES]
            qlat_ref[h * ls:(h + 1) * ls, :] = _dot(qn, wukt_ref[h]).astype(BF16)
            qrp_ref[h * ls:(h + 1) * ls, :] = q_ref[:, h * QK_PAD + LANES:(h + 1) * QK_PAD]
        m_ref[...] = jnp.full(m_ref.shape, -jnp.inf, F32)
        l_ref[...] = jnp.zeros(l_ref.shape, F32)
        acc_ref[...] = jnp.zeros(acc_ref.shape, F32)

    q_chunk = (past + lax.broadcasted_iota(jnp.int32, (rows, 1), 0) % ls) // CHUNK

    def update(s, lat_b):
        m_old = m_ref[...]
        m_new = jnp.maximum(m_old, jnp.max(s, axis=1, keepdims=True))
        alpha = jnp.exp2(m_old - m_new)
        p = jnp.exp2(s - m_new)
        l_ref[...] = alpha * l_ref[...] + jnp.sum(p, axis=1, keepdims=True)
        acc_ref[...] = alpha * acc_ref[...] + _dot(p.astype(BF16), lat_b)
        m_ref[...] = m_new

    lat_b = latp_ref[...].astype(BF16)
    kr_b = krpast_ref[...].astype(BF16)
    s = _dot_nt(qlat_ref[...], lat_b) + _dot_nt(qrp_ref[:, :QK_ROPE], kr_b)
    k_pos = k * tk + lax.broadcasted_iota(jnp.int32, (rows, tk), 1)
    s = jnp.where(k_pos // CHUNK <= q_chunk, s, -jnp.inf)
    update(s, lat_b)

    @pl.when(k == nk - 1)
    def _():
        latn_b = latn_ref[...].astype(BF16)
        s2 = _dot_nt(qlat_ref[...], latn_b) + _dot_nt(qrp_ref[...], krn_ref[...])
        j = lax.broadcasted_iota(jnp.int32, (rows, n_new_pad), 1)
        ok = ((past + j) // CHUNK <= q_chunk) & (j < ls)
        update(jnp.where(ok, s2, -jnp.inf), latn_b)
        o_lat = (acc_ref[...] / l_ref[...]).astype(BF16)
        for h in range(MLA_HEADS):
            o_ref[:, h * V_DIM:(h + 1) * V_DIM] = _dot(o_lat[h * ls:(h + 1) * ls, :], wuv_ref[h]).astype(BF16)


def _attn_sample(q, pw, lat_past, kr_past, lat_new, krp_new, *, tk):
    Bs, Ls, _ = q.shape
    past = lat_past.shape[1]
    n_new_pad = LANES
    lat_new = jnp.pad(lat_new, ((0, 0), (0, n_new_pad - Ls), (0, 0)))
    krp_new = jnp.pad(krp_new, ((0, 0), (0, n_new_pad - Ls), (0, 0)))
    rows = MLA_HEADS * Ls
    return pl.pallas_call(
        functools.partial(_attn_sample_kernel, ls=Ls, past=past, tk=tk, n_new_pad=n_new_pad),
        grid=(Bs, past // tk),
        in_specs=[
            pl.BlockSpec((None, Ls, MLA_HEADS * QK_PAD), lambda b, k: (b, 0, 0)),
            _const_spec((MLA_HEADS, QK_NOPE, KV_LORA)),
            _const_spec((MLA_HEADS, KV_LORA, V_DIM)),
            pl.BlockSpec((None, tk, KV_LORA), lambda b, k: (b, k, 0)),
            pl.BlockSpec((None, tk, QK_ROPE), lambda b, k: (b, k, 0)),
            pl.BlockSpec((None, n_new_pad, KV_LORA), lambda b, k: (b, 0, 0)),
            pl.BlockSpec((None, n_new_pad, LANES), lambda b, k: (b, 0, 0)),
        ],
        out_specs=pl.BlockSpec((None, Ls, D_ATTN), lambda b, k: (b, 0, 0)),
        out_shape=jax.ShapeDtypeStruct((Bs, Ls, D_ATTN), BF16),
        scratch_shapes=[pltpu.VMEM((rows, KV_LORA), BF16), pltpu.VMEM((rows, LANES), BF16),
                        pltpu.VMEM((rows, 1), F32), pltpu.VMEM((rows, 1), F32),
                        pltpu.VMEM((rows, KV_LORA), F32)],
        compiler_params=pltpu.CompilerParams(dimension_semantics=("arbitrary", "arbitrary"),
                                             vmem_limit_bytes=VMEM_LIMIT),
        name="attn_sample",
    )(q, pw["w_ukt"], pw["w_uvh"], lat_past, kr_past, lat_new, krp_new)


def _ssd_consts(q):
    hb = MXU_DIM // q
    hq = SSM_HEADS * q
    lane = np.arange(hq)
    ltri = (np.arange(q)[None, :] <= np.arange(q)[:, None]).astype(np.float32)
    sel_c = (np.arange(LANES)[:, None] == (lane // q)[None, :]).astype(np.float32)
    sel_p = (np.arange(LANES)[:, None] == (np.arange(D_SSM) // SSM_HEAD_DIM)[None, :]).astype(np.float32)
    diag = (np.arange(q)[:, None] == (lane % q)[None, :]).astype(np.float32)
    causal = (np.arange(q)[:, None] >= (lane % q)[None, :]).astype(np.float32)
    bd_rows = np.arange(hb * q) // q
    bd_cols = np.arange(hb * SSM_HEAD_DIM) // SSM_HEAD_DIM
    bdmask = (bd_rows[:, None] == bd_cols[None, :]).astype(np.float32)
    return dict(ltri=jnp.asarray(ltri, BF16), sel_c=jnp.asarray(sel_c, BF16),
                sel_p=jnp.asarray(sel_p, BF16), diag=jnp.asarray(diag, F32),
                causal=jnp.asarray(causal, F32), bdmask=jnp.asarray(bdmask, BF16))


def _ssd_kernel(xbc_ref, dt_ref, z_ref, cprev_ref, h0_ref, cw_ref, cb_ref, a_ref, dskip_ref, norm_ref,
                ltri_ref, selc_ref, selp_ref, diag_ref, causal_ref, bdmask_ref,
                y_ref, cout_ref, hout_ref,
                ext_ref, u_ref, ht_ref, *, q, tt):
    t = pl.program_id(1)
    nt = pl.num_programs(1)
    hb = MXU_DIM // q
    nblk = SSM_HEADS // hb
    gw = HEADS_PER_GROUP * SSM_HEAD_DIM
    pad = 8

    @pl.when(t == 0)
    def _():
        ext_ref[0:pad, :] = jnp.zeros((pad, CONV_CH), F32)
        ext_ref[pad - (CONV_K - 1):pad, :] = cprev_ref[...]
        ht_ref[...] = h0_ref[...].T

    xbc = xbc_ref[...]
    ext_ref[pad:pad + tt, :] = xbc
    cout_ref[...] = xbc[tt - (CONV_K - 1):, :]
    conv = cb_ref[...] + cw_ref[0:1, :] * ext_ref[pad - 3:pad - 3 + tt, :]
    for k in range(1, CONV_K):
        conv = conv + cw_ref[k:k + 1, :] * ext_ref[pad - 3 + k:pad - 3 + k + tt, :]
    u_ref[...] = _silu(conv)
    ext_ref[pad - (CONV_K - 1):pad, :] = xbc[tt - (CONV_K - 1):, :]

    a_row = a_ref[...]
    d_row = dskip_ref[...]
    norm_row = norm_ref[...]

    def chunk(c, carry):
        r0 = pl.multiple_of(c * q, q)
        u = u_ref[pl.ds(r0, q), :]
        xs = u[:, :D_SSM]
        dt = dt_ref[pl.ds(r0, q), :]
        a_cs = _exact_dot_l(ltri_ref[...], dt * a_row)
        ap = _exact_dot(a_cs, selp_ref[...])
        dtp = _exact_dot(dt, selp_ref[...])
        a_i = ap if q == SSM_HEAD_DIM else _exact_dot(a_cs, selc_ref[...])
        a_j = jnp.sum(a_i * diag_ref[...], axis=0, keepdims=True)
        decay = jnp.exp(jnp.where(causal_ref[...] > 0.5, a_i - a_j, -jnp.inf))
        ap_last = ap[q - 1:q, :]
        xdt = (xs * dtp).astype(BF16)
        xw = (xs * (jnp.exp(ap_last - ap) * dtp)).astype(BF16)
        e_ap = jnp.exp(ap)

        cb_parts, y_off_parts = [], []
        for g in range(SSM_GROUPS):
            bm = u[:, D_SSM + g * D_STATE:D_SSM + (g + 1) * D_STATE].astype(BF16)
            cm = u[:, D_SSM + (SSM_GROUPS + g) * D_STATE:D_SSM + (SSM_GROUPS + g + 1) * D_STATE].astype(BF16)
            cb_parts.append(_dot_nt(cm, jnp.concatenate([bm] * HEADS_PER_GROUP, axis=0)))
            gl = slice(g * gw, (g + 1) * gw)
            ht_g = ht_ref[:, gl]
            y_off_parts.append(_dot(cm, ht_g.astype(BF16)) * e_ap[:, gl])
            st = _dot_tn(bm, xw[:, gl])
            ht_ref[:, gl] = ht_g * jnp.exp(ap_last[:, gl]) + st
        w_all = (jnp.concatenate(cb_parts, axis=1) * decay).astype(BF16)
        y_diag_parts = []
        for b in range(nblk):
            x_b = xdt[:, b * hb * SSM_HEAD_DIM:(b + 1) * hb * SSM_HEAD_DIM]
            bd = jnp.concatenate([x_b] * hb, axis=0) * bdmask_ref[...]
            y_diag_parts.append(_dot(w_all[:, b * MXU_DIM:(b + 1) * MXU_DIM], bd))
        y = jnp.concatenate(y_diag_parts, axis=1) + jnp.concatenate(y_off_parts, axis=1)

        y = y + d_row * xs
        y = y * _silu(z_ref[pl.ds(r0, q), :].astype(F32))
        outs = []
        for g in range(SSM_GROUPS):
            yg = y[:, g * gw:(g + 1) * gw]
            outs.append(yg * lax.rsqrt(jnp.mean(yg * yg, axis=1, keepdims=True) + RMS_EPS))
        y_ref[pl.ds(r0, q), :] = (jnp.concatenate(outs, axis=1) * norm_row).astype(BF16)
        return carry

    lax.fori_loop(0, tt // q, chunk, 0, unroll=True)

    @pl.when(t == nt - 1)
    def _():
        hout_ref[...] = ht_ref[...].T


def _ssd(xbc, dt, z, conv_prev, h0, pw, *, q, tt):
    B, L, _ = xbc.shape
    cs = _ssd_consts(q)
    hq = SSM_HEADS * q
    hb = MXU_DIM // q
    tile = lambda w: pl.BlockSpec((None, tt, w), lambda b, t: (b, t, 0))
    per_b = lambda s: pl.BlockSpec((None,) + s, lambda b, t: (b, 0, 0))
    return pl.pallas_call(
        functools.partial(_ssd_kernel, q=q, tt=tt),
        grid=(B, L // tt),
        in_specs=[tile(CONV_CH), tile(LANES), tile(D_SSM), per_b((CONV_K - 1, CONV_CH)),
                  per_b((D_SSM, D_STATE)),
                  _const_spec((CONV_K, CONV_CH)), _const_spec((1, CONV_CH)), _const_spec((1, LANES)),
                  _const_spec((1, D_SSM)), _const_spec((1, D_SSM)),
                  _const_spec((q, q)), _const_spec((LANES, hq)), _const_spec((LANES, D_SSM)),
                  _const_spec((q, hq)), _const_spec((q, hq)),
                  _const_spec((MXU_DIM, hb * SSM_HEAD_DIM))],
        out_specs=[tile(D_SSM), per_b((CONV_K - 1, CONV_CH)), per_b((D_SSM, D_STATE))],
        out_shape=[jax.ShapeDtypeStruct((B, L, D_SSM), BF16),
                   jax.ShapeDtypeStruct((B, CONV_K - 1, CONV_CH), F32),
                   jax.ShapeDtypeStruct((B, D_SSM, D_STATE), F32)],
        scratch_shapes=[pltpu.VMEM((8 + tt, CONV_CH), F32), pltpu.VMEM((tt, CONV_CH), F32),
                        pltpu.VMEM((D_STATE, D_SSM), F32)],
        compiler_params=pltpu.CompilerParams(dimension_semantics=("arbitrary", "arbitrary"),
                                             vmem_limit_bytes=VMEM_LIMIT),
        name="ssd",
    )(xbc, dt, z, conv_prev, h0, pw["conv_w"], pw["conv_b"], pw["a_neg"], pw["d_skip"], pw["ssm_norm"],
      cs["ltri"], cs["sel_c"], cs["sel_p"], cs["diag"], cs["causal"], cs["bdmask"])


R_OFF = N_EXPERT_GROUPS


def _outproj_kernel(xp_ref, xs_ref, attnp_ref, attns_ref, yp_ref, ys_ref, g0_ref, b0_ref, wo_a_ref, wo_s_ref,
                    g1_ref, b1_ref, wr_ref, br_ref, ltri_ref, h_ref, hpk_ref, route_ref, cnt_ref,
                    *, n_prompt_tiles, tm, n_groups):
    i = pl.program_id(0)
    is_p = i < n_prompt_tiles

    @pl.when(i == 0)
    def _():
        cnt_ref[...] = jnp.zeros(cnt_ref.shape, F32)

    cnt = cnt_ref[0:1, :]
    for g in range(n_groups):
        rows = slice(g * (tm // n_groups), (g + 1) * (tm // n_groups))
        refs = (xp_ref, xs_ref, attnp_ref, attns_ref, yp_ref, ys_ref)
        cnt = _outproj_rows(rows, is_p, cnt, refs, g0_ref, b0_ref, wo_a_ref, wo_s_ref, g1_ref, b1_ref,
                            wr_ref, br_ref, ltri_ref, h_ref, hpk_ref, route_ref)
    cnt_ref[...] = jnp.broadcast_to(cnt, cnt_ref.shape)


def _outproj_rows(rows, is_p, cnt, refs, g0_ref, b0_ref, wo_a_ref, wo_s_ref, g1_ref, b1_ref, wr_ref, br_ref,
                  ltri_ref, h_ref, hpk_ref, route_ref):
    xp_ref, xs_ref, attnp_ref, attns_ref, yp_ref, ys_ref = refs
    x = jnp.where(is_p, xp_ref[rows, :], xs_ref[rows, :])
    attn = jnp.where(is_p, attnp_ref[rows, :], attns_ref[rows, :])
    yssm = jnp.where(is_p, yp_ref[rows, :], ys_ref[rows, :])

    xn = _layernorm(x, g0_ref[...], b0_ref[...])
    mixed = _dot(attn, wo_a_ref[...]) + _dot(yssm, wo_s_ref[...])
    h = _layernorm(ALPHA * xn + mixed, g1_ref[...], b1_ref[...])
    h_ref[rows, :] = h

    h_hi = h.astype(BF16)
    h_hi32 = h_hi.astype(F32)
    hpk_ref[rows, :] = _pack_bf16_pairs(h_hi32)
    h_lo = (h - h_hi32).astype(BF16)
    tm = h.shape[0]
    prod = _dot(jnp.concatenate([h_hi, h_lo], axis=0), wr_ref[...])
    lg = (prod[:tm, :LANES] + prod[:tm, LANES:]) + (prod[tm:, :LANES] + prod[tm:, LANES:]) + br_ref[...]
    lane = lax.broadcasted_iota(jnp.int32, lg.shape, 1)
    big = jnp.int32(1 << 20)
    gl = jnp.where(lane < N_EXPERT_GROUPS, lg, -jnp.inf)
    gmax = jnp.max(gl, axis=1, keepdims=True)
    grp = jnp.min(jnp.where(gl == gmax, lane, big), axis=1, keepdims=True)
    g_w = 1.0 / jnp.sum(jnp.exp(gl - gmax), axis=1, keepdims=True)
    in_grp = (lane >= R_OFF) & (lane < R_OFF + N_EXPERTS) & ((lane - R_OFF) // EXPERTS_PER_GROUP == grp)
    el = jnp.where(in_grp, lg, -jnp.inf)
    emax = jnp.max(el, axis=1, keepdims=True)
    ee = jnp.exp(el - emax)
    prob = jnp.where(in_grp, ee / jnp.sum(ee, axis=1, keepdims=True), -1.0)
    p1 = jnp.max(prob, axis=1, keepdims=True)
    i1 = jnp.min(jnp.where(prob == p1, lane, big), axis=1, keepdims=True)
    prob2 = jnp.where(lane == i1, -1.0, prob)
    p2 = jnp.max(prob2, axis=1, keepdims=True)
    i2 = jnp.min(jnp.where(prob2 == p2, lane, big), axis=1, keepdims=True)
    denom = p1 + p2
    oh1 = (lane == i1).astype(F32)
    oh2 = (lane == i2).astype(F32)
    oh = oh1 + oh2
    before = _dot(ltri_ref[...], oh.astype(BF16)) + cnt
    rank1 = jnp.sum(before * oh1, axis=1, keepdims=True)
    rank2 = jnp.sum(before * oh2, axis=1, keepdims=True)
    route = jnp.where(lane == 0, (i1 - R_OFF).astype(F32),
                      jnp.where(lane == 1, (i2 - R_OFF).astype(F32),
                                jnp.where(lane == 2, g_w * p1 / denom,
                                          jnp.where(lane == 3, g_w * p2 / denom,
                                                    jnp.where(lane == 4, rank1,
                                                              jnp.where(lane == 5, rank2, 0.0))))))
    route_ref[rows, :] = route
    return cnt + jnp.sum(oh, axis=0, keepdims=True)


def _outproj(xp, xs, attn_p, attn_s, y_p, y_s, pw, *, tm):
    Tp, Ts = xp.shape[0], xs.shape[0]
    npt, nst = Tp // tm, Ts // tm
    T = Tp + Ts
    row = lambda i: (i, 0)
    prow = lambda i: (jnp.minimum(i, npt - 1), 0)
    srow = lambda i: (jnp.maximum(i - npt, 0), 0)
    n_groups = 1
    gm = tm // n_groups
    ltri = jnp.asarray(np.tril(np.ones((gm, gm), np.float32), -1), BF16)
    return pl.pallas_call(
        functools.partial(_outproj_kernel, n_prompt_tiles=npt, tm=tm, n_groups=n_groups),
        grid=(npt + nst,),
        in_specs=[pl.BlockSpec((tm, D_MODEL), prow), pl.BlockSpec((tm, D_MODEL), srow),
                  pl.BlockSpec((tm, D_ATTN), prow), pl.BlockSpec((tm, D_ATTN), srow),
                  pl.BlockSpec((tm, D_SSM), prow), pl.BlockSpec((tm, D_SSM), srow),
                  _const_spec((1, D_MODEL)), _const_spec((1, D_MODEL)),
                  _const_spec((D_ATTN, D_MODEL)), _const_spec((D_SSM, D_MODEL)),
                  _const_spec((1, D_MODEL)), _const_spec((1, D_MODEL)),
                  _const_spec((D_MODEL, 2 * LANES)), _const_spec((1, LANES)),
                  _const_spec((gm, gm))],
        out_specs=[pl.BlockSpec((tm, D_MODEL), row), pl.BlockSpec((tm, D_MODEL // 2), row),
                   pl.BlockSpec((tm, LANES), row), pl.BlockSpec((8, LANES), lambda i: (0, 0))],
        out_shape=[jax.ShapeDtypeStruct((T, D_MODEL), F32), jax.ShapeDtypeStruct((T, D_MODEL // 2), jnp.uint32),
                   jax.ShapeDtypeStruct((T, LANES), F32), jax.ShapeDtypeStruct((8, LANES), F32)],
        compiler_params=pltpu.CompilerParams(dimension_semantics=("arbitrary",),
                                             vmem_limit_bytes=VMEM_LIMIT),
        name="outproj",
    )(xp, xs, attn_p, attn_s, y_p, y_s, pw["ln0_g"], pw["ln0_b"], pw["w_o_a"], pw["w_o_s"],
      pw["ln1_g"], pw["ln1_b"], pw["w_r"], pw["b_r"], ltri)


def _dispatch_kernel(pad_base_ref, pad_n_ref, n_used_ref, pos_ref, hpk_ref, xs_hbm, sbuf, zbuf, sem,
                     *, tm, n_tiles, rb, n_blk):
    i = pl.program_id(0)
    rows = TOP_K * tm
    slot = i % 2
    pad_sem = 2

    def wait_tile(s):
        for _ in range(TOP_K):
            pltpu.make_async_copy(sbuf.at[s], xs_hbm.at[pl.ds(0, tm)], sem.at[s]).wait()

    @pl.when(i == 0)
    def _():
        zbuf[...] = jnp.zeros(zbuf.shape, jnp.uint32)

    @pl.when(i < n_tiles)
    def _():
        @pl.when(i >= 2)
        def _():
            wait_tile(slot)
        sbuf[slot] = hpk_ref[...]
        for a in range(rows):
            pltpu.make_async_copy(sbuf.at[slot, pl.ds(a % tm, 1)], xs_hbm.at[pl.ds(pos_ref[a], 1)],
                                  sem.at[slot]).start(priority=a % 2)

    @pl.when(i == n_tiles)
    def _():
        for t in range(max(n_tiles - 2, 0), n_tiles):
            wait_tile(t % 2)

        def tail_copy(b):
            r0 = pl.multiple_of(b * rb, rb)
            return pltpu.make_async_copy(zbuf, xs_hbm.at[pl.ds(r0, rb)], sem.at[pad_sem])

        def tail_start(b, c):
            tail_copy(b).start()
            return c

        def tail_wait(b, c):
            tail_copy(b).wait()
            return c
        lax.fori_loop(n_used_ref[0], n_blk, tail_start, 0)
        lax.fori_loop(n_used_ref[0], n_blk, tail_wait, 0)

    @pl.when(i >= n_tiles)
    def _():
        e = i - n_tiles
        base = pad_base_ref[e]

        def pad_copy(j):
            return pltpu.make_async_copy(zbuf.at[pl.ds(0, 1)], xs_hbm.at[pl.ds(base + j, 1)], sem.at[pad_sem])

        def start(j, c):
            pad_copy(j).start()
            return c

        def wait(j, c):
            pad_copy(j).wait()
            return c
        lax.fori_loop(0, pad_n_ref[e], start, 0)
        lax.fori_loop(0, pad_n_ref[e], wait, 0)


def _dispatch(hpk_all, pos_tiles, pad_base, pad_n, n_used, *, tm, rb, n_blk):
    T = hpk_all.shape[0]
    n_tiles = T // tm
    rows = TOP_K * tm
    grid_spec = pltpu.PrefetchScalarGridSpec(
        num_scalar_prefetch=3,
        grid=(n_tiles + N_EXPERTS,),
        in_specs=[pl.BlockSpec((rows,), lambda i, pb, pn, nu: (jnp.minimum(i, n_tiles - 1),),
                               memory_space=pltpu.SMEM),
                  pl.BlockSpec((tm, D_MODEL // 2), lambda i, pb, pn, nu: (jnp.minimum(i, n_tiles - 1), 0))],
        out_specs=pl.BlockSpec(memory_space=pl.ANY),
        scratch_shapes=[pltpu.VMEM((2, tm, D_MODEL // 2), jnp.uint32),
                        pltpu.VMEM((rb, D_MODEL // 2), jnp.uint32), pltpu.SemaphoreType.DMA((3,))],
    )
    return pl.pallas_call(
        functools.partial(_dispatch_kernel, tm=tm, n_tiles=n_tiles, rb=rb, n_blk=n_blk),
        grid_spec=grid_spec,
        out_shape=jax.ShapeDtypeStruct((n_blk * rb, D_MODEL // 2), jnp.uint32),
        compiler_params=pltpu.CompilerParams(dimension_semantics=("arbitrary",)),
        name="dispatch",
    )(pad_base, pad_n, n_used, pos_tiles, hpk_all)


def _experts_kernel(blk_e_ref, n_used_ref, next_e_ref, x_ref, wg_hbm, wu_hbm, wd_hbm, y_ref,
                    wg_s, wu_s, wd_s, wsem, wg_b, wu_b, wd_b, prev_e):
    i = pl.program_id(0)
    n_used = n_used_ref[0]

    def weight_copies(e):
        return (pltpu.make_async_copy(wg_hbm.at[e], wg_s, wsem.at[0]),
                pltpu.make_async_copy(wu_hbm.at[e], wu_s, wsem.at[1]),
                pltpu.make_async_copy(wd_hbm.at[e], wd_s, wsem.at[2]))

    @pl.when(i == 0)
    def _():
        prev_e[0] = -1
        for c in weight_copies(blk_e_ref[0]):
            c.start()

    @pl.when(i < n_used)
    def _():
        e = blk_e_ref[i]

        @pl.when(e != prev_e[0])
        def _():
            for c in weight_copies(e):
                c.wait()
            wg_b[...] = wg_s[...].astype(BF16)
            wu_b[...] = wu_s[...].astype(BF16)
            wd_b[...] = wd_s[...].astype(BF16)
            prev_e[0] = e
            nxt = next_e_ref[i]

            @pl.when(nxt >= 0)
            def _():
                for c in weight_copies(nxt):
                    c.start()

        xb = _unpack_bf16_pairs(x_ref[...]).astype(BF16)
        hid = (_silu(_dot(xb, wg_b[...])) * _dot(xb, wu_b[...])).astype(BF16)
        y = _dot(hid, wd_b[...])
        y_ref[...] = _pack_bf16_pairs(y.astype(BF16).astype(F32))

    @pl.when(i >= n_used)
    def _():
        y_ref[...] = jnp.zeros(y_ref.shape, jnp.uint32)


def _experts(xs_rows, blk_e, n_used, next_e, w_gate, w_up, w_down, *, rb):
    n_blk = blk_e.shape[0]
    n_rows = n_blk * rb
    hbm = pl.BlockSpec(memory_space=pl.ANY)
    grid_spec = pltpu.PrefetchScalarGridSpec(
        num_scalar_prefetch=3,
        grid=(n_blk,),
        in_specs=[pl.BlockSpec((rb, D_MODEL // 2), lambda i, be, nu, ne: (jnp.minimum(i, nu[0] - 1), 0)),
                  hbm, hbm, hbm],
        out_specs=pl.BlockSpec((rb, D_MODEL // 2), lambda i, be, nu, ne: (i, 0)),
        scratch_shapes=[pltpu.VMEM((D_MODEL, D_EXPERT), F32), pltpu.VMEM((D_MODEL, D_EXPERT), F32),
                        pltpu.VMEM((D_EXPERT, D_MODEL), F32), pltpu.SemaphoreType.DMA((3,)),
                        pltpu.VMEM((D_MODEL, D_EXPERT), BF16), pltpu.VMEM((D_MODEL, D_EXPERT), BF16),
                        pltpu.VMEM((D_EXPERT, D_MODEL), BF16), pltpu.SMEM((1,), jnp.int32)],
    )
    return pl.pallas_call(
        _experts_kernel,
        grid_spec=grid_spec,
        out_shape=jax.ShapeDtypeStruct((n_rows, D_MODEL // 2), jnp.uint32),
        compiler_params=pltpu.CompilerParams(dimension_semantics=("arbitrary",),
                                             vmem_limit_bytes=VMEM_LIMIT),
        name="experts",
    )(blk_e, n_used, next_e, xs_rows, w_gate, w_up, w_down)


def _combine_kernel(pos_ref, pos_n1_ref, pos_n2_ref, y_hbm, h_ref, route_ref, g2_ref, b2_ref, op_ref, os_ref,
                    ybuf, sem, *, tm, n_prompt_tiles):
    i = pl.program_id(0)
    n = pl.num_programs(0)
    slot = i % GATHER_SLOTS
    rows = TOP_K * tm

    def row_copy(idx, dst_slot, r):
        return pltpu.make_async_copy(y_hbm.at[pl.ds(idx, 1)], ybuf.at[dst_slot, pl.ds(r, 1)],
                                     sem.at[dst_slot])

    def wait_tile(s):
        pltpu.make_async_copy(y_hbm.at[pl.ds(0, rows)], ybuf.at[s], sem.at[s]).wait()

    @pl.when(i == 0)
    def _():
        def body(r, c):
            row_copy(pos_ref[r], 0, r).start()
            row_copy(pos_n1_ref[r], 1, r).start()
            return c
        lax.fori_loop(0, rows, body, 0)

    wait_tile(slot)
    gate1 = route_ref[:, 2:3]
    gate2 = route_ref[:, 3:4]
    f = (_unpack_bf16_pairs(ybuf[slot, 0:tm, :]) * gate1
         + _unpack_bf16_pairs(ybuf[slot, tm:rows, :]) * gate2)
    ahead = (i + GATHER_AHEAD) % GATHER_SLOTS
    for r in range(rows):
        row_copy(pos_n2_ref[r], ahead, r).start(priority=r % 2)
    out = _layernorm(ALPHA * h_ref[...] + f, g2_ref[...], b2_ref[...])

    @pl.when(i < n_prompt_tiles)
    def _():
        op_ref[...] = out

    @pl.when(i >= n_prompt_tiles)
    def _():
        os_ref[...] = out

    @pl.when(i == n - 1)
    def _():
        wait_tile((i + 1) % GATHER_SLOTS)
        wait_tile(ahead)


def _combine(y_rows, pos_tiles, h_all, route, pw, *, tm, t_prompt):
    T = h_all.shape[0]
    n = T // tm
    npt = t_prompt // tm
    rows = TOP_K * tm
    return pl.pallas_call(
        functools.partial(_combine_kernel, tm=tm, n_prompt_tiles=npt),
        grid=(n,),
        in_specs=[
            pl.BlockSpec((rows,), lambda i: (i,), memory_space=pltpu.SMEM),
            pl.BlockSpec((rows,), lambda i: (jnp.minimum(i + 1, n - 1),), memory_space=pltpu.SMEM),
            pl.BlockSpec((rows,), lambda i: (jnp.minimum(i + GATHER_AHEAD, n - 1),), memory_space=pltpu.SMEM),
            pl.BlockSpec(memory_space=pl.ANY),
            pl.BlockSpec((tm, D_MODEL), lambda i: (i, 0)),
            pl.BlockSpec((tm, LANES), lambda i: (i, 0)),
            _const_spec((1, D_MODEL)), _const_spec((1, D_MODEL)),
        ],
        out_specs=[pl.BlockSpec((tm, D_MODEL), lambda i: (jnp.minimum(i, npt - 1), 0)),
                   pl.BlockSpec((tm, D_MODEL), lambda i: (jnp.maximum(i - npt, 0), 0))],
        out_shape=[jax.ShapeDtypeStruct((t_prompt, D_MODEL), F32),
                   jax.ShapeDtypeStruct((T - t_prompt, D_MODEL), F32)],
        scratch_shapes=[pltpu.VMEM((GATHER_SLOTS, rows, D_MODEL // 2), jnp.uint32),
                        pltpu.SemaphoreType.DMA((GATHER_SLOTS,))],
        compiler_params=pltpu.CompilerParams(dimension_semantics=("arbitrary",),
                                             vmem_limit_bytes=VMEM_LIMIT),
        name="combine",
    )(pos_tiles, pos_tiles, pos_tiles, y_rows, h_all, route, pw["ln2_g"], pw["ln2_b"])


def _prep_weights(ln0_g, ln0_b, w_in, q_norm, w_uq, kv_norm, w_uk, w_uv, conv_w, conv_b, dt_bias, a_log,
                  d_skip, ssm_norm, w_o, ln1_g, ln1_b, w_rg, b_rg, w_re, b_re, ln2_g, ln2_b):
    s_q, s_kv, s_kr, s_z, s_xbc = Q_LORA, Q_LORA + KV_LORA, Q_LORA + KV_LORA + QK_ROPE, \
        Q_LORA + KV_LORA + QK_ROPE + D_SSM, Q_LORA + KV_LORA + QK_ROPE + D_SSM + CONV_CH
    zc = lambda n: jnp.zeros((D_MODEL, n), F32)
    w_in_p = jnp.concatenate([
        w_in[:, :s_q], w_in[:, s_q:s_kv], w_in[:, s_kr:s_z], w_in[:, s_z:s_xbc],
        w_in[:, s_kv:s_kr], zc(LANES - QK_ROPE), w_in[:, s_xbc:], zc(LANES - SSM_HEADS)], axis=1)
    wq = w_uq.reshape(Q_LORA, MLA_HEADS, QK_DIM) * (ATTN_SCALE * math.log2(math.e))
    wq = jnp.concatenate([wq, jnp.zeros((Q_LORA, MLA_HEADS, QK_PAD - QK_DIM), F32)], axis=2)
    w_r = jnp.concatenate([w_rg, w_re, jnp.zeros((D_MODEL, LANES - R_OFF - N_EXPERTS), F32)], axis=1)
    w_r_hi = w_r.astype(BF16)
    row = lambda v: v.reshape(1, -1)
    pad_row = lambda v: jnp.pad(v, (0, LANES - v.shape[0])).reshape(1, LANES)
    return dict(
        ln0_g=row(ln0_g), ln0_b=row(ln0_b), w_in=w_in_p.astype(BF16),
        q_norm=row(q_norm), kv_norm=row(kv_norm),
        w_uq=wq.reshape(Q_LORA, MLA_HEADS * QK_PAD).astype(BF16),
        w_ukv=jnp.concatenate([w_uk.reshape(KV_LORA, D_ATTN), w_uv.reshape(KV_LORA, D_ATTN)],
                              axis=1).astype(BF16),
        w_ukt=jnp.transpose(w_uk, (1, 2, 0)).astype(BF16),
        w_uvh=jnp.transpose(w_uv, (1, 0, 2)).astype(BF16),
        conv_w=conv_w, conv_b=row(conv_b), dt_bias=pad_row(dt_bias),
        a_neg=pad_row(-jnp.exp(a_log)), d_skip=row(jnp.repeat(d_skip, SSM_HEAD_DIM)),
        ssm_norm=row(ssm_norm),
        w_o_a=w_o[:D_ATTN].astype(BF16), w_o_s=w_o[D_ATTN:].astype(BF16),
        ln1_g=row(ln1_g), ln1_b=row(ln1_b),
        w_r=jnp.concatenate([w_r_hi, (w_r - w_r_hi.astype(F32)).astype(BF16)], axis=1),
        b_r=pad_row(jnp.concatenate([b_rg, b_re])),
        ln2_g=row(ln2_g), ln2_b=row(ln2_b),
    )


def _rope_tables(pos):
    half = QK_ROPE // 2
    inv_freq = ROPE_THETA ** (-jnp.arange(half, dtype=F32) / half)
    ang = pos.astype(F32)[:, None] * inv_freq[None, :]
    cos, sin = jnp.cos(ang), jnp.sin(ang)
    zeros = jnp.zeros((pos.shape[0], LANES - QK_ROPE), F32)
    return (jnp.concatenate([cos, cos, zeros], axis=1), jnp.concatenate([-sin, sin, zeros], axis=1))


def _dispatch_tables(route, counts, *, rb, n_blk):
    experts = jnp.arange(N_EXPERTS, dtype=jnp.int32)
    e = route[:, 0:TOP_K].astype(jnp.int32)
    rank = route[:, 4:4 + TOP_K].astype(jnp.int32)
    counts = counts.astype(jnp.int32)
    pcounts = (counts + rb - 1) // rb * rb
    pend = jnp.cumsum(pcounts)
    pstart = pend - pcounts
    pos = jnp.sum(jnp.where(e[:, :, None] == experts, pstart, 0), axis=2) + rank
    blk_start = jnp.arange(n_blk, dtype=jnp.int32) * rb
    blk_e = jnp.minimum(jnp.sum((pend[None, :] <= blk_start[:, None]).astype(jnp.int32), axis=1),
                        N_EXPERTS - 1)
    n_used = (pend[-1] // rb).reshape(1)
    blk_idx = jnp.arange(n_blk, dtype=jnp.int32)
    later = (blk_idx[None, :] > blk_idx[:, None]) & (blk_idx[None, :] < n_used) & (blk_e[None, :] > blk_e[:, None])
    next_e = jnp.min(jnp.where(later, blk_e[None, :], N_EXPERTS), axis=1)
    next_e = jnp.where(next_e < N_EXPERTS, next_e, -1).astype(jnp.int32)
    pad_base = pstart + counts
    pad_n = pcounts - counts
    return blk_e, n_used, next_e, pos, pad_base, pad_n


def _largest_tile(n, cap):
    t = cap
    while n % t:
        t //= 2
    return t


def kernel(x_prompt, x_sample, cache_kv_latent, cache_k_rope, state_conv, state_ssm, ln0_g, ln0_b, w_in, q_norm, w_uq, kv_norm, w_uk, w_uv, conv_w, conv_b, dt_bias, a_log, d_skip, ssm_norm, w_o, ln1_g, ln1_b, w_rg, b_rg, w_re, b_re, w_gate, w_up, w_down, ln2_g, ln2_b):
    B, L, _ = x_prompt.shape
    Bs, Ls, _ = x_sample.shape
    past = cache_kv_latent.shape[2]
    Tp, Ts = B * L, Bs * Ls
    pw = _prep_weights(ln0_g, ln0_b, w_in[0], q_norm[0], w_uq[0], kv_norm[0], w_uk[0], w_uv[0], conv_w[0],
                       conv_b[0], dt_bias[0], a_log[0], d_skip[0], ssm_norm[0], w_o[0], ln1_g[0], ln1_b[0],
                       w_rg[0], b_rg[0], w_re[0], b_re[0], ln2_g[0], ln2_b[0])

    tm_p = _largest_tile(L, 512)
    xp = x_prompt.reshape(Tp, D_MODEL)
    cos_p, sin_p = _rope_tables(jnp.arange(L, dtype=jnp.int32))
    q_p, kv_p, lat_p, kr_p, krp_p, z_p, xbc_p, dt_p = _inproj(
        xp, pw, cos_p, sin_p, tm=tm_p, pos_blocks=L // tm_p, with_kv=True)
    attn_p = _attn_prompt(q_p.reshape(B, L, -1), kv_p.reshape(B, L, -1), krp_p.reshape(B, L, -1),
                          tq=_largest_tile(L, 512), tk=_largest_tile(L, 1024))
    q_ssd = min(CHUNK, L)
    y_p, conv_p, ssm_p = _ssd(
        xbc_p.reshape(B, L, -1), dt_p.reshape(B, L, -1), z_p.reshape(B, L, -1),
        jnp.zeros((B, CONV_K - 1, CONV_CH), F32), jnp.zeros((B, D_SSM, D_STATE), F32), pw,
        q=q_ssd, tt=_largest_tile(L, 512))

    xs = x_sample.reshape(Ts, D_MODEL)
    cos_s, sin_s = _rope_tables(past + jnp.arange(Ls, dtype=jnp.int32))
    tm_s = _largest_tile(Ts, 256)
    assert tm_s % Ls == 0
    q_s, _, lat_s, kr_s, krp_s, z_s, xbc_s, dt_s = _inproj(
        xs, pw, jnp.tile(cos_s, (tm_s // Ls, 1)), jnp.tile(sin_s, (tm_s // Ls, 1)),
        tm=tm_s, pos_blocks=1, with_kv=False)
    attn_s = _attn_sample(q_s.reshape(Bs, Ls, -1), pw, cache_kv_latent[0], cache_k_rope[0],
                          lat_s.reshape(Bs, Ls, -1), krp_s.reshape(Bs, Ls, -1),
                          tk=_largest_tile(past, 1024))
    y_s, conv_s, ssm_s = _ssd(
        xbc_s.reshape(Bs, Ls, -1), dt_s.reshape(Bs, Ls, -1), z_s.reshape(Bs, Ls, -1),
        state_conv[0], state_ssm[0].reshape(Bs, D_SSM, D_STATE), pw, q=min(CHUNK, Ls), tt=Ls)

    T = Tp + Ts
    tm_c = _largest_tile(math.gcd(Tp, Ts), 256)
    h_all, hpk_all, route, counts = _outproj(xp, xs, attn_p.reshape(Tp, -1), attn_s.reshape(Ts, -1),
                                             y_p.reshape(Tp, -1), y_s.reshape(Ts, -1), pw, tm=tm_c)
    rb = 256
    n_blk = (T * TOP_K + N_EXPERTS * (rb - 1) + rb - 1) // rb
    blk_e, n_used, next_e, pos, pad_base, pad_n = _dispatch_tables(
        route, counts[0, R_OFF:R_OFF + N_EXPERTS], rb=rb, n_blk=n_blk)
    pos_tiles = pos.reshape(T // tm_c, tm_c, TOP_K).transpose(0, 2, 1).reshape(-1)
    xs_rows = _dispatch(hpk_all, pos_tiles, pad_base, pad_n, n_used, tm=tm_c, rb=rb, n_blk=n_blk)
    y_rows = _experts(xs_rows, blk_e, n_used, next_e, w_gate[0], w_up[0], w_down[0], rb=rb)
    out_p, out_s = _combine(y_rows, pos_tiles, h_all, route, pw, tm=tm_c, t_prompt=Tp)

    return (out_p.reshape(B, L, D_MODEL), out_s.reshape(Bs, Ls, D_MODEL),
            lat_p.reshape(1, B, L, KV_LORA), lat_s.reshape(1, Bs, Ls, KV_LORA),
            kr_p.reshape(1, B, L, QK_ROPE), kr_s.reshape(1, Bs, Ls, QK_ROPE),
            conv_p[None], conv_s[None],
            ssm_p.reshape(1, B, SSM_HEADS, SSM_HEAD_DIM, D_STATE),
            ssm_s.reshape(1, Bs, SSM_HEADS, SSM_HEAD_DIM, D_STATE))
```

```python
import functools
import math

import jax
import jax.numpy as jnp
import numpy as np
from jax import lax
from jax.experimental import pallas as pl
from jax.experimental.pallas import tpu as pltpu

F32 = jnp.float32
BF16 = jnp.bfloat16

D_MODEL = 2048
CHUNK = 64
MLA_HEADS = 8
QK_NOPE = 128
QK_ROPE = 64
QK_DIM = QK_NOPE + QK_ROPE
V_DIM = 128
Q_LORA = 512
KV_LORA = 512
ROPE_THETA = 10000.0
ATTN_SCALE = QK_DIM ** -0.5
D_ATTN = MLA_HEADS * V_DIM
D_SSM = 1024
SSM_HEAD_DIM = 64
SSM_HEADS = D_SSM // SSM_HEAD_DIM
SSM_GROUPS = 2
HEADS_PER_GROUP = SSM_HEADS // SSM_GROUPS
D_STATE = 128
CONV_K = 4
CONV_CH = D_SSM + 2 * SSM_GROUPS * D_STATE
N_EXPERT_GROUPS = 8
EXPERTS_PER_GROUP = 8
N_EXPERTS = N_EXPERT_GROUPS * EXPERTS_PER_GROUP
TOP_K = 2
D_EXPERT = 512
DEPTH = 1
ALPHA = (2 * DEPTH) ** 0.25
RMS_EPS = 1e-6
LN_EPS = 1e-5

LANES = 128
MXU_DIM = 256
VMEM_LIMIT = 56 * 1024 * 1024

QK_PAD = 2 * LANES
GATHER_AHEAD = 2
GATHER_SLOTS = GATHER_AHEAD + 1
C_Q, C_KV, C_Z, C_XBC, C_KR, C_DT = 0, 512, 1024, 2048, 3584, 3712
D_IN_PAD = 3840


def _const_spec(shape):
    nd = len(shape)
    return pl.BlockSpec(shape, lambda *_: (0,) * nd, pipeline_mode=pl.Buffered(1))


def _split3(a):
    hi = a.astype(BF16)
    r1 = a - hi.astype(F32)
    mid = r1.astype(BF16)
    lo = (r1 - mid.astype(F32)).astype(BF16)
    return hi, mid, lo


def _dot(a, b):
    return jnp.dot(a, b, preferred_element_type=F32)


def _dot_nt(a, b):
    return lax.dot_general(a, b, (((1,), (1,)), ((), ())), preferred_element_type=F32)


def _dot_tn(a, b):
    return lax.dot_general(a, b, (((0,), (0,)), ((), ())), preferred_element_type=F32)


def _exact_dot(a_f32, sel_bf16):
    hi, mid, lo = _split3(a_f32)
    return _dot(hi, sel_bf16) + _dot(mid, sel_bf16) + _dot(lo, sel_bf16)


def _exact_dot_l(sel_bf16, a_f32):
    hi, mid, lo = _split3(a_f32)
    return _dot(sel_bf16, hi) + _dot(sel_bf16, mid) + _dot(sel_bf16, lo)


def _layernorm(x, g, b):
    mu = jnp.mean(x, axis=-1, keepdims=True)
    xc = x - mu
    var = jnp.mean(xc * xc, axis=-1, keepdims=True)
    return xc * lax.rsqrt(var + LN_EPS) * g + b


def _rmsnorm(x, g):
    return x * lax.rsqrt(jnp.mean(x * x, axis=-1, keepdims=True) + RMS_EPS) * g


def _silu(x):
    return x * (1.0 / (1.0 + jnp.exp(-x)))


def _pack_bf16_pairs(x):
    n = x.shape[1] // 2
    bits = lax.bitcast_convert_type(x, jnp.uint32)
    return (bits[:, n:] & jnp.uint32(0xFFFF0000)) | (bits[:, :n] >> 16)


def _unpack_bf16_pairs(u):
    lo = lax.bitcast_convert_type(u << 16, F32)
    hi = lax.bitcast_convert_type(u & jnp.uint32(0xFFFF0000), F32)
    return jnp.concatenate([lo, hi], axis=1)


def _rope128(x, cos_t, sin_t):
    lane = lax.broadcasted_iota(jnp.int32, x.shape, 1)
    partner = jnp.where(lane < QK_ROPE // 2, pltpu.roll(x, LANES - QK_ROPE // 2, 1),
                        pltpu.roll(x, QK_ROPE // 2, 1))
    return x * cos_t + partner * sin_t


def _inproj_kernel(x_ref, g0_ref, b0_ref, w_in_ref, qn_ref, kvn_ref, w_uq_ref, w_ukv_ref,
                   cos_ref, sin_ref, dtb_ref,
                   q_ref, kv_ref, lat_ref, kr_ref, krp_ref, z_ref, xbc_ref, dt_ref, *, with_kv):
    xn = _layernorm(x_ref[...], g0_ref[...], b0_ref[...]).astype(BF16)
    cos_t = cos_ref[...]
    sin_t = sin_ref[...]

    c_q = _dot(xn, w_in_ref[:, C_Q:C_Q + Q_LORA])
    qb = _rmsnorm(c_q, qn_ref[...]).astype(BF16)
    for h in range(MLA_HEADS):
        qh = _dot(qb, w_uq_ref[:, h * QK_PAD:(h + 1) * QK_PAD])
        q_ref[:, h * QK_PAD:h * QK_PAD + LANES] = qh[:, :LANES].astype(BF16)
        q_ref[:, h * QK_PAD + LANES:(h + 1) * QK_PAD] = _rope128(qh[:, LANES:], cos_t, sin_t).astype(BF16)

    c_kv = _dot(xn, w_in_ref[:, C_KV:C_KV + KV_LORA])
    lat = _rmsnorm(c_kv, kvn_ref[...])
    lat_ref[...] = lat
    if with_kv:
        kv_ref[...] = _dot(lat.astype(BF16), w_ukv_ref[...]).astype(BF16)
    else:
        kv_ref[...] = jnp.zeros(kv_ref.shape, BF16)

    k_r = _rope128(_dot(xn, w_in_ref[:, C_KR:C_KR + LANES]), cos_t, sin_t)
    kr_ref[...] = k_r[:, :QK_ROPE]
    krp_ref[...] = k_r.astype(BF16)

    z_ref[...] = _dot(xn, w_in_ref[:, C_Z:C_Z + D_SSM]).astype(BF16)
    xbc_ref[...] = _dot(xn, w_in_ref[:, C_XBC:C_XBC + CONV_CH])

    dt_raw = _dot(xn, w_in_ref[:, C_DT:C_DT + LANES]) + dtb_ref[...]
    dt = jnp.maximum(dt_raw, 0.0) + jnp.log1p(jnp.exp(-jnp.abs(dt_raw)))
    lane = lax.broadcasted_iota(jnp.int32, dt.shape, 1)
    dt_ref[...] = jnp.where(lane < SSM_HEADS, dt, 0.0)


def _inproj(x, pw, cos_t, sin_t, *, tm, pos_blocks, with_kv):
    T = x.shape[0]
    n = T // tm
    row = lambda i: (i, 0)
    pos = lambda i: (i % pos_blocks, 0)
    kv_cols = 2 * D_ATTN if with_kv else LANES
    out_shape = [
        jax.ShapeDtypeStruct((T, MLA_HEADS * QK_PAD), BF16),
        jax.ShapeDtypeStruct((T, kv_cols), BF16),
        jax.ShapeDtypeStruct((T, KV_LORA), F32),
        jax.ShapeDtypeStruct((T, QK_ROPE), F32),
        jax.ShapeDtypeStruct((T, LANES), BF16),
        jax.ShapeDtypeStruct((T, D_SSM), BF16),
        jax.ShapeDtypeStruct((T, CONV_CH), F32),
        jax.ShapeDtypeStruct((T, LANES), F32),
    ]
    out_specs = [pl.BlockSpec((tm, s.shape[1]), row) for s in out_shape]
    in_specs = [
        pl.BlockSpec((tm, D_MODEL), row),
        _const_spec((1, D_MODEL)), _const_spec((1, D_MODEL)),
        _const_spec((D_MODEL, D_IN_PAD)),
        _const_spec((1, Q_LORA)), _const_spec((1, KV_LORA)),
        _const_spec((Q_LORA, MLA_HEADS * QK_PAD)),
        _const_spec((KV_LORA, 2 * D_ATTN)),
        pl.BlockSpec((tm, LANES), pos), pl.BlockSpec((tm, LANES), pos),
        _const_spec((1, LANES)),
    ]
    return pl.pallas_call(
        functools.partial(_inproj_kernel, with_kv=with_kv),
        grid=(n,), in_specs=in_specs, out_specs=out_specs, out_shape=out_shape,
        compiler_params=pltpu.CompilerParams(dimension_semantics=("arbitrary",),
                                             vmem_limit_bytes=VMEM_LIMIT),
        name="inproj",
    )(x, pw["ln0_g"], pw["ln0_b"], pw["w_in"], pw["q_norm"], pw["kv_norm"], pw["w_uq"],
      pw["w_ukv"], cos_t, sin_t, pw["dt_bias"])


def _attn_prompt_kernel(qi_ref, ki_ref, q_ref, kv_ref, krp_ref, o_ref, *scratch, tq, tk):
    m_refs = scratch[0:MLA_HEADS]
    acc_refs = scratch[MLA_HEADS:2 * MLA_HEADS]
    p_id = pl.program_id(1)
    qi = qi_ref[p_id]
    ki = ki_ref[p_id]
    k_last = ((qi + 1) * tq - 1) // tk

    @pl.when(ki == 0)
    def _():
        for h in range(MLA_HEADS):
            m_refs[h][...] = jnp.full(m_refs[h].shape, -jnp.inf, F32)
            acc_refs[h][...] = jnp.zeros(acc_refs[h].shape, F32)

    def step(masked):
        krp = krp_ref[...]
        ones_col = (lax.broadcasted_iota(jnp.int32, (tk, LANES), 1) == 0).astype(BF16)
        if masked:
            r = (qi * tq + lax.broadcasted_iota(jnp.int32, (tq, tk), 0)) // CHUNK
            c = (ki * tk + lax.broadcasted_iota(jnp.int32, (tq, tk), 1)) // CHUNK
            allowed = c <= r
        for h in range(MLA_HEADS):
            qh = q_ref[:, h * QK_PAD:(h + 1) * QK_PAD]
            kh = jnp.concatenate([kv_ref[:, h * QK_NOPE:(h + 1) * QK_NOPE], krp], axis=1)
            s = _dot_nt(qh, kh).astype(BF16)
            if masked:
                s = jnp.where(allowed, s, -jnp.inf)
            m_old = m_refs[h][...]
            m_new = jnp.maximum(m_old, jnp.max(s, axis=1, keepdims=True).astype(F32))
            alpha = jnp.exp2(m_old - m_new)
            p = jnp.exp2(s - m_new.astype(BF16))
            v_aug = jnp.concatenate([kv_ref[:, D_ATTN + h * V_DIM:D_ATTN + (h + 1) * V_DIM], ones_col], axis=1)
            acc_refs[h][...] = alpha * acc_refs[h][...] + _dot(p, v_aug)
            m_refs[h][...] = m_new

    @pl.when(ki < k_last)
    def _():
        step(False)

    @pl.when(ki == k_last)
    def _():
        step(True)
        for h in range(MLA_HEADS):
            acc = acc_refs[h][...]
            o_ref[:, h * V_DIM:(h + 1) * V_DIM] = (acc[:, :V_DIM] / acc[:, V_DIM:V_DIM + 1]).astype(BF16)


def _attn_prompt(q, kv, krp, *, tq, tk):
    B, L, _ = q.shape
    nq = L // tq
    pairs = [(i, j) for i in range(nq) for j in range(((i + 1) * tq - 1) // tk + 1)]
    qi_tab = jnp.asarray(np.array([p[0] for p in pairs], np.int32))
    ki_tab = jnp.asarray(np.array([p[1] for p in pairs], np.int32))
    grid_spec = pltpu.PrefetchScalarGridSpec(
        num_scalar_prefetch=2,
        grid=(B, len(pairs)),
        in_specs=[
            pl.BlockSpec((None, tq, MLA_HEADS * QK_PAD), lambda b, p, qi, ki: (b, qi[p], 0)),
            pl.BlockSpec((None, tk, 2 * D_ATTN), lambda b, p, qi, ki: (b, ki[p], 0)),
            pl.BlockSpec((None, tk, LANES), lambda b, p, qi, ki: (b, ki[p], 0)),
        ],
        out_specs=pl.BlockSpec((None, tq, D_ATTN), lambda b, p, qi, ki: (b, qi[p], 0)),
        scratch_shapes=([pltpu.VMEM((tq, 1), F32)] * MLA_HEADS
                        + [pltpu.VMEM((tq, 2 * V_DIM), F32)] * MLA_HEADS),
    )
    return pl.pallas_call(
        functools.partial(_attn_prompt_kernel, tq=tq, tk=tk),
        grid_spec=grid_spec,
        out_shape=jax.ShapeDtypeStruct((B, L, D_ATTN), BF16),
        compiler_params=pltpu.CompilerParams(dimension_semantics=("arbitrary", "arbitrary"),
                                             vmem_limit_bytes=VMEM_LIMIT),
        name="attn_prompt",
    )(qi_tab, ki_tab, q, kv, krp)


def _attn_sample_kernel(q_ref, wukt_ref, wuv_ref, latp_ref, krpast_ref, latn_ref, krn_ref, o_ref,
                        qlat_ref, qrp_ref, m_ref, l_ref, acc_ref, *, ls, past, tk, n_new_pad):
    k = pl.program_id(1)
    nk = pl.num_programs(1)
    rows = MLA_HEADS * ls

    @pl.when(k == 0)
    def _():
        for h in range(MLA_HEADS):
            qn = q_ref[:, h * QK_PAD:h * QK_PAD + LANES]
            qlat_ref[h * ls:(h + 1) * ls, :] = _dot(qn, wukt_ref[h]).astype(BF16)
            qrp_ref[h * ls:(h + 1) * ls, :] = q_ref[:, h * QK_PAD + LANES:(h + 1) * QK_PAD]
        m_ref[...] = jnp.full(m_ref.shape, -jnp.inf, F32)
        l_ref[...] = jnp.zeros(l_ref.shape, F32)
        acc_ref[...] = jnp.zeros(acc_ref.shape, F32)

    q_chunk = (past + lax.broadcasted_iota(jnp.int32, (rows, 1), 0) % ls) // CHUNK

    def update(s, lat_b):
        m_old = m_ref[...]
        m_new = jnp.maximum(m_old, jnp.max(s, axis=1, keepdims=True))
        alpha = jnp.exp2(m_old - m_new)
        p = jnp.exp2(s - m_new)
        l_ref[...] = alpha * l_ref[...] + jnp.sum(p, axis=1, keepdims=True)
        acc_ref[...] = alpha * acc_ref[...] + _dot(p.astype(BF16), lat_b)
        m_ref[...] = m_new

    lat_b = latp_ref[...].astype(BF16)
    kr_b = krpast_ref[...].astype(BF16)
    s = _dot_nt(qlat_ref[...], lat_b) + _dot_nt(qrp_ref[:, :QK_ROPE], kr_b)
    k_pos = k * tk + lax.broadcasted_iota(jnp.int32, (rows, tk), 1)
    s = jnp.where(k_pos // CHUNK <= q_chunk, s, -jnp.inf)
    update(s, lat_b)

    @pl.when(k == nk - 1)
    def _():
        latn_b = latn_ref[...].astype(BF16)
        s2 = _dot_nt(qlat_ref[...], latn_b) + _dot_nt(qrp_ref[...], krn_ref[...])
        j = lax.broadcasted_iota(jnp.int32, (rows, n_new_pad), 1)
        ok = ((past + j) // CHUNK <= q_chunk) & (j < ls)
        update(jnp.where(ok, s2, -jnp.inf), latn_b)
        o_lat = (acc_ref[...] / l_ref[...]).astype(BF16)
        for h in range(MLA_HEADS):
            o_ref[:, h * V_DIM:(h + 1) * V_DIM] = _dot(o_lat[h * ls:(h + 1) * ls, :], wuv_ref[h]).astype(BF16)


def _attn_sample(q, pw, lat_past, kr_past, lat_new, krp_new, *, tk):
    Bs, Ls, _ = q.shape
    past = lat_past.shape[1]
    n_new_pad = LANES
    lat_new = jnp.pad(lat_new, ((0, 0), (0, n_new_pad - Ls), (0, 0)))
    krp_new = jnp.pad(krp_new, ((0, 0), (0, n_new_pad - Ls), (0, 0)))
    rows = MLA_HEADS * Ls
    return pl.pallas_call(
        functools.partial(_attn_sample_kernel, ls=Ls, past=past, tk=tk, n_new_pad=n_new_pad),
        grid=(Bs, past // tk),
        in_specs=[
            pl.BlockSpec((None, Ls, MLA_HEADS * QK_PAD), lambda b, k: (b, 0, 0)),
            _const_spec((MLA_HEADS, QK_NOPE, KV_LORA)),
            _const_spec((MLA_HEADS, KV_LORA, V_DIM)),
            pl.BlockSpec((None, tk, KV_LORA), lambda b, k: (b, k, 0)),
            pl.BlockSpec((None, tk, QK_ROPE), lambda b, k: (b, k, 0)),
            pl.BlockSpec((None, n_new_pad, KV_LORA), lambda b, k: (b, 0, 0)),
            pl.BlockSpec((None, n_new_pad, LANES), lambda b, k: (b, 0, 0)),
        ],
        out_specs=pl.BlockSpec((None, Ls, D_ATTN), lambda b, k: (b, 0, 0)),
        out_shape=jax.ShapeDtypeStruct((Bs, Ls, D_ATTN), BF16),
        scratch_shapes=[pltpu.VMEM((rows, KV_LORA), BF16), pltpu.VMEM((rows, LANES), BF16),
                        pltpu.VMEM((rows, 1), F32), pltpu.VMEM((rows, 1), F32),
                        pltpu.VMEM((rows, KV_LORA), F32)],
        compiler_params=pltpu.CompilerParams(dimension_semantics=("arbitrary", "arbitrary"),
                                             vmem_limit_bytes=VMEM_LIMIT),
        name="attn_sample",
    )(q, pw["w_ukt"], pw["w_uvh"], lat_past, kr_past, lat_new, krp_new)


def _ssd_consts(q):
    hb = MXU_DIM // q
    hq = SSM_HEADS * q
    lane = np.arange(hq)
    ltri = (np.arange(q)[None, :] <= np.arange(q)[:, None]).astype(np.float32)
    sel_c = (np.arange(LANES)[:, None] == (lane // q)[None, :]).astype(np.float32)
    sel_p = (np.arange(LANES)[:, None] == (np.arange(D_SSM) // SSM_HEAD_DIM)[None, :]).astype(np.float32)
    diag = (np.arange(q)[:, None] == (lane % q)[None, :]).astype(np.float32)
    causal = (np.arange(q)[:, None] >= (lane % q)[None, :]).astype(np.float32)
    bd_rows = np.arange(hb * q) // q
    bd_cols = np.arange(hb * SSM_HEAD_DIM) // SSM_HEAD_DIM
    bdmask = (bd_rows[:, None] == bd_cols[None, :]).astype(np.float32)
    return dict(ltri=jnp.asarray(ltri, BF16), sel_c=jnp.asarray(sel_c, BF16),
                sel_p=jnp.asarray(sel_p, BF16), diag=jnp.asarray(diag, F32),
                causal=jnp.asarray(causal, F32), bdmask=jnp.asarray(bdmask, BF16))


def _ssd_kernel(xbc_ref, dt_ref, z_ref, cprev_ref, h0_ref, cw_ref, cb_ref, a_ref, dskip_ref, norm_ref,
                ltri_ref, selc_ref, selp_ref, diag_ref, causal_ref, bdmask_ref,
                y_ref, cout_ref, hout_ref,
                ext_ref, u_ref, ht_ref, *, q, tt):
    t = pl.program_id(1)
    nt = pl.num_programs(1)
    hb = MXU_DIM // q
    nblk = SSM_HEADS // hb
    gw = HEADS_PER_GROUP * SSM_HEAD_DIM
    pad = 8

    @pl.when(t == 0)
    def _():
        ext_ref[0:pad, :] = jnp.zeros((pad, CONV_CH), F32)
        ext_ref[pad - (CONV_K - 1):pad, :] = cprev_ref[...]
        ht_ref[...] = h0_ref[...].T

    xbc = xbc_ref[...]
    ext_ref[pad:pad + tt, :] = xbc
    cout_ref[...] = xbc[tt - (CONV_K - 1):, :]
    conv = cb_ref[...] + cw_ref[0:1, :] * ext_ref[pad - 3:pad - 3 + tt, :]
    for k in range(1, CONV_K):
        conv = conv + cw_ref[k:k + 1, :] * ext_ref[pad - 3 + k:pad - 3 + k + tt, :]
    u_ref[...] = _silu(conv)
    ext_ref[pad - (CONV_K - 1):pad, :] = xbc[tt - (CONV_K - 1):, :]

    a_row = a_ref[...]
    d_row = dskip_ref[...]
    norm_row = norm_ref[...]

    def chunk(c, carry):
        r0 = pl.multiple_of(c * q, q)
        u = u_ref[pl.ds(r0, q), :]
        xs = u[:, :D_SSM]
        dt = dt_ref[pl.ds(r0, q), :]
        a_cs = _exact_dot_l(ltri_ref[...], dt * a_row)
        ap = _exact_dot(a_cs, selp_ref[...])
        dtp = _exact_dot(dt, selp_ref[...])
        a_i = ap if q == SSM_HEAD_DIM else _exact_dot(a_cs, selc_ref[...])
        a_j = jnp.sum(a_i * diag_ref[...], axis=0, keepdims=True)
        decay = jnp.exp(jnp.where(causal_ref[...] > 0.5, a_i - a_j, -jnp.inf))
        ap_last = ap[q - 1:q, :]
        xdt = (xs * dtp).astype(BF16)
        xw = (xs * (jnp.exp(ap_last - ap) * dtp)).astype(BF16)
        e_ap = jnp.exp(ap)

        cb_parts, y_off_parts = [], []
        for g in range(SSM_GROUPS):
            bm = u[:, D_SSM + g * D_STATE:D_SSM + (g + 1) * D_STATE].astype(BF16)
            cm = u[:, D_SSM + (SSM_GROUPS + g) * D_STATE:D_SSM + (SSM_GROUPS + g + 1) * D_STATE].astype(BF16)
            cb_parts.append(_dot_nt(cm, jnp.concatenate([bm] * HEADS_PER_GROUP, axis=0)))
            gl = slice(g * gw, (g + 1) * gw)
            ht_g = ht_ref[:, gl]
            y_off_parts.append(_dot(cm, ht_g.astype(BF16)) * e_ap[:, gl])
            st = _dot_tn(bm, xw[:, gl])
            ht_ref[:, gl] = ht_g * jnp.exp(ap_last[:, gl]) + st
        w_all = (jnp.concatenate(cb_parts, axis=1) * decay).astype(BF16)
        y_diag_parts = []
        for b in range(nblk):
            x_b = xdt[:, b * hb * SSM_HEAD_DIM:(b + 1) * hb * SSM_HEAD_DIM]
            bd = jnp.concatenate([x_b] * hb, axis=0) * bdmask_ref[...]
            y_diag_parts.append(_dot(w_all[:, b * MXU_DIM:(b + 1) * MXU_DIM], bd))
        y = jnp.concatenate(y_diag_parts, axis=1) + jnp.concatenate(y_off_parts, axis=1)

        y = y + d_row * xs
        y = y * _silu(z_ref[pl.ds(r0, q), :].astype(F32))
        outs = []
        for g in range(SSM_GROUPS):
            yg = y[:, g * gw:(g + 1) * gw]
            outs.append(yg * lax.rsqrt(jnp.mean(yg * yg, axis=1, keepdims=True) + RMS_EPS))
        y_ref[pl.ds(r0, q), :] = (jnp.concatenate(outs, axis=1) * norm_row).astype(BF16)
        return carry

    lax.fori_loop(0, tt // q, chunk, 0, unroll=True)

    @pl.when(t == nt - 1)
    def _():
        hout_ref[...] = ht_ref[...].T


def _ssd(xbc, dt, z, conv_prev, h0, pw, *, q, tt):
    B, L, _ = xbc.shape
    cs = _ssd_consts(q)
    hq = SSM_HEADS * q
    hb = MXU_DIM // q
    tile = lambda w: pl.BlockSpec((None, tt, w), lambda b, t: (b, t, 0))
    per_b = lambda s: pl.BlockSpec((None,) + s, lambda b, t: (b, 0, 0))
    return pl.pallas_call(
        functools.partial(_ssd_kernel, q=q, tt=tt),
        grid=(B, L // tt),
        in_specs=[tile(CONV_CH), tile(LANES), tile(D_SSM), per_b((CONV_K - 1, CONV_CH)),
                  per_b((D_SSM, D_STATE)),
                  _const_spec((CONV_K, CONV_CH)), _const_spec((1, CONV_CH)), _const_spec((1, LANES)),
                  _const_spec((1, D_SSM)), _const_spec((1, D_SSM)),
                  _const_spec((q, q)), _const_spec((LANES, hq)), _const_spec((LANES, D_SSM)),
                  _const_spec((q, hq)), _const_spec((q, hq)),
                  _const_spec((MXU_DIM, hb * SSM_HEAD_DIM))],
        out_specs=[tile(D_SSM), per_b((CONV_K - 1, CONV_CH)), per_b((D_SSM, D_STATE))],
        out_shape=[jax.ShapeDtypeStruct((B, L, D_SSM), BF16),
                   jax.ShapeDtypeStruct((B, CONV_K - 1, CONV_CH), F32),
                   jax.ShapeDtypeStruct((B, D_SSM, D_STATE), F32)],
        scratch_shapes=[pltpu.VMEM((8 + tt, CONV_CH), F32), pltpu.VMEM((tt, CONV_CH), F32),
                        pltpu.VMEM((D_STATE, D_SSM), F32)],
        compiler_params=pltpu.CompilerParams(dimension_semantics=("arbitrary", "arbitrary"),
                                             vmem_limit_bytes=VMEM_LIMIT),
        name="ssd",
    )(xbc, dt, z, conv_prev, h0, pw["conv_w"], pw["conv_b"], pw["a_neg"], pw["d_skip"], pw["ssm_norm"],
      cs["ltri"], cs["sel_c"], cs["sel_p"], cs["diag"], cs["causal"], cs["bdmask"])


R_OFF = N_EXPERT_GROUPS


def _outproj_kernel(xp_ref, xs_ref, attnp_ref, attns_ref, yp_ref, ys_ref, g0_ref, b0_ref, wo_a_ref, wo_s_ref,
                    g1_ref, b1_ref, wr_ref, br_ref, ltri_ref, h_ref, hpk_ref, route_ref, cnt_ref,
                    *, n_prompt_tiles, tm, n_groups):
    i = pl.program_id(0)
    is_p = i < n_prompt_tiles

    @pl.when(i == 0)
    def _():
        cnt_ref[...] = jnp.zeros(cnt_ref.shape, F32)

    cnt = cnt_ref[0:1, :]
    for g in range(n_groups):
        rows = slice(g * (tm // n_groups), (g + 1) * (tm // n_groups))
        refs = (xp_ref, xs_ref, attnp_ref, attns_ref, yp_ref, ys_ref)
        cnt = _outproj_rows(rows, is_p, cnt, refs, g0_ref, b0_ref, wo_a_ref, wo_s_ref, g1_ref, b1_ref,
                            wr_ref, br_ref, ltri_ref, h_ref, hpk_ref, route_ref)
    cnt_ref[...] = jnp.broadcast_to(cnt, cnt_ref.shape)


def _outproj_rows(rows, is_p, cnt, refs, g0_ref, b0_ref, wo_a_ref, wo_s_ref, g1_ref, b1_ref, wr_ref, br_ref,
                  ltri_ref, h_ref, hpk_ref, route_ref):
    xp_ref, xs_ref, attnp_ref, attns_ref, yp_ref, ys_ref = refs
    x = jnp.where(is_p, xp_ref[rows, :], xs_ref[rows, :])
    attn = jnp.where(is_p, attnp_ref[rows, :], attns_ref[rows, :])
    yssm = jnp.where(is_p, yp_ref[rows, :], ys_ref[rows, :])

    xn = _layernorm(x, g0_ref[...], b0_ref[...])
    mixed = _dot(attn, wo_a_ref[...]) + _dot(yssm, wo_s_ref[...])
    h = _layernorm(ALPHA * xn + mixed, g1_ref[...], b1_ref[...])
    h_ref[rows, :] = h

    h_hi = h.astype(BF16)
    h_hi32 = h_hi.astype(F32)
    hpk_ref[rows, :] = _pack_bf16_pairs(h_hi32)
    h_lo = (h - h_hi32).astype(BF16)
    tm = h.shape[0]
    prod = _dot(jnp.concatenate([h_hi, h_lo], axis=0), wr_ref[...])
    lg = (prod[:tm, :LANES] + prod[:tm, LANES:]) + (prod[tm:, :LANES] + prod[tm:, LANES:]) + br_ref[...]
    lane = lax.broadcasted_iota(jnp.int32, lg.shape, 1)
    big = jnp.int32(1 << 20)
    gl = jnp.where(lane < N_EXPERT_GROUPS, lg, -jnp.inf)
    gmax = jnp.max(gl, axis=1, keepdims=True)
    grp = jnp.min(jnp.where(gl == gmax, lane, big), axis=1, keepdims=True)
    g_w = 1.0 / jnp.sum(jnp.exp(gl - gmax), axis=1, keepdims=True)
    in_grp = (lane >= R_OFF) & (lane < R_OFF + N_EXPERTS) & ((lane - R_OFF) // EXPERTS_PER_GROUP == grp)
    el = jnp.where(in_grp, lg, -jnp.inf)
    emax = jnp.max(el, axis=1, keepdims=True)
    ee = jnp.exp(el - emax)
    prob = jnp.where(in_grp, ee / jnp.sum(ee, axis=1, keepdims=True), -1.0)
    p1 = jnp.max(prob, axis=1, keepdims=True)
    i1 = jnp.min(jnp.where(prob == p1, lane, big), axis=1, keepdims=True)
    prob2 = jnp.where(lane == i1, -1.0, prob)
    p2 = jnp.max(prob2, axis=1, keepdims=True)
    i2 = jnp.min(jnp.where(prob2 == p2, lane, big), axis=1, keepdims=True)
    denom = p1 + p2
    oh1 = (lane == i1).astype(F32)
    oh2 = (lane == i2).astype(F32)
    oh = oh1 + oh2
    before = _dot(ltri_ref[...], oh.astype(BF16)) + cnt
    rank1 = jnp.sum(before * oh1, axis=1, keepdims=True)
    rank2 = jnp.sum(before * oh2, axis=1, keepdims=True)
    route = jnp.where(lane == 0, (i1 - R_OFF).astype(F32),
                      jnp.where(lane == 1, (i2 - R_OFF).astype(F32),
                                jnp.where(lane == 2, g_w * p1 / denom,
                                          jnp.where(lane == 3, g_w * p2 / denom,
                                                    jnp.where(lane == 4, rank1,
                                                              jnp.where(lane == 5, rank2, 0.0))))))
    route_ref[rows, :] = route
    return cnt + jnp.sum(oh, axis=0, keepdims=True)


def _outproj(xp, xs, attn_p, attn_s, y_p, y_s, pw, *, tm):
    Tp, Ts = xp.shape[0], xs.shape[0]
    npt, nst = Tp // tm, Ts // tm
    T = Tp + Ts
    row = lambda i: (i, 0)
    prow = lambda i: (jnp.minimum(i, npt - 1), 0)
    srow = lambda i: (jnp.maximum(i - npt, 0), 0)
    n_groups = 1
    gm = tm // n_groups
    ltri = jnp.asarray(np.tril(np.ones((gm, gm), np.float32), -1), BF16)
    return pl.pallas_call(
        functools.partial(_outproj_kernel, n_prompt_tiles=npt, tm=tm, n_groups=n_groups),
        grid=(npt + nst,),
        in_specs=[pl.BlockSpec((tm, D_MODEL), prow), pl.BlockSpec((tm, D_MODEL), srow),
                  pl.BlockSpec((tm, D_ATTN), prow), pl.BlockSpec((tm, D_ATTN), srow),
                  pl.BlockSpec((tm, D_SSM), prow), pl.BlockSpec((tm, D_SSM), srow),
                  _const_spec((1, D_MODEL)), _const_spec((1, D_MODEL)),
                  _const_spec((D_ATTN, D_MODEL)), _const_spec((D_SSM, D_MODEL)),
                  _const_spec((1, D_MODEL)), _const_spec((1, D_MODEL)),
                  _const_spec((D_MODEL, 2 * LANES)), _const_spec((1, LANES)),
                  _const_spec((gm, gm))],
        out_specs=[pl.BlockSpec((tm, D_MODEL), row), pl.BlockSpec((tm, D_MODEL // 2), row),
                   pl.BlockSpec((tm, LANES), row), pl.BlockSpec((8, LANES), lambda i: (0, 0))],
        out_shape=[jax.ShapeDtypeStruct((T, D_MODEL), F32), jax.ShapeDtypeStruct((T, D_MODEL // 2), jnp.uint32),
                   jax.ShapeDtypeStruct((T, LANES), F32), jax.ShapeDtypeStruct((8, LANES), F32)],
        compiler_params=pltpu.CompilerParams(dimension_semantics=("arbitrary",),
                                             vmem_limit_bytes=VMEM_LIMIT),
        name="outproj",
    )(xp, xs, attn_p, attn_s, y_p, y_s, pw["ln0_g"], pw["ln0_b"], pw["w_o_a"], pw["w_o_s"],
      pw["ln1_g"], pw["ln1_b"], pw["w_r"], pw["b_r"], ltri)


def _dispatch_kernel(pad_base_ref, pad_n_ref, n_used_ref, pos_ref, hpk_ref, xs_hbm, sbuf, zbuf, sem,
                     *, tm, n_tiles, rb, n_blk):
    i = pl.program_id(0)
    rows = TOP_K * tm
    slot = i % 2
    pad_sem = 2

    def wait_tile(s):
        for _ in range(TOP_K):
            pltpu.make_async_copy(sbuf.at[s], xs_hbm.at[pl.ds(0, tm)], sem.at[s]).wait()

    @pl.when(i == 0)
    def _():
        zbuf[...] = jnp.zeros(zbuf.shape, jnp.uint32)

    @pl.when(i < n_tiles)
    def _():
        @pl.when(i >= 2)
        def _():
            wait_tile(slot)
        sbuf[slot] = hpk_ref[...]
        for a in range(rows):
            pltpu.make_async_copy(sbuf.at[slot, pl.ds(a % tm, 1)], xs_hbm.at[pl.ds(pos_ref[a], 1)],
                                  sem.at[slot]).start(priority=a % 2)

    @pl.when(i == n_tiles)
    def _():
        for t in range(max(n_tiles - 2, 0), n_tiles):
            wait_tile(t % 2)

        def tail_copy(b):
            r0 = pl.multiple_of(b * rb, rb)
            return pltpu.make_async_copy(zbuf, xs_hbm.at[pl.ds(r0, rb)], sem.at[pad_sem])

        def tail_start(b, c):
            tail_copy(b).start()
            return c

        def tail_wait(b, c):
            tail_copy(b).wait()
            return c
        lax.fori_loop(n_used_ref[0], n_blk, tail_start, 0)
        lax.fori_loop(n_used_ref[0], n_blk, tail_wait, 0)

    @pl.when(i >= n_tiles)
    def _():
        e = i - n_tiles
        base = pad_base_ref[e]

        def pad_copy(j):
            return pltpu.make_async_copy(zbuf.at[pl.ds(0, 1)], xs_hbm.at[pl.ds(base + j, 1)], sem.at[pad_sem])

        def start(j, c):
            pad_copy(j).start()
            return c

        def wait(j, c):
            pad_copy(j).wait()
            return c
        lax.fori_loop(0, pad_n_ref[e], start, 0)
        lax.fori_loop(0, pad_n_ref[e], wait, 0)


def _dispatch(hpk_all, pos_tiles, pad_base, pad_n, n_used, *, tm, rb, n_blk):
    T = hpk_all.shape[0]
    n_tiles = T // tm
    rows = TOP_K * tm
    grid_spec = pltpu.PrefetchScalarGridSpec(
        num_scalar_prefetch=3,
        grid=(n_tiles + N_EXPERTS,),
        in_specs=[pl.BlockSpec((rows,), lambda i, pb, pn, nu: (jnp.minimum(i, n_tiles - 1),),
                               memory_space=pltpu.SMEM),
                  pl.BlockSpec((tm, D_MODEL // 2), lambda i, pb, pn, nu: (jnp.minimum(i, n_tiles - 1), 0))],
        out_specs=pl.BlockSpec(memory_space=pl.ANY),
        scratch_shapes=[pltpu.VMEM((2, tm, D_MODEL // 2), jnp.uint32),
                        pltpu.VMEM((rb, D_MODEL // 2), jnp.uint32), pltpu.SemaphoreType.DMA((3,))],
    )
    return pl.pallas_call(
        functools.partial(_dispatch_kernel, tm=tm, n_tiles=n_tiles, rb=rb, n_blk=n_blk),
        grid_spec=grid_spec,
        out_shape=jax.ShapeDtypeStruct((n_blk * rb, D_MODEL // 2), jnp.uint32),
        compiler_params=pltpu.CompilerParams(dimension_semantics=("arbitrary",)),
        name="dispatch",
    )(pad_base, pad_n, n_used, pos_tiles, hpk_all)


def _experts_kernel(blk_e_ref, n_used_ref, next_e_ref, x_ref, wg_hbm, wu_hbm, wd_hbm, y_ref,
                    wg_s, wu_s, wd_s, wsem, wg_b, wu_b, wd_b, prev_e):
    i = pl.program_id(0)
    n_used = n_used_ref[0]

    def weight_copies(e):
        return (pltpu.make_async_copy(wg_hbm.at[e], wg_s, wsem.at[0]),
                pltpu.make_async_copy(wu_hbm.at[e], wu_s, wsem.at[1]),
                pltpu.make_async_copy(wd_hbm.at[e], wd_s, wsem.at[2]))

    @pl.when(i == 0)
    def _():
        prev_e[0] = -1
        for c in weight_copies(blk_e_ref[0]):
            c.start()

    @pl.when(i < n_used)
    def _():
        e = blk_e_ref[i]

        @pl.when(e != prev_e[0])
        def _():
            for c in weight_copies(e):
                c.wait()
            wg_b[...] = wg_s[...].astype(BF16)
            wu_b[...] = wu_s[...].astype(BF16)
            wd_b[...] = wd_s[...].astype(BF16)
            prev_e[0] = e
            nxt = next_e_ref[i]

            @pl.when(nxt >= 0)
            def _():
                for c in weight_copies(nxt):
                    c.start()

        xb = _unpack_bf16_pairs(x_ref[...]).astype(BF16)
        hid = (_silu(_dot(xb, wg_b[...])) * _dot(xb, wu_b[...])).astype(BF16)
        y = _dot(hid, wd_b[...])
        y_ref[...] = _pack_bf16_pairs(y.astype(BF16).astype(F32))

    @pl.when(i >= n_used)
    def _():
        y_ref[...] = jnp.zeros(y_ref.shape, jnp.uint32)


def _experts(xs_rows, blk_e, n_used, next_e, w_gate, w_up, w_down, *, rb):
    n_blk = blk_e.shape[0]
    n_rows = n_blk * rb
    hbm = pl.BlockSpec(memory_space=pl.ANY)
    grid_spec = pltpu.PrefetchScalarGridSpec(
        num_scalar_prefetch=3,
        grid=(n_blk,),
        in_specs=[pl.BlockSpec((rb, D_MODEL // 2), lambda i, be, nu, ne: (jnp.minimum(i, nu[0] - 1), 0)),
                  hbm, hbm, hbm],
        out_specs=pl.BlockSpec((rb, D_MODEL // 2), lambda i, be, nu, ne: (i, 0)),
        scratch_shapes=[pltpu.VMEM((D_MODEL, D_EXPERT), F32), pltpu.VMEM((D_MODEL, D_EXPERT), F32),
                        pltpu.VMEM((D_EXPERT, D_MODEL), F32), pltpu.SemaphoreType.DMA((3,)),
                        pltpu.VMEM((D_MODEL, D_EXPERT), BF16), pltpu.VMEM((D_MODEL, D_EXPERT), BF16),
                        pltpu.VMEM((D_EXPERT, D_MODEL), BF16), pltpu.SMEM((1,), jnp.int32)],
    )
    return pl.pallas_call(
        _experts_kernel,
        grid_spec=grid_spec,
        out_shape=jax.ShapeDtypeStruct((n_rows, D_MODEL // 2), jnp.uint32),
        compiler_params=pltpu.CompilerParams(dimension_semantics=("arbitrary",),
                                             vmem_limit_bytes=VMEM_LIMIT),
        name="experts",
    )(blk_e, n_used, next_e, xs_rows, w_gate, w_up, w_down)


def _combine_kernel(pos_ref, pos_n1_ref, pos_n2_ref, y_hbm, h_ref, route_ref, g2_ref, b2_ref, op_ref, os_ref,
                    ybuf, sem, *, tm, n_prompt_tiles):
    i = pl.program_id(0)
    n = pl.num_programs(0)
    slot = i % GATHER_SLOTS
    rows = TOP_K * tm

    def row_copy(idx, dst_slot, r):
        return pltpu.make_async_copy(y_hbm.at[pl.ds(idx, 1)], ybuf.at[dst_slot, pl.ds(r, 1)],
                                     sem.at[dst_slot])

    def wait_tile(s):
        pltpu.make_async_copy(y_hbm.at[pl.ds(0, rows)], ybuf.at[s], sem.at[s]).wait()

    @pl.when(i == 0)
    def _():
        def body(r, c):
            row_copy(pos_ref[r], 0, r).start()
            row_copy(pos_n1_ref[r], 1, r).start()
            return c
        lax.fori_loop(0, rows, body, 0)

    wait_tile(slot)
    gate1 = route_ref[:, 2:3]
    gate2 = route_ref[:, 3:4]
    f = (_unpack_bf16_pairs(ybuf[slot, 0:tm, :]) * gate1
         + _unpack_bf16_pairs(ybuf[slot, tm:rows, :]) * gate2)
    ahead = (i + GATHER_AHEAD) % GATHER_SLOTS
    for r in range(rows):
        row_copy(pos_n2_ref[r], ahead, r).start(priority=r % 2)
    out = _layernorm(ALPHA * h_ref[...] + f, g2_ref[...], b2_ref[...])

    @pl.when(i < n_prompt_tiles)
    def _():
        op_ref[...] = out

    @pl.when(i >= n_prompt_tiles)
    def _():
        os_ref[...] = out

    @pl.when(i == n - 1)
    def _():
        wait_tile((i + 1) % GATHER_SLOTS)
        wait_tile(ahead)


def _combine(y_rows, pos_tiles, h_all, route, pw, *, tm, t_prompt):
    T = h_all.shape[0]
    n = T // tm
    npt = t_prompt // tm
    rows = TOP_K * tm
    return pl.pallas_call(
        functools.partial(_combine_kernel, tm=tm, n_prompt_tiles=npt),
        grid=(n,),
        in_specs=[
            pl.BlockSpec((rows,), lambda i: (i,), memory_space=pltpu.SMEM),
            pl.BlockSpec((rows,), lambda i: (jnp.minimum(i + 1, n - 1),), memory_space=pltpu.SMEM),
            pl.BlockSpec((rows,), lambda i: (jnp.minimum(i + GATHER_AHEAD, n - 1),), memory_space=pltpu.SMEM),
            pl.BlockSpec(memory_space=pl.ANY),
            pl.BlockSpec((tm, D_MODEL), lambda i: (i, 0)),
            pl.BlockSpec((tm, LANES), lambda i: (i, 0)),
            _const_spec((1, D_MODEL)), _const_spec((1, D_MODEL)),
        ],
        out_specs=[pl.BlockSpec((tm, D_MODEL), lambda i: (jnp.minimum(i, npt - 1), 0)),
                   pl.BlockSpec((tm, D_MODEL), lambda i: (jnp.maximum(i - npt, 0), 0))],
        out_shape=[jax.ShapeDtypeStruct((t_prompt, D_MODEL), F32),
                   jax.ShapeDtypeStruct((T - t_prompt, D_MODEL), F32)],
        scratch_shapes=[pltpu.VMEM((GATHER_SLOTS, rows, D_MODEL // 2), jnp.uint32),
                        pltpu.SemaphoreType.DMA((GATHER_SLOTS,))],
        compiler_params=pltpu.CompilerParams(dimension_semantics=("arbitrary",),
                                             vmem_limit_bytes=VMEM_LIMIT),
        name="combine",
    )(pos_tiles, pos_tiles, pos_tiles, y_rows, h_all, route, pw["ln2_g"], pw["ln2_b"])


def _prep_weights(ln0_g, ln0_b, w_in, q_norm, w_uq, kv_norm, w_uk, w_uv, conv_w, conv_b, dt_bias, a_log,
                  d_skip, ssm_norm, w_o, ln1_g, ln1_b, w_rg, b_rg, w_re, b_re, ln2_g, ln2_b):
    s_q, s_kv, s_kr, s_z, s_xbc = Q_LORA, Q_LORA + KV_LORA, Q_LORA + KV_LORA + QK_ROPE, \
        Q_LORA + KV_LORA + QK_ROPE + D_SSM, Q_LORA + KV_LORA + QK_ROPE + D_SSM + CONV_CH
    zc = lambda n: jnp.zeros((D_MODEL, n), F32)
    w_in_p = jnp.concatenate([
        w_in[:, :s_q], w_in[:, s_q:s_kv], w_in[:, s_kr:s_z], w_in[:, s_z:s_xbc],
        w_in[:, s_kv:s_kr], zc(LANES - QK_ROPE), w_in[:, s_xbc:], zc(LANES - SSM_HEADS)], axis=1)
    wq = w_uq.reshape(Q_LORA, MLA_HEADS, QK_DIM) * (ATTN_SCALE * math.log2(math.e))
    wq = jnp.concatenate([wq, jnp.zeros((Q_LORA, MLA_HEADS, QK_PAD - QK_DIM), F32)], axis=2)
    w_r = jnp.concatenate([w_rg, w_re, jnp.zeros((D_MODEL, LANES - R_OFF - N_EXPERTS), F32)], axis=1)
    w_r_hi = w_r.astype(BF16)
    row = lambda v: v.reshape(1, -1)
    pad_row = lambda v: jnp.pad(v, (0, LANES - v.shape[0])).reshape(1, LANES)
    return dict(
        ln0_g=row(ln0_g), ln0_b=row(ln0_b), w_in=w_in_p.astype(BF16),
        q_norm=row(q_norm), kv_norm=row(kv_norm),
        w_uq=wq.reshape(Q_LORA, MLA_HEADS * QK_PAD).astype(BF16),
        w_ukv=jnp.concatenate([w_uk.reshape(KV_LORA, D_ATTN), w_uv.reshape(KV_LORA, D_ATTN)],
                              axis=1).astype(BF16),
        w_ukt=jnp.transpose(w_uk, (1, 2, 0)).astype(BF16),
        w_uvh=jnp.transpose(w_uv, (1, 0, 2)).astype(BF16),
        conv_w=conv_w, conv_b=row(conv_b), dt_bias=pad_row(dt_bias),
        a_neg=pad_row(-jnp.exp(a_log)), d_skip=row(jnp.repeat(d_skip, SSM_HEAD_DIM)),
        ssm_norm=row(ssm_norm),
        w_o_a=w_o[:D_ATTN].astype(BF16), w_o_s=w_o[D_ATTN:].astype(BF16),
        ln1_g=row(ln1_g), ln1_b=row(ln1_b),
        w_r=jnp.concatenate([w_r_hi, (w_r - w_r_hi.astype(F32)).astype(BF16)], axis=1),
        b_r=pad_row(jnp.concatenate([b_rg, b_re])),
        ln2_g=row(ln2_g), ln2_b=row(ln2_b),
    )


def _rope_tables(pos):
    half = QK_ROPE // 2
    inv_freq = ROPE_THETA ** (-jnp.arange(half, dtype=F32) / half)
    ang = pos.astype(F32)[:, None] * inv_freq[None, :]
    cos, sin = jnp.cos(ang), jnp.sin(ang)
    zeros = jnp.zeros((pos.shape[0], LANES - QK_ROPE), F32)
    return (jnp.concatenate([cos, cos, zeros], axis=1), jnp.concatenate([-sin, sin, zeros], axis=1))


def _dispatch_tables(route, counts, *, rb, n_blk):
    experts = jnp.arange(N_EXPERTS, dtype=jnp.int32)
    e = route[:, 0:TOP_K].astype(jnp.int32)
    rank = route[:, 4:4 + TOP_K].astype(jnp.int32)
    counts = counts.astype(jnp.int32)
    pcounts = (counts + rb - 1) // rb * rb
    pend = jnp.cumsum(pcounts)
    pstart = pend - pcounts
    pos = jnp.sum(jnp.where(e[:, :, None] == experts, pstart, 0), axis=2) + rank
    blk_start = jnp.arange(n_blk, dtype=jnp.int32) * rb
    blk_e = jnp.minimum(jnp.sum((pend[None, :] <= blk_start[:, None]).astype(jnp.int32), axis=1),
                        N_EXPERTS - 1)
    n_used = (pend[-1] // rb).reshape(1)
    blk_idx = jnp.arange(n_blk, dtype=jnp.int32)
    later = (blk_idx[None, :] > blk_idx[:, None]) & (blk_idx[None, :] < n_used) & (blk_e[None, :] > blk_e[:, None])
    next_e = jnp.min(jnp.where(later, blk_e[None, :], N_EXPERTS), axis=1)
    next_e = jnp.where(next_e < N_EXPERTS, next_e, -1).astype(jnp.int32)
    pad_base = pstart + counts
    pad_n = pcounts - counts
    return blk_e, n_used, next_e, pos, pad_base, pad_n


def _largest_tile(n, cap):
    t = cap
    while n % t:
        t //= 2
    return t


def kernel(x_prompt, x_sample, cache_kv_latent, cache_k_rope, state_conv, state_ssm, ln0_g, ln0_b, w_in, q_norm, w_uq, kv_norm, w_uk, w_uv, conv_w, conv_b, dt_bias, a_log, d_skip, ssm_norm, w_o, ln1_g, ln1_b, w_rg, b_rg, w_re, b_re, w_gate, w_up, w_down, ln2_g, ln2_b):
    B, L, _ = x_prompt.shape
    Bs, Ls, _ = x_sample.shape
    past = cache_kv_latent.shape[2]
    Tp, Ts = B * L, Bs * Ls
    pw = _prep_weights(ln0_g, ln0_b, w_in[0], q_norm[0], w_uq[0], kv_norm[0], w_uk[0], w_uv[0], conv_w[0],
                       conv_b[0], dt_bias[0], a_log[0], d_skip[0], ssm_norm[0], w_o[0], ln1_g[0], ln1_b[0],
                       w_rg[0], b_rg[0], w_re[0], b_re[0], ln2_g[0], ln2_b[0])

    tm_p = _largest_tile(L, 512)
    xp = x_prompt.reshape(Tp, D_MODEL)
    cos_p, sin_p = _rope_tables(jnp.arange(L, dtype=jnp.int32))
    q_p, kv_p, lat_p, kr_p, krp_p, z_p, xbc_p, dt_p = _inproj(
        xp, pw, cos_p, sin_p, tm=tm_p, pos_blocks=L // tm_p, with_kv=True)
    attn_p = _attn_prompt(q_p.reshape(B, L, -1), kv_p.reshape(B, L, -1), krp_p.reshape(B, L, -1),
                          tq=_largest_tile(L, 512), tk=_largest_tile(L, 1024))
    q_ssd = min(CHUNK, L)
    y_p, conv_p, ssm_p = _ssd(
        xbc_p.reshape(B, L, -1), dt_p.reshape(B, L, -1), z_p.reshape(B, L, -1),
        jnp.zeros((B, CONV_K - 1, CONV_CH), F32), jnp.zeros((B, D_SSM, D_STATE), F32), pw,
        q=q_ssd, tt=_largest_tile(L, 512))

    xs = x_sample.reshape(Ts, D_MODEL)
    cos_s, sin_s = _rope_tables(past + jnp.arange(Ls, dtype=jnp.int32))
    tm_s = _largest_tile(Ts, 256)
    assert tm_s % Ls == 0
    q_s, _, lat_s, kr_s, krp_s, z_s, xbc_s, dt_s = _inproj(
        xs, pw, jnp.tile(cos_s, (tm_s // Ls, 1)), jnp.tile(sin_s, (tm_s // Ls, 1)),
        tm=tm_s, pos_blocks=1, with_kv=False)
    attn_s = _attn_sample(q_s.reshape(Bs, Ls, -1), pw, cache_kv_latent[0], cache_k_rope[0],
                          lat_s.reshape(Bs, Ls, -1), krp_s.reshape(Bs, Ls, -1),
                          tk=_largest_tile(past, 1024))
    y_s, conv_s, ssm_s = _ssd(
        xbc_s.reshape(Bs, Ls, -1), dt_s.reshape(Bs, Ls, -1), z_s.reshape(Bs, Ls, -1),
        state_conv[0], state_ssm[0].reshape(Bs, D_SSM, D_STATE), pw, q=min(CHUNK, Ls), tt=Ls)

    T = Tp + Ts
    tm_c = _largest_tile(math.gcd(Tp, Ts), 256)
    h_all, hpk_all, route, counts = _outproj(xp, xs, attn_p.reshape(Tp, -1), attn_s.reshape(Ts, -1),
                                             y_p.reshape(Tp, -1), y_s.reshape(Ts, -1), pw, tm=tm_c)
    rb = 256
    n_blk = (T * TOP_K + N_EXPERTS * (rb - 1) + rb - 1) // rb
    blk_e, n_used, next_e, pos, pad_base, pad_n = _dispatch_tables(
        route, counts[0, R_OFF:R_OFF + N_EXPERTS], rb=rb, n_blk=n_blk)
    pos_tiles = pos.reshape(T // tm_c, tm_c, TOP_K).transpose(0, 2, 1).reshape(-1)
    xs_rows = _dispatch(hpk_all, pos_tiles, pad_base, pad_n, n_used, tm=tm_c, rb=rb, n_blk=n_blk)
    y_rows = _experts(xs_rows, blk_e, n_used, next_e, w_gate[0], w_up[0], w_down[0], rb=rb)
    out_p, out_s = _combine(y_rows, pos_tiles, h_all, route, pw, tm=tm_c, t_prompt=Tp)

    return (out_p.reshape(B, L, D_MODEL), out_s.reshape(Bs, Ls, D_MODEL),
            lat_p.reshape(1, B, L, KV_LORA), lat_s.reshape(1, Bs, Ls, KV_LORA),
            kr_p.reshape(1, B, L, QK_ROPE), kr_s.reshape(1, Bs, Ls, QK_ROPE),
            conv_p[None], conv_s[None],
            ssm_p.reshape(1, B, SSM_HEADS, SSM_HEAD_DIM, D_STATE),
            ssm_s.reshape(1, Bs, SSM_HEADS, SSM_HEAD_DIM, D_STATE))
```

```python
import functools
import math

import jax
import jax.numpy as jnp
import numpy as np
from jax import lax
from jax.experimental import pallas as pl
from jax.experimental.pallas import tpu as pltpu

F32 = jnp.float32
BF16 = jnp.bfloat16

D_MODEL = 2048
CHUNK = 64
MLA_HEADS = 8
QK_NOPE = 128
QK_ROPE = 64
QK_DIM = QK_NOPE + QK_ROPE
V_DIM = 128
Q_LORA = 512
KV_LORA = 512
ROPE_THETA = 10000.0
ATTN_SCALE = QK_DIM ** -0.5
D_ATTN = MLA_HEADS * V_DIM
D_SSM = 1024
SSM_HEAD_DIM = 64
SSM_HEADS = D_SSM // SSM_HEAD_DIM
SSM_GROUPS = 2
HEADS_PER_GROUP = SSM_HEADS // SSM_GROUPS
D_STATE = 128
CONV_K = 4
CONV_CH = D_SSM + 2 * SSM_GROUPS * D_STATE
N_EXPERT_GROUPS = 8
EXPERTS_PER_GROUP = 8
N_EXPERTS = N_EXPERT_GROUPS * EXPERTS_PER_GROUP
TOP_K = 2
D_EXPERT = 512
DEPTH = 1
ALPHA = (2 * DEPTH) ** 0.25
RMS_EPS = 1e-6
LN_EPS = 1e-5

LANES = 128
MXU_DIM = 256
VMEM_LIMIT = 56 * 1024 * 1024

QK_PAD = 2 * LANES
GATHER_AHEAD = 2
GATHER_SLOTS = GATHER_AHEAD + 1
C_Q, C_KV, C_Z, C_XBC, C_KR, C_DT = 0, 512, 1024, 2048, 3584, 3712
D_IN_PAD = 3840


def _const_spec(shape):
    nd = len(shape)
    return pl.BlockSpec(shape, lambda *_: (0,) * nd, pipeline_mode=pl.Buffered(1))


def _split3(a):
    hi = a.astype(BF16)
    r1 = a - hi.astype(F32)
    mid = r1.astype(BF16)
    lo = (r1 - mid.astype(F32)).astype(BF16)
    return hi, mid, lo


def _dot(a, b):
    return jnp.dot(a, b, preferred_element_type=F32)


def _dot_nt(a, b):
    return lax.dot_general(a, b, (((1,), (1,)), ((), ())), preferred_element_type=F32)


def _dot_tn(a, b):
    return lax.dot_general(a, b, (((0,), (0,)), ((), ())), preferred_element_type=F32)


def _exact_dot(a_f32, sel_bf16):
    hi, mid, lo = _split3(a_f32)
    return _dot(hi, sel_bf16) + _dot(mid, sel_bf16) + _dot(lo, sel_bf16)


def _exact_dot_l(sel_bf16, a_f32):
    hi, mid, lo = _split3(a_f32)
    return _dot(sel_bf16, hi) + _dot(sel_bf16, mid) + _dot(sel_bf16, lo)


def _layernorm(x, g, b):
    mu = jnp.mean(x, axis=-1, keepdims=True)
    xc = x - mu
    var = jnp.mean(xc * xc, axis=-1, keepdims=True)
    return xc * lax.rsqrt(var + LN_EPS) * g + b


def _rmsnorm(x, g):
    return x * lax.rsqrt(jnp.mean(x * x, axis=-1, keepdims=True) + RMS_EPS) * g


def _silu(x):
    return x * (1.0 / (1.0 + jnp.exp(-x)))


def _pack_bf16_pairs(x):
    n = x.shape[1] // 2
    bits = lax.bitcast_convert_type(x, jnp.uint32)
    return (bits[:, n:] & jnp.uint32(0xFFFF0000)) | (bits[:, :n] >> 16)


def _unpack_bf16_pairs(u):
    lo = lax.bitcast_convert_type(u << 16, F32)
    hi = lax.bitcast_convert_type(u & jnp.uint32(0xFFFF0000), F32)
    return jnp.concatenate([lo, hi], axis=1)


def _rope128(x, cos_t, sin_t):
    lane = lax.broadcasted_iota(jnp.int32, x.shape, 1)
    partner = jnp.where(lane < QK_ROPE // 2, pltpu.roll(x, LANES - QK_ROPE // 2, 1),
                        pltpu.roll(x, QK_ROPE // 2, 1))
    return x * cos_t + partner * sin_t


def _inproj_kernel(x_ref, g0_ref, b0_ref, w_in_ref, qn_ref, kvn_ref, w_uq_ref, w_ukv_ref,
                   cos_ref, sin_ref, dtb_ref,
                   q_ref, kv_ref, lat_ref, kr_ref, krp_ref, z_ref, xbc_ref, dt_ref, *, with_kv):
    xn = _layernorm(x_ref[...], g0_ref[...], b0_ref[...]).astype(BF16)
    cos_t = cos_ref[...]
    sin_t = sin_ref[...]

    c_q = _dot(xn, w_in_ref[:, C_Q:C_Q + Q_LORA])
    qb = _rmsnorm(c_q, qn_ref[...]).astype(BF16)
    for h in range(MLA_HEADS):
        qh = _dot(qb, w_uq_ref[:, h * QK_PAD:(h + 1) * QK_PAD])
        q_ref[:, h * QK_PAD:h * QK_PAD + LANES] = qh[:, :LANES].astype(BF16)
        q_ref[:, h * QK_PAD + LANES:(h + 1) * QK_PAD] = _rope128(qh[:, LANES:], cos_t, sin_t).astype(BF16)

    c_kv = _dot(xn, w_in_ref[:, C_KV:C_KV + KV_LORA])
    lat = _rmsnorm(c_kv, kvn_ref[...])
    lat_ref[...] = lat
    if with_kv:
        kv_ref[...] = _dot(lat.astype(BF16), w_ukv_ref[...]).astype(BF16)
    else:
        kv_ref[...] = jnp.zeros(kv_ref.shape, BF16)

    k_r = _rope128(_dot(xn, w_in_ref[:, C_KR:C_KR + LANES]), cos_t, sin_t)
    kr_ref[...] = k_r[:, :QK_ROPE]
    krp_ref[...] = k_r.astype(BF16)

    z_ref[...] = _dot(xn, w_in_ref[:, C_Z:C_Z + D_SSM]).astype(BF16)
    xbc_ref[...] = _dot(xn, w_in_ref[:, C_XBC:C_XBC + CONV_CH])

    dt_raw = _dot(xn, w_in_ref[:, C_DT:C_DT + LANES]) + dtb_ref[...]
    dt = jnp.maximum(dt_raw, 0.0) + jnp.log1p(jnp.exp(-jnp.abs(dt_raw)))
    lane = lax.broadcasted_iota(jnp.int32, dt.shape, 1)
    dt_ref[...] = jnp.where(lane < SSM_HEADS, dt, 0.0)


def _inproj(x, pw, cos_t, sin_t, *, tm, pos_blocks, with_kv):
    T = x.shape[0]
    n = T // tm
    row = lambda i: (i, 0)
    pos = lambda i: (i % pos_blocks, 0)
    kv_cols = 2 * D_ATTN if with_kv else LANES
    out_shape = [
        jax.ShapeDtypeStruct((T, MLA_HEADS * QK_PAD), BF16),
        jax.ShapeDtypeStruct((T, kv_cols), BF16),
        jax.ShapeDtypeStruct((T, KV_LORA), F32),
        jax.ShapeDtypeStruct((T, QK_ROPE), F32),
        jax.ShapeDtypeStruct((T, LANES), BF16),
        jax.ShapeDtypeStruct((T, D_SSM), BF16),
        jax.ShapeDtypeStruct((T, CONV_CH), F32),
        jax.ShapeDtypeStruct((T, LANES), F32),
    ]
    out_specs = [pl.BlockSpec((tm, s.shape[1]), row) for s in out_shape]
    in_specs = [
        pl.BlockSpec((tm, D_MODEL), row),
        _const_spec((1, D_MODEL)), _const_spec((1, D_MODEL)),
        _const_spec((D_MODEL, D_IN_PAD)),
        _const_spec((1, Q_LORA)), _const_spec((1, KV_LORA)),
        _const_spec((Q_LORA, MLA_HEADS * QK_PAD)),
        _const_spec((KV_LORA, 2 * D_ATTN)),
        pl.BlockSpec((tm, LANES), pos), pl.BlockSpec((tm, LANES), pos),
        _const_spec((1, LANES)),
    ]
    return pl.pallas_call(
        functools.partial(_inproj_kernel, with_kv=with_kv),
        grid=(n,), in_specs=in_specs, out_specs=out_specs, out_shape=out_shape,
        compiler_params=pltpu.CompilerParams(dimension_semantics=("arbitrary",),
                                             vmem_limit_bytes=VMEM_LIMIT),
        name="inproj",
    )(x, pw["ln0_g"], pw["ln0_b"], pw["w_in"], pw["q_norm"], pw["kv_norm"], pw["w_uq"],
      pw["w_ukv"], cos_t, sin_t, pw["dt_bias"])


def _attn_prompt_kernel(qi_ref, ki_ref, q_ref, kv_ref, krp_ref, o_ref, *scratch, tq, tk):
    m_refs = scratch[0:MLA_HEADS]
    acc_refs = scratch[MLA_HEADS:2 * MLA_HEADS]
    p_id = pl.program_id(1)
    qi = qi_ref[p_id]
    ki = ki_ref[p_id]
    k_last = ((qi + 1) * tq - 1) // tk

    @pl.when(ki == 0)
    def _():
        for h in range(MLA_HEADS):
            m_refs[h][...] = jnp.full(m_refs[h].shape, -jnp.inf, F32)
            acc_refs[h][...] = jnp.zeros(acc_refs[h].shape, F32)

    def step(masked):
        krp = krp_ref[...]
        ones_col = (lax.broadcasted_iota(jnp.int32, (tk, LANES), 1) == 0).astype(BF16)
        if masked:
            r = (qi * tq + lax.broadcasted_iota(jnp.int32, (tq, tk), 0)) // CHUNK
            c = (ki * tk + lax.broadcasted_iota(jnp.int32, (tq, tk), 1)) // CHUNK
            allowed = c <= r
        for h in range(MLA_HEADS):
            qh = q_ref[:, h * QK_PAD:(h + 1) * QK_PAD]
            kh = jnp.concatenate([kv_ref[:, h * QK_NOPE:(h + 1) * QK_NOPE], krp], axis=1)
            s = _dot_nt(qh, kh).astype(BF16)
            if masked:
                s = jnp.where(allowed, s, -jnp.inf)
            m_old = m_refs[h][...]
            m_new = jnp.maximum(m_old, jnp.max(s, axis=1, keepdims=True).astype(F32))
            alpha = jnp.exp2(m_old - m_new)
            p = jnp.exp2(s - m_new.astype(BF16))
            v_aug = jnp.concatenate([kv_ref[:, D_ATTN + h * V_DIM:D_ATTN + (h + 1) * V_DIM], ones_col], axis=1)
            acc_refs[h][...] = alpha * acc_refs[h][...] + _dot(p, v_aug)
            m_refs[h][...] = m_new

    @pl.when(ki < k_last)
    def _():
        step(False)

    @pl.when(ki == k_last)
    def _():
        step(True)
        for h in range(MLA_HEADS):
            acc = acc_refs[h][...]
            o_ref[:, h * V_DIM:(h + 1) * V_DIM] = (acc[:, :V_DIM] / acc[:, V_DIM:V_DIM + 1]).astype(BF16)


def _attn_prompt(q, kv, krp, *, tq, tk):
    B, L, _ = q.shape
    nq = L // tq
    pairs = [(i, j) for i in range(nq) for j in range(((i + 1) * tq - 1) // tk + 1)]
    qi_tab = jnp.asarray(np.array([p[0] for p in pairs], np.int32))
    ki_tab = jnp.asarray(np.array([p[1] for p in pairs], np.int32))
    grid_spec = pltpu.PrefetchScalarGridSpec(
        num_scalar_prefetch=2,
        grid=(B, len(pairs)),
        in_specs=[
            pl.BlockSpec((None, tq, MLA_HEADS * QK_PAD), lambda b, p, qi, ki: (b, qi[p], 0)),
            pl.BlockSpec((None, tk, 2 * D_ATTN), lambda b, p, qi, ki: (b, ki[p], 0)),
            pl.BlockSpec((None, tk, LANES), lambda b, p, qi, ki: (b, ki[p], 0)),
        ],
        out_specs=pl.BlockSpec((None, tq, D_ATTN), lambda b, p, qi, ki: (b, qi[p], 0)),
        scratch_shapes=([pltpu.VMEM((tq, 1), F32)] * MLA_HEADS
                        + [pltpu.VMEM((tq, 2 * V_DIM), F32)] * MLA_HEADS),
    )
    return pl.pallas_call(
        functools.partial(_attn_prompt_kernel, tq=tq, tk=tk),
        grid_spec=grid_spec,
        out_shape=jax.ShapeDtypeStruct((B, L, D_ATTN), BF16),
        compiler_params=pltpu.CompilerParams(dimension_semantics=("arbitrary", "arbitrary"),
                                             vmem_limit_bytes=VMEM_LIMIT),
        name="attn_prompt",
    )(qi_tab, ki_tab, q, kv, krp)


def _attn_sample_kernel(q_ref, wukt_ref, wuv_ref, latp_ref, krpast_ref, latn_ref, krn_ref, o_ref,
                        qlat_ref, qrp_ref, m_ref, l_ref, acc_ref, *, ls, past, tk, n_new_pad):
    k = pl.program_id(1)
    nk = pl.num_programs(1)
    rows = MLA_HEADS * ls

    @pl.when(k == 0)
    def _():
        for h in range(MLA_HEADS):
            qn = q_ref[:, h * QK_PAD:h * QK_PAD + LANES]
            qlat_ref[h * ls:(h + 1) * ls, :] = _dot(qn, wukt_ref[h]).astype(BF16)
            qrp_ref[h * ls:(h + 1) * ls, :] = q_ref[:, h * QK_PAD + LANES:(h + 1) * QK_PAD]
        m_ref[...] = jnp.full(m_ref.shape, -jnp.inf, F32)
        l_ref[...] = jnp.zeros(l_ref.shape, F32)
        acc_ref[...] = jnp.zeros(acc_ref.shape, F32)

    q_chunk = (past + lax.broadcasted_iota(jnp.int32, (rows, 1), 0) % ls) // CHUNK

    def update(s, lat_b):
        m_old = m_ref[...]
        m_new = jnp.maximum(m_old, jnp.max(s, axis=1, keepdims=True))
        alpha = jnp.exp2(m_old - m_new)
        p = jnp.exp2(s - m_new)
        l_ref[...] = alpha * l_ref[...] + jnp.sum(p, axis=1, keepdims=True)
        acc_ref[...] = alpha * acc_ref[...] + _dot(p.astype(BF16), lat_b)
        m_ref[...] = m_new

    lat_b = latp_ref[...].astype(BF16)
    kr_b = krpast_ref[...].astype(BF16)
    s = _dot_nt(qlat_ref[...], lat_b) + _dot_nt(qrp_ref[:, :QK_ROPE], kr_b)
    k_pos = k * tk + lax.broadcasted_iota(jnp.int32, (rows, tk), 1)
    s = jnp.where(k_pos // CHUNK <= q_chunk, s, -jnp.inf)
    update(s, lat_b)

    @pl.when(k == nk - 1)
    def _():
        latn_b = latn_ref[...].astype(BF16)
        s2 = _dot_nt(qlat_ref[...], latn_b) + _dot_nt(qrp_ref[...], krn_ref[...])
        j = lax.broadcasted_iota(jnp.int32, (rows, n_new_pad), 1)
        ok = ((past + j) // CHUNK <= q_chunk) & (j < ls)
        update(jnp.where(ok, s2, -jnp.inf), latn_b)
        o_lat = (acc_ref[...] / l_ref[...]).astype(BF16)
        for h in range(MLA_HEADS):
            o_ref[:, h * V_DIM:(h + 1) * V_DIM] = _dot(o_lat[h * ls:(h + 1) * ls, :], wuv_ref[h]).astype(BF16)


def _attn_sample(q, pw, lat_past, kr_past, lat_new, krp_new, *, tk):
    Bs, Ls, _ = q.shape
    past = lat_past.shape[1]
    n_new_pad = LANES
    lat_new = jnp.pad(lat_new, ((0, 0), (0, n_new_pad - Ls), (0, 0)))
    krp_new = jnp.pad(krp_new, ((0, 0), (0, n_new_pad - Ls), (0, 0)))
    rows = MLA_HEADS * Ls
    return pl.pallas_call(
        functools.partial(_attn_sample_kernel, ls=Ls, past=past, tk=tk, n_new_pad=n_new_pad),
        grid=(Bs, past // tk),
        in_specs=[
            pl.BlockSpec((None, Ls, MLA_HEADS * QK_PAD), lambda b, k: (b, 0, 0)),
            _const_spec((MLA_HEADS, QK_NOPE, KV_LORA)),
            _const_spec((MLA_HEADS, KV_LORA, V_DIM)),
            pl.BlockSpec((None, tk, KV_LORA), lambda b, k: (b, k, 0)),
            pl.BlockSpec((None, tk, QK_ROPE), lambda b, k: (b, k, 0)),
            pl.BlockSpec((None, n_new_pad, KV_LORA), lambda b, k: (b, 0, 0)),
            pl.BlockSpec((None, n_new_pad, LANES), lambda b, k: (b, 0, 0)),
        ],
        out_specs=pl.BlockSpec((None, Ls, D_ATTN), lambda b, k: (b, 0, 0)),
        out_shape=jax.ShapeDtypeStruct((Bs, Ls, D_ATTN), BF16),
        scratch_shapes=[pltpu.VMEM((rows, KV_LORA), BF16), pltpu.VMEM((rows, LANES), BF16),
                        pltpu.VMEM((rows, 1), F32), pltpu.VMEM((rows, 1), F32),
                        pltpu.VMEM((rows, KV_LORA), F32)],
        compiler_params=pltpu.CompilerParams(dimension_semantics=("arbitrary", "arbitrary"),
                                             vmem_limit_bytes=VMEM_LIMIT),
        name="attn_sample",
    )(q, pw["w_ukt"], pw["w_uvh"], lat_past, kr_past, lat_new, krp_new)


def _ssd_consts(q):
    hb = MXU_DIM // q
    hq = SSM_HEADS * q
    lane = np.arange(hq)
    ltri = (np.arange(q)[None, :] <= np.arange(q)[:, None]).astype(np.float32)
    sel_c = (np.arange(LANES)[:, None] == (lane // q)[None, :]).astype(np.float32)
    sel_p = (np.arange(LANES)[:, None] == (np.arange(D_SSM) // SSM_HEAD_DIM)[None, :]).astype(np.float32)
    diag = (np.arange(q)[:, None] == (lane % q)[None, :]).astype(np.float32)
    causal = (np.arange(q)[:, None] >= (lane % q)[None, :]).astype(np.float32)
    bd_rows = np.arange(hb * q) // q
    bd_cols = np.arange(hb * SSM_HEAD_DIM) // SSM_HEAD_DIM
    bdmask = (bd_rows[:, None] == bd_cols[None, :]).astype(np.float32)
    return dict(ltri=jnp.asarray(ltri, BF16), sel_c=jnp.asarray(sel_c, BF16),
                sel_p=jnp.asarray(sel_p, BF16), diag=jnp.asarray(diag, F32),
                causal=jnp.asarray(causal, F32), bdmask=jnp.asarray(bdmask, BF16))


def _ssd_kernel(xbc_ref, dt_ref, z_ref, cprev_ref, h0_ref, cw_ref, cb_ref, a_ref, dskip_ref, norm_ref,
                ltri_ref, selc_ref, selp_ref, diag_ref, causal_ref, bdmask_ref,
                y_ref, cout_ref, hout_ref,
                ext_ref, u_ref, ht_ref, *, q, tt):
    t = pl.program_id(1)
    nt = pl.num_programs(1)
    hb = MXU_DIM // q
    nblk = SSM_HEADS // hb
    gw = HEADS_PER_GROUP * SSM_HEAD_DIM
    pad = 8

    @pl.when(t == 0)
    def _():
        ext_ref[0:pad, :] = jnp.zeros((pad, CONV_CH), F32)
        ext_ref[pad - (CONV_K - 1):pad, :] = cprev_ref[...]
        ht_ref[...] = h0_ref[...].T

    xbc = xbc_ref[...]
    ext_ref[pad:pad + tt, :] = xbc
    cout_ref[...] = xbc[tt - (CONV_K - 1):, :]
    conv = cb_ref[...] + cw_ref[0:1, :] * ext_ref[pad - 3:pad - 3 + tt, :]
    for k in range(1, CONV_K):
        conv = conv + cw_ref[k:k + 1, :] * ext_ref[pad - 3 + k:pad - 3 + k + tt, :]
    u_ref[...] = _silu(conv)
    ext_ref[pad - (CONV_K - 1):pad, :] = xbc[tt - (CONV_K - 1):, :]

    a_row = a_ref[...]
    d_row = dskip_ref[...]
    norm_row = norm_ref[...]

    def chunk(c, carry):
        r0 = pl.multiple_of(c * q, q)
        u = u_ref[pl.ds(r0, q), :]
        xs = u[:, :D_SSM]
        dt = dt_ref[pl.ds(r0, q), :]
        a_cs = _exact_dot_l(ltri_ref[...], dt * a_row)
        ap = _exact_dot(a_cs, selp_ref[...])
        dtp = _exact_dot(dt, selp_ref[...])
        a_i = ap if q == SSM_HEAD_DIM else _exact_dot(a_cs, selc_ref[...])
        a_j = jnp.sum(a_i * diag_ref[...], axis=0, keepdims=True)
        decay = jnp.exp(jnp.where(causal_ref[...] > 0.5, a_i - a_j, -jnp.inf))
        ap_last = ap[q - 1:q, :]
        xdt = (xs * dtp).astype(BF16)
        xw = (xs * (jnp.exp(ap_last - ap) * dtp)).astype(BF16)
        e_ap = jnp.exp(ap)

        cb_parts, y_off_parts = [], []
        for g in range(SSM_GROUPS):
            bm = u[:, D_SSM + g * D_STATE:D_SSM + (g + 1) * D_STATE].astype(BF16)
            cm = u[:, D_SSM + (SSM_GROUPS + g) * D_STATE:D_SSM + (SSM_GROUPS + g + 1) * D_STATE].astype(BF16)
            cb_parts.append(_dot_nt(cm, jnp.concatenate([bm] * HEADS_PER_GROUP, axis=0)))
            gl = slice(g * gw, (g + 1) * gw)
            ht_g = ht_ref[:, gl]
            y_off_parts.append(_dot(cm, ht_g.astype(BF16)) * e_ap[:, gl])
            st = _dot_tn(bm, xw[:, gl])
            ht_ref[:, gl] = ht_g * jnp.exp(ap_last[:, gl]) + st
        w_all = (jnp.concatenate(cb_parts, axis=1) * decay).astype(BF16)
        y_diag_parts = []
        for b in range(nblk):
            x_b = xdt[:, b * hb * SSM_HEAD_DIM:(b + 1) * hb * SSM_HEAD_DIM]
            bd = jnp.concatenate([x_b] * hb, axis=0) * bdmask_ref[...]
            y_diag_parts.append(_dot(w_all[:, b * MXU_DIM:(b + 1) * MXU_DIM], bd))
        y = jnp.concatenate(y_diag_parts, axis=1) + jnp.concatenate(y_off_parts, axis=1)

        y = y + d_row * xs
        y = y * _silu(z_ref[pl.ds(r0, q), :].astype(F32))
        outs = []
        for g in range(SSM_GROUPS):
            yg = y[:, g * gw:(g + 1) * gw]
            outs.append(yg * lax.rsqrt(jnp.mean(yg * yg, axis=1, keepdims=True) + RMS_EPS))
        y_ref[pl.ds(r0, q), :] = (jnp.concatenate(outs, axis=1) * norm_row).astype(BF16)
        return carry

    lax.fori_loop(0, tt // q, chunk, 0, unroll=True)

    @pl.when(t == nt - 1)
    def _():
        hout_ref[...] = ht_ref[...].T


def _ssd(xbc, dt, z, conv_prev, h0, pw, *, q, tt):
    B, L, _ = xbc.shape
    cs = _ssd_consts(q)
    hq = SSM_HEADS * q
    hb = MXU_DIM // q
    tile = lambda w: pl.BlockSpec((None, tt, w), lambda b, t: (b, t, 0))
    per_b = lambda s: pl.BlockSpec((None,) + s, lambda b, t: (b, 0, 0))
    return pl.pallas_call(
        functools.partial(_ssd_kernel, q=q, tt=tt),
        grid=(B, L // tt),
        in_specs=[tile(CONV_CH), tile(LANES), tile(D_SSM), per_b((CONV_K - 1, CONV_CH)),
                  per_b((D_SSM, D_STATE)),
                  _const_spec((CONV_K, CONV_CH)), _const_spec((1, CONV_CH)), _const_spec((1, LANES)),
                  _const_spec((1, D_SSM)), _const_spec((1, D_SSM)),
                  _const_spec((q, q)), _const_spec((LANES, hq)), _const_spec((LANES, D_SSM)),
                  _const_spec((q, hq)), _const_spec((q, hq)),
                  _const_spec((MXU_DIM, hb * SSM_HEAD_DIM))],
        out_specs=[tile(D_SSM), per_b((CONV_K - 1, CONV_CH)), per_b((D_SSM, D_STATE))],
        out_shape=[jax.ShapeDtypeStruct((B, L, D_SSM), BF16),
                   jax.ShapeDtypeStruct((B, CONV_K - 1, CONV_CH), F32),
                   jax.ShapeDtypeStruct((B, D_SSM, D_STATE), F32)],
        scratch_shapes=[pltpu.VMEM((8 + tt, CONV_CH), F32), pltpu.VMEM((tt, CONV_CH), F32),
                        pltpu.VMEM((D_STATE, D_SSM), F32)],
        compiler_params=pltpu.CompilerParams(dimension_semantics=("arbitrary", "arbitrary"),
                                             vmem_limit_bytes=VMEM_LIMIT),
        name="ssd",
    )(xbc, dt, z, conv_prev, h0, pw["conv_w"], pw["conv_b"], pw["a_neg"], pw["d_skip"], pw["ssm_norm"],
      cs["ltri"], cs["sel_c"], cs["sel_p"], cs["diag"], cs["causal"], cs["bdmask"])


R_OFF = N_EXPERT_GROUPS


def _outproj_kernel(xp_ref, xs_ref, attnp_ref, attns_ref, yp_ref, ys_ref, g0_ref, b0_ref, wo_a_ref, wo_s_ref,
                    g1_ref, b1_ref, wr_ref, br_ref, ltri_ref, h_ref, hpk_ref, route_ref, cnt_ref,
                    *, n_prompt_tiles, tm, n_groups):
    i = pl.program_id(0)
    is_p = i < n_prompt_tiles

    @pl.when(i == 0)
    def _():
        cnt_ref[...] = jnp.zeros(cnt_ref.shape, F32)

    cnt = cnt_ref[0:1, :]
    for g in range(n_groups):
        rows = slice(g * (tm // n_groups), (g + 1) * (tm // n_groups))
        refs = (xp_ref, xs_ref, attnp_ref, attns_ref, yp_ref, ys_ref)
        cnt = _outproj_rows(rows, is_p, cnt, refs, g0_ref, b0_ref, wo_a_ref, wo_s_ref, g1_ref, b1_ref,
                            wr_ref, br_ref, ltri_ref, h_ref, hpk_ref, route_ref)
    cnt_ref[...] = jnp.broadcast_to(cnt, cnt_ref.shape)


def _outproj_rows(rows, is_p, cnt, refs, g0_ref, b0_ref, wo_a_ref, wo_s_ref, g1_ref, b1_ref, wr_ref, br_ref,
                  ltri_ref, h_ref, hpk_ref, route_ref):
    xp_ref, xs_ref, attnp_ref, attns_ref, yp_ref, ys_ref = refs
    x = jnp.where(is_p, xp_ref[rows, :], xs_ref[rows, :])
    attn = jnp.where(is_p, attnp_ref[rows, :], attns_ref[rows, :])
    yssm = jnp.where(is_p, yp_ref[rows, :], ys_ref[rows, :])

    xn = _layernorm(x, g0_ref[...], b0_ref[...])
    mixed = _dot(attn, wo_a_ref[...]) + _dot(yssm, wo_s_ref[...])
    h = _layernorm(ALPHA * xn + mixed, g1_ref[...], b1_ref[...])
    h_ref[rows, :] = h

    h_hi = h.astype(BF16)
    h_hi32 = h_hi.astype(F32)
    hpk_ref[rows, :] = _pack_bf16_pairs(h_hi32)
    h_lo = (h - h_hi32).astype(BF16)
    tm = h.shape[0]
    prod = _dot(jnp.concatenate([h_hi, h_lo], axis=0), wr_ref[...])
    lg = (prod[:tm, :LANES] + prod[:tm, LANES:]) + (prod[tm:, :LANES] + prod[tm:, LANES:]) + br_ref[...]
    lane = lax.broadcasted_iota(jnp.int32, lg.shape, 1)
    big = jnp.int32(1 << 20)
    gl = jnp.where(lane < N_EXPERT_GROUPS, lg, -jnp.inf)
    gmax = jnp.max(gl, axis=1, keepdims=True)
    grp = jnp.min(jnp.where(gl == gmax, lane, big), axis=1, keepdims=True)
    g_w = 1.0 / jnp.sum(jnp.exp(gl - gmax), axis=1, keepdims=True)
    in_grp = (lane >= R_OFF) & (lane < R_OFF + N_EXPERTS) & ((lane - R_OFF) // EXPERTS_PER_GROUP == grp)
    el = jnp.where(in_grp, lg, -jnp.inf)
    emax = jnp.max(el, axis=1, keepdims=True)
    ee = jnp.exp(el - emax)
    prob = jnp.where(in_grp, ee / jnp.sum(ee, axis=1, keepdims=True), -1.0)
    p1 = jnp.max(prob, axis=1, keepdims=True)
    i1 = jnp.min(jnp.where(prob == p1, lane, big), axis=1, keepdims=True)
    prob2 = jnp.where(lane == i1, -1.0, prob)
    p2 = jnp.max(prob2, axis=1, keepdims=True)
    i2 = jnp.min(jnp.where(prob2 == p2, lane, big), axis=1, keepdims=True)
    denom = p1 + p2
    oh1 = (lane == i1).astype(F32)
    oh2 = (lane == i2).astype(F32)
    oh = oh1 + oh2
    before = _dot(ltri_ref[...], oh.astype(BF16)) + cnt
    rank1 = jnp.sum(before * oh1, axis=1, keepdims=True)
    rank2 = jnp.sum(before * oh2, axis=1, keepdims=True)
    route = jnp.where(lane == 0, (i1 - R_OFF).astype(F32),
                      jnp.where(lane == 1, (i2 - R_OFF).astype(F32),
                                jnp.where(lane == 2, g_w * p1 / denom,
                                          jnp.where(lane == 3, g_w * p2 / denom,
                                                    jnp.where(lane == 4, rank1,
                                                              jnp.where(lane == 5, rank2, 0.0))))))
    route_ref[rows, :] = route
    return cnt + jnp.sum(oh, axis=0, keepdims=True)


def _outproj(xp, xs, attn_p, attn_s, y_p, y_s, pw, *, tm):
    Tp, Ts = xp.shape[0], xs.shape[0]
    npt, nst = Tp // tm, Ts // tm
    T = Tp + Ts
    row = lambda i: (i, 0)
    prow = lambda i: (jnp.minimum(i, npt - 1), 0)
    srow = lambda i: (jnp.maximum(i - npt, 0), 0)
    n_groups = 1
    gm = tm // n_groups
    ltri = jnp.asarray(np.tril(np.ones((gm, gm), np.float32), -1), BF16)
    return pl.pallas_call(
        functools.partial(_outproj_kernel, n_prompt_tiles=npt, tm=tm, n_groups=n_groups),
        grid=(npt + nst,),
        in_specs=[pl.BlockSpec((tm, D_MODEL), prow), pl.BlockSpec((tm, D_MODEL), srow),
                  pl.BlockSpec((tm, D_ATTN), prow), pl.BlockSpec((tm, D_ATTN), srow),
                  pl.BlockSpec((tm, D_SSM), prow), pl.BlockSpec((tm, D_SSM), srow),
                  _const_spec((1, D_MODEL)), _const_spec((1, D_MODEL)),
                  _const_spec((D_ATTN, D_MODEL)), _const_spec((D_SSM, D_MODEL)),
                  _const_spec((1, D_MODEL)), _const_spec((1, D_MODEL)),
                  _const_spec((D_MODEL, 2 * LANES)), _const_spec((1, LANES)),
                  _const_spec((gm, gm))],
        out_specs=[pl.BlockSpec((tm, D_MODEL), row), pl.BlockSpec((tm, D_MODEL // 2), row),
                   pl.BlockSpec((tm, LANES), row), pl.BlockSpec((8, LANES), lambda i: (0, 0))],
        out_shape=[jax.ShapeDtypeStruct((T, D_MODEL), F32), jax.ShapeDtypeStruct((T, D_MODEL // 2), jnp.uint32),
                   jax.ShapeDtypeStruct((T, LANES), F32), jax.ShapeDtypeStruct((8, LANES), F32)],
        compiler_params=pltpu.CompilerParams(dimension_semantics=("arbitrary",),
                                             vmem_limit_bytes=VMEM_LIMIT),
        name="outproj",
    )(xp, xs, attn_p, attn_s, y_p, y_s, pw["ln0_g"], pw["ln0_b"], pw["w_o_a"], pw["w_o_s"],
      pw["ln1_g"], pw["ln1_b"], pw["w_r"], pw["b_r"], ltri)


def _dispatch_kernel(pad_base_ref, pad_n_ref, n_used_ref, pos_ref, hpk_ref, xs_hbm, sbuf, zbuf, sem,
                     *, tm, n_tiles, rb, n_blk):
    i = pl.program_id(0)
    rows = TOP_K * tm
    slot = i % 2
    pad_sem = 2

    def wait_tile(s):
        for _ in range(TOP_K):
            pltpu.make_async_copy(sbuf.at[s], xs_hbm.at[pl.ds(0, tm)], sem.at[s]).wait()

    @pl.when(i == 0)
    def _():
        zbuf[...] = jnp.zeros(zbuf.shape, jnp.uint32)

    @pl.when(i < n_tiles)
    def _():
        @pl.when(i >= 2)
        def _():
            wait_tile(slot)
        sbuf[slot] = hpk_ref[...]
        for a in range(rows):
            pltpu.make_async_copy(sbuf.at[slot, pl.ds(a % tm, 1)], xs_hbm.at[pl.ds(pos_ref[a], 1)],
                                  sem.at[slot]).start(priority=a % 2)

    @pl.when(i == n_tiles)
    def _():
        for t in range(max(n_tiles - 2, 0), n_tiles):
            wait_tile(t % 2)

        def tail_copy(b):
            r0 = pl.multiple_of(b * rb, rb)
            return pltpu.make_async_copy(zbuf, xs_hbm.at[pl.ds(r0, rb)], sem.at[pad_sem])

        def tail_start(b, c):
            tail_copy(b).start()
            return c

        def tail_wait(b, c):
            tail_copy(b).wait()
            return c
        lax.fori_loop(n_used_ref[0], n_blk, tail_start, 0)
        lax.fori_loop(n_used_ref[0], n_blk, tail_wait, 0)

    @pl.when(i >= n_tiles)
    def _():
        e = i - n_tiles
        base = pad_base_ref[e]

        def pad_copy(j):
            return pltpu.make_async_copy(zbuf.at[pl.ds(0, 1)], xs_hbm.at[pl.ds(base + j, 1)], sem.at[pad_sem])

        def start(j, c):
            pad_copy(j).start()
            return c

        def wait(j, c):
            pad_copy(j).wait()
            return c
        lax.fori_loop(0, pad_n_ref[e], start, 0)
        lax.fori_loop(0, pad_n_ref[e], wait, 0)


def _dispatch(hpk_all, pos_tiles, pad_base, pad_n, n_used, *, tm, rb, n_blk):
    T = hpk_all.shape[0]
    n_tiles = T // tm
    rows = TOP_K * tm
    grid_spec = pltpu.PrefetchScalarGridSpec(
        num_scalar_prefetch=3,
        grid=(n_tiles + N_EXPERTS,),
        in_specs=[pl.BlockSpec((rows,), lambda i, pb, pn, nu: (jnp.minimum(i, n_tiles - 1),),
                               memory_space=pltpu.SMEM),
                  pl.BlockSpec((tm, D_MODEL // 2), lambda i, pb, pn, nu: (jnp.minimum(i, n_tiles - 1), 0))],
        out_specs=pl.BlockSpec(memory_space=pl.ANY),
        scratch_shapes=[pltpu.VMEM((2, tm, D_MODEL // 2), jnp.uint32),
                        pltpu.VMEM((rb, D_MODEL // 2), jnp.uint32), pltpu.SemaphoreType.DMA((3,))],
    )
    return pl.pallas_call(
        functools.partial(_dispatch_kernel, tm=tm, n_tiles=n_tiles, rb=rb, n_blk=n_blk),
        grid_spec=grid_spec,
        out_shape=jax.ShapeDtypeStruct((n_blk * rb, D_MODEL // 2), jnp.uint32),
        compiler_params=pltpu.CompilerParams(dimension_semantics=("arbitrary",)),
        name="dispatch",
    )(pad_base, pad_n, n_used, pos_tiles, hpk_all)


def _experts_kernel(blk_e_ref, n_used_ref, next_e_ref, x_ref, wg_hbm, wu_hbm, wd_hbm, y_ref,
                    wg_s, wu_s, wd_s, wsem, wg_b, wu_b, wd_b, prev_e):
    i = pl.program_id(0)
    n_used = n_used_ref[0]

    def weight_copies(e):
        return (pltpu.make_async_copy(wg_hbm.at[e], wg_s, wsem.at[0]),
                pltpu.make_async_copy(wu_hbm.at[e], wu_s, wsem.at[1]),
                pltpu.make_async_copy(wd_hbm.at[e], wd_s, wsem.at[2]))

    @pl.when(i == 0)
    def _():
        prev_e[0] = -1
        for c in weight_copies(blk_e_ref[0]):
            c.start()

    @pl.when(i < n_used)
    def _():
        e = blk_e_ref[i]

        @pl.when(e != prev_e[0])
        def _():
            for c in weight_copies(e):
                c.wait()
            wg_b[...] = wg_s[...].astype(BF16)
            wu_b[...] = wu_s[...].astype(BF16)
            wd_b[...] = wd_s[...].astype(BF16)
            prev_e[0] = e
            nxt = next_e_ref[i]

            @pl.when(nxt >= 0)
            def _():
                for c in weight_copies(nxt):
                    c.start()

        xb = _unpack_bf16_pairs(x_ref[...]).astype(BF16)
        hid = (_silu(_dot(xb, wg_b[...])) * _dot(xb, wu_b[...])).astype(BF16)
        y = _dot(hid, wd_b[...])
        y_ref[...] = _pack_bf16_pairs(y.astype(BF16).astype(F32))

    @pl.when(i >= n_used)
    def _():
        y_ref[...] = jnp.zeros(y_ref.shape, jnp.uint32)


def _experts(xs_rows, blk_e, n_used, next_e, w_gate, w_up, w_down, *, rb):
    n_blk = blk_e.shape[0]
    n_rows = n_blk * rb
    hbm = pl.BlockSpec(memory_space=pl.ANY)
    grid_spec = pltpu.PrefetchScalarGridSpec(
        num_scalar_prefetch=3,
        grid=(n_blk,),
        in_specs=[pl.BlockSpec((rb, D_MODEL // 2), lambda i, be, nu, ne: (jnp.minimum(i, nu[0] - 1), 0)),
                  hbm, hbm, hbm],
        out_specs=pl.BlockSpec((rb, D_MODEL // 2), lambda i, be, nu, ne: (i, 0)),
        scratch_shapes=[pltpu.VMEM((D_MODEL, D_EXPERT), F32), pltpu.VMEM((D_MODEL, D_EXPERT), F32),
                        pltpu.VMEM((D_EXPERT, D_MODEL), F32), pltpu.SemaphoreType.DMA((3,)),
                        pltpu.VMEM((D_MODEL, D_EXPERT), BF16), pltpu.VMEM((D_MODEL, D_EXPERT), BF16),
                        pltpu.VMEM((D_EXPERT, D_MODEL), BF16), pltpu.SMEM((1,), jnp.int32)],
    )
    return pl.pallas_call(
        _experts_kernel,
        grid_spec=grid_spec,
        out_shape=jax.ShapeDtypeStruct((n_rows, D_MODEL // 2), jnp.uint32),
        compiler_params=pltpu.CompilerParams(dimension_semantics=("arbitrary",),
                                             vmem_limit_bytes=VMEM_LIMIT),
        name="experts",
    )(blk_e, n_used, next_e, xs_rows, w_gate, w_up, w_down)


def _combine_kernel(pos_ref, pos_n1_ref, pos_n2_ref, y_hbm, h_ref, route_ref, g2_ref, b2_ref, op_ref, os_ref,
                    ybuf, sem, *, tm, n_prompt_tiles):
    i = pl.program_id(0)
    n = pl.num_programs(0)
    slot = i % GATHER_SLOTS
    rows = TOP_K * tm

    def row_copy(idx, dst_slot, r):
        return pltpu.make_async_copy(y_hbm.at[pl.ds(idx, 1)], ybuf.at[dst_slot, pl.ds(r, 1)],
                                     sem.at[dst_slot])

    def wait_tile(s):
        pltpu.make_async_copy(y_hbm.at[pl.ds(0, rows)], ybuf.at[s], sem.at[s]).wait()

    @pl.when(i == 0)
    def _():
        def body(r, c):
            row_copy(pos_ref[r], 0, r).start()
            row_copy(pos_n1_ref[r], 1, r).start()
            return c
        lax.fori_loop(0, rows, body, 0)

    wait_tile(slot)
    gate1 = route_ref[:, 2:3]
    gate2 = route_ref[:, 3:4]
    f = (_unpack_bf16_pairs(ybuf[slot, 0:tm, :]) * gate1
         + _unpack_bf16_pairs(ybuf[slot, tm:rows, :]) * gate2)
    ahead = (i + GATHER_AHEAD) % GATHER_SLOTS
    for r in range(rows):
        row_copy(pos_n2_ref[r], ahead, r).start(priority=r % 2)
    out = _layernorm(ALPHA * h_ref[...] + f, g2_ref[...], b2_ref[...])

    @pl.when(i < n_prompt_tiles)
    def _():
        op_ref[...] = out

    @pl.when(i >= n_prompt_tiles)
    def _():
        os_ref[...] = out

    @pl.when(i == n - 1)
    def _():
        wait_tile((i + 1) % GATHER_SLOTS)
        wait_tile(ahead)


def _combine(y_rows, pos_tiles, h_all, route, pw, *, tm, t_prompt):
    T = h_all.shape[0]
    n = T // tm
    npt = t_prompt // tm
    rows = TOP_K * tm
    return pl.pallas_call(
        functools.partial(_combine_kernel, tm=tm, n_prompt_tiles=npt),
        grid=(n,),
        in_specs=[
            pl.BlockSpec((rows,), lambda i: (i,), memory_space=pltpu.SMEM),
            pl.BlockSpec((rows,), lambda i: (jnp.minimum(i + 1, n - 1),), memory_space=pltpu.SMEM),
            pl.BlockSpec((rows,), lambda i: (jnp.minimum(i + GATHER_AHEAD, n - 1),), memory_space=pltpu.SMEM),
            pl.BlockSpec(memory_space=pl.ANY),
            pl.BlockSpec((tm, D_MODEL), lambda i: (i, 0)),
            pl.BlockSpec((tm, LANES), lambda i: (i, 0)),
            _const_spec((1, D_MODEL)), _const_spec((1, D_MODEL)),
        ],
        out_specs=[pl.BlockSpec((tm, D_MODEL), lambda i: (jnp.minimum(i, npt - 1), 0)),
                   pl.BlockSpec((tm, D_MODEL), lambda i: (jnp.maximum(i - npt, 0), 0))],
        out_shape=[jax.ShapeDtypeStruct((t_prompt, D_MODEL), F32),
                   jax.ShapeDtypeStruct((T - t_prompt, D_MODEL), F32)],
        scratch_shapes=[pltpu.VMEM((GATHER_SLOTS, rows, D_MODEL // 2), jnp.uint32),
                        pltpu.SemaphoreType.DMA((GATHER_SLOTS,))],
        compiler_params=pltpu.CompilerParams(dimension_semantics=("arbitrary",),
                                             vmem_limit_bytes=VMEM_LIMIT),
        name="combine",
    )(pos_tiles, pos_tiles, pos_tiles, y_rows, h_all, route, pw["ln2_g"], pw["ln2_b"])


def _prep_weights(ln0_g, ln0_b, w_in, q_norm, w_uq, kv_norm, w_uk, w_uv, conv_w, conv_b, dt_bias, a_log,
                  d_skip, ssm_norm, w_o, ln1_g, ln1_b, w_rg, b_rg, w_re, b_re, ln2_g, ln2_b):
    s_q, s_kv, s_kr, s_z, s_xbc = Q_LORA, Q_LORA + KV_LORA, Q_LORA + KV_LORA + QK_ROPE, \
        Q_LORA + KV_LORA + QK_ROPE + D_SSM, Q_LORA + KV_LORA + QK_ROPE + D_SSM + CONV_CH
    zc = lambda n: jnp.zeros((D_MODEL, n), F32)
    w_in_p = jnp.concatenate([
        w_in[:, :s_q], w_in[:, s_q:s_kv], w_in[:, s_kr:s_z], w_in[:, s_z:s_xbc],
        w_in[:, s_kv:s_kr], zc(LANES - QK_ROPE), w_in[:, s_xbc:], zc(LANES - SSM_HEADS)], axis=1)
    wq = w_uq.reshape(Q_LORA, MLA_HEADS, QK_DIM) * (ATTN_SCALE * math.log2(math.e))
    wq = jnp.concatenate([wq, jnp.zeros((Q_LORA, MLA_HEADS, QK_PAD - QK_DIM), F32)], axis=2)
    w_r = jnp.concatenate([w_rg, w_re, jnp.zeros((D_MODEL, LANES - R_OFF - N_EXPERTS), F32)], axis=1)
    w_r_hi = w_r.astype(BF16)
    row = lambda v: v.reshape(1, -1)
    pad_row = lambda v: jnp.pad(v, (0, LANES - v.shape[0])).reshape(1, LANES)
    return dict(
        ln0_g=row(ln0_g), ln0_b=row(ln0_b), w_in=w_in_p.astype(BF16),
        q_norm=row(q_norm), kv_norm=row(kv_norm),
        w_uq=wq.reshape(Q_LORA, MLA_HEADS * QK_PAD).astype(BF16),
        w_ukv=jnp.concatenate([w_uk.reshape(KV_LORA, D_ATTN), w_uv.reshape(KV_LORA, D_ATTN)],
                              axis=1).astype(BF16),
        w_ukt=jnp.transpose(w_uk, (1, 2, 0)).astype(BF16),
        w_uvh=jnp.transpose(w_uv, (1, 0, 2)).astype(BF16),
        conv_w=conv_w, conv_b=row(conv_b), dt_bias=pad_row(dt_bias),
        a_neg=pad_row(-jnp.exp(a_log)), d_skip=row(jnp.repeat(d_skip, SSM_HEAD_DIM)),
        ssm_norm=row(ssm_norm),
        w_o_a=w_o[:D_ATTN].astype(BF16), w_o_s=w_o[D_ATTN:].astype(BF16),
        ln1_g=row(ln1_g), ln1_b=row(ln1_b),
        w_r=jnp.concatenate([w_r_hi, (w_r - w_r_hi.astype(F32)).astype(BF16)], axis=1),
        b_r=pad_row(jnp.concatenate([b_rg, b_re])),
        ln2_g=row(ln2_g), ln2_b=row(ln2_b),
    )


def _rope_tables(pos):
    half = QK_ROPE // 2
    inv_freq = ROPE_THETA ** (-jnp.arange(half, dtype=F32) / half)
    ang = pos.astype(F32)[:, None] * inv_freq[None, :]
    cos, sin = jnp.cos(ang), jnp.sin(ang)
    zeros = jnp.zeros((pos.shape[0], LANES - QK_ROPE), F32)
    return (jnp.concatenate([cos, cos, zeros], axis=1), jnp.concatenate([-sin, sin, zeros], axis=1))


def _dispatch_tables(route, counts, *, rb, n_blk):
    experts = jnp.arange(N_EXPERTS, dtype=jnp.int32)
    e = route[:, 0:TOP_K].astype(jnp.int32)
    rank = route[:, 4:4 + TOP_K].astype(jnp.int32)
    counts = counts.astype(jnp.int32)
    pcounts = (counts + rb - 1) // rb * rb
    pend = jnp.cumsum(pcounts)
    pstart = pend - pcounts
    pos = jnp.sum(jnp.where(e[:, :, None] == experts, pstart, 0), axis=2) + rank
    blk_start = jnp.arange(n_blk, dtype=jnp.int32) * rb
    blk_e = jnp.minimum(jnp.sum((pend[None, :] <= blk_start[:, None]).astype(jnp.int32), axis=1),
                        N_EXPERTS - 1)
    n_used = (pend[-1] // rb).reshape(1)
    blk_idx = jnp.arange(n_blk, dtype=jnp.int32)
    later = (blk_idx[None, :] > blk_idx[:, None]) & (blk_idx[None, :] < n_used) & (blk_e[None, :] > blk_e[:, None])
    next_e = jnp.min(jnp.where(later, blk_e[None, :], N_EXPERTS), axis=1)
    next_e = jnp.where(next_e < N_EXPERTS, next_e, -1).astype(jnp.int32)
    pad_base = pstart + counts
    pad_n = pcounts - counts
    return blk_e, n_used, next_e, pos, pad_base, pad_n


def _largest_tile(n, cap):
    t = cap
    while n % t:
        t //= 2
    return t


def kernel(x_prompt, x_sample, cache_kv_latent, cache_k_rope, state_conv, state_ssm, ln0_g, ln0_b, w_in, q_norm, w_uq, kv_norm, w_uk, w_uv, conv_w, conv_b, dt_bias, a_log, d_skip, ssm_norm, w_o, ln1_g, ln1_b, w_rg, b_rg, w_re, b_re, w_gate, w_up, w_down, ln2_g, ln2_b):
    B, L, _ = x_prompt.shape
    Bs, Ls, _ = x_sample.shape
    past = cache_kv_latent.shape[2]
    Tp, Ts = B * L, Bs * Ls
    pw = _prep_weights(ln0_g, ln0_b, w_in[0], q_norm[0], w_uq[0], kv_norm[0], w_uk[0], w_uv[0], conv_w[0],
                       conv_b[0], dt_bias[0], a_log[0], d_skip[0], ssm_norm[0], w_o[0], ln1_g[0], ln1_b[0],
                       w_rg[0], b_rg[0], w_re[0], b_re[0], ln2_g[0], ln2_b[0])

    tm_p = _largest_tile(L, 512)
    xp = x_prompt.reshape(Tp, D_MODEL)
    cos_p, sin_p = _rope_tables(jnp.arange(L, dtype=jnp.int32))
    q_p, kv_p, lat_p, kr_p, krp_p, z_p, xbc_p, dt_p = _inproj(
        xp, pw, cos_p, sin_p, tm=tm_p, pos_blocks=L // tm_p, with_kv=True)
    attn_p = _attn_prompt(q_p.reshape(B, L, -1), kv_p.reshape(B, L, -1), krp_p.reshape(B, L, -1),
                          tq=_largest_tile(L, 1024), tk=_largest_tile(L, 1024))
    q_ssd = min(CHUNK, L)
    y_p, conv_p, ssm_p = _ssd(
        xbc_p.reshape(B, L, -1), dt_p.reshape(B, L, -1), z_p.reshape(B, L, -1),
        jnp.zeros((B, CONV_K - 1, CONV_CH), F32), jnp.zeros((B, D_SSM, D_STATE), F32), pw,
        q=q_ssd, tt=_largest_tile(L, 512))

    xs = x_sample.reshape(Ts, D_MODEL)
    cos_s, sin_s = _rope_tables(past + jnp.arange(Ls, dtype=jnp.int32))
    tm_s = _largest_tile(Ts, 256)
    assert tm_s % Ls == 0
    q_s, _, lat_s, kr_s, krp_s, z_s, xbc_s, dt_s = _inproj(
        xs, pw, jnp.tile(cos_s, (tm_s // Ls, 1)), jnp.tile(sin_s, (tm_s // Ls, 1)),
        tm=tm_s, pos_blocks=1, with_kv=False)
    attn_s = _attn_sample(q_s.reshape(Bs, Ls, -1), pw, cache_kv_latent[0], cache_k_rope[0],
                          lat_s.reshape(Bs, Ls, -1), krp_s.reshape(Bs, Ls, -1),
                          tk=_largest_tile(past, 1024))
    y_s, conv_s, ssm_s = _ssd(
        xbc_s.reshape(Bs, Ls, -1), dt_s.reshape(Bs, Ls, -1), z_s.reshape(Bs, Ls, -1),
        state_conv[0], state_ssm[0].reshape(Bs, D_SSM, D_STATE), pw, q=min(CHUNK, Ls), tt=Ls)

    T = Tp + Ts
    tm_c = _largest_tile(math.gcd(Tp, Ts), 256)
    h_all, hpk_all, route, counts = _outproj(xp, xs, attn_p.reshape(Tp, -1), attn_s.reshape(Ts, -1),
                                             y_p.reshape(Tp, -1), y_s.reshape(Ts, -1), pw, tm=tm_c)
    rb = 256
    n_blk = (T * TOP_K + N_EXPERTS * (rb - 1) + rb - 1) // rb
    blk_e, n_used, next_e, pos, pad_base, pad_n = _dispatch_tables(
        route, counts[0, R_OFF:R_OFF + N_EXPERTS], rb=rb, n_blk=n_blk)
    pos_tiles = pos.reshape(T // tm_c, tm_c, TOP_K).transpose(0, 2, 1).reshape(-1)
    xs_rows = _dispatch(hpk_all, pos_tiles, pad_base, pad_n, n_used, tm=tm_c, rb=rb, n_blk=n_blk)
    y_rows = _experts(xs_rows, blk_e, n_used, next_e, w_gate[0], w_up[0], w_down[0], rb=rb)
    out_p, out_s = _combine(y_rows, pos_tiles, h_all, route, pw, tm=tm_c, t_prompt=Tp)

    return (out_p.reshape(B, L, D_MODEL), out_s.reshape(Bs, Ls, D_MODEL),
            lat_p.reshape(1, B, L, KV_LORA), lat_s.reshape(1, Bs, Ls, KV_LORA),
            kr_p.reshape(1, B, L, QK_ROPE), kr_s.reshape(1, Bs, Ls, QK_ROPE),
            conv_p[None], conv_s[None],
            ssm_p.reshape(1, B, SSM_HEADS, SSM_HEAD_DIM, D_STATE),
            ssm_s.reshape(1, Bs, SSM_HEADS, SSM_HEAD_DIM, D_STATE))
```
